```python
import math
import jax
import jax.numpy as jnp
from jax import lax
import numpy as np

D_MODEL = 2048
BATCH = 4
SEQ = 4096
DEPTH = 1

N_META = 16
HEAD_DIM = 128
GDN_HEADS = D_MODEL // (2 * HEAD_DIM)
GDN_DK = HEAD_DIM
GDN_DV = HEAD_DIM
GDN_DIM = GDN_HEADS * HEAD_DIM
CONV_K = 4
GDN_CHUNK = 64
SB_HEADS = D_MODEL // (2 * HEAD_DIM)
SB_HEAD_DIM = HEAD_DIM
SB_DIM = SB_HEADS * SB_HEAD_DIM
SB_BLOCK = 128
MIX_DIM = GDN_DIM + SB_DIM
IN_PROJ_DIM = 4 * GDN_DIM + 2 * GDN_HEADS + 3 * SB_DIM
N_EXPERTS = 32
TOP_K = 4
D_EXPERT = D_MODEL
SWIGLU_LIMIT = 7.0
SWIGLU_ALPHA = 1.702
MOE_BLOCK = 256
NORM_EPS = 1e-6

kernel_name = "hymba_gdn_stickbreaking_moe_layer"


def rms_norm(x, w):
    xf = x.astype(jnp.float32)
    y = xf * lax.rsqrt(jnp.mean(xf * xf, axis=-1, keepdims=True) + NORM_EPS)
    return y * w.astype(jnp.float32)


def l2_norm(x):
    return x * lax.rsqrt(jnp.sum(x * x, axis=-1, keepdims=True) + NORM_EPS)


def causal_depthwise_conv(x, w):
    k, c = w.shape
    return lax.conv_general_dilated(
        x, w[:, None, :].astype(x.dtype), window_strides=(1,), padding=[(k - 1, 0)],
        dimension_numbers=('NWC', 'WIO', 'NWC'), feature_group_count=c)


def chunked_gated_delta(q, k, v, beta, g):
    bsz, nh, t_len, dk = q.shape
    dv = v.shape[-1]
    c = GDN_CHUNK
    n = t_len // c
    q = q.reshape(bsz, nh, n, c, dk)
    k = k.reshape(bsz, nh, n, c, dk)
    v = v.reshape(bsz, nh, n, c, dv)
    beta = beta.reshape(bsz, nh, n, c)
    g = jnp.cumsum(g.reshape(bsz, nh, n, c), axis=-1)
    incl = jnp.tril(jnp.ones((c, c), dtype=bool))
    strict = jnp.tril(jnp.ones((c, c), dtype=bool), -1)
    decay = jnp.exp(jnp.where(incl, g[..., :, None] - g[..., None, :], -jnp.inf))
    k_beta = k * beta[..., None]
    v_beta = v * beta[..., None]
    low = jnp.where(strict, jnp.einsum('bhncd,bhnsd->bhncs', k_beta, k) * decay, 0.0)
    eye = jnp.eye(c, dtype=q.dtype)
    rhs = jnp.concatenate([v_beta, k_beta * jnp.exp(g)[..., None]], axis=-1)
    sol = lax.linalg.triangular_solve(eye + low, rhs, left_side=True, lower=True,
                                      unit_diagonal=True)
    u = sol[..., :dv]
    w = sol[..., dv:]
    attn = jnp.where(incl, jnp.einsum('bhncd,bhnsd->bhncs', q, k) * decay, 0.0)

    def step(state, inp):
        qi, ki, ui, wi, ai, gi = inp
        v_new = ui - jnp.einsum('bhck,bhkv->bhcv', wi, state)
        o = (jnp.einsum('bhck,bhkv->bhcv', qi * jnp.exp(gi)[..., None], state)
             + jnp.einsum('bhcs,bhsv->bhcv', ai, v_new))
        g_last = gi[..., -1]
        k_dec = ki * jnp.exp(g_last[..., None] - gi)[..., None]
        state = (state * jnp.exp(g_last)[..., None, None]
                 + jnp.einsum('bhck,bhcv->bhkv', k_dec, v_new))
        return state, o

    to_chunk_major = lambda t: jnp.moveaxis(t, 2, 0)
    s0 = jnp.zeros((bsz, nh, dk, dv), dtype=q.dtype)
    _, o = lax.scan(step, s0, tuple(to_chunk_major(t) for t in (q, k, u, w, attn, g)))
    return jnp.moveaxis(o, 0, 2).reshape(bsz, nh, t_len, dv)


def gated_deltanet_group(qkv, z, b, a, conv_w, a_log, dt_bias, norm_w):
    bsz, seq_len, _ = qkv.shape
    f32 = jnp.float32
    qkv = jax.nn.silu(causal_depthwise_conv(qkv, conv_w).astype(f32))
    q, k, v = jnp.split(qkv, 3, axis=-1)
    heads = lambda t: t.reshape(bsz, seq_len, GDN_HEADS, HEAD_DIM).transpose(0, 2, 1, 3)
    q = l2_norm(heads(q)) * (GDN_DK ** -0.5)
    k = l2_norm(heads(k))
    v = heads(v)
    beta = jax.nn.sigmoid(b.astype(f32)).transpose(0, 2, 1)
    g = (-jnp.exp(a_log.astype(f32))
         * jax.nn.softplus(a.astype(f32) + dt_bias.astype(f32))).transpose(0, 2, 1)
    pad = (-seq_len) % GDN_CHUNK
    pad_t = lambda t: jnp.pad(t, [(0, 0), (0, 0), (pad, 0)] + [(0, 0)] * (t.ndim - 3))
    o = chunked_gated_delta(pad_t(q), pad_t(k), pad_t(v), pad_t(beta), pad_t(g))[:, :, pad:]
    o = o.transpose(0, 2, 1, 3)
    zf = z.astype(f32).reshape(bsz, seq_len, GDN_HEADS, GDN_DV)
    o = rms_norm(o, norm_w) * jax.nn.silu(zf)
    return o.reshape(bsz, seq_len, GDN_DIM)


def stick_breaking_group(q, k, v, q_norm_w, k_norm_w, out_norm_w):
    bsz, seq_len, _ = q.shape
    heads = lambda t: t.reshape(bsz, seq_len, SB_HEADS, SB_HEAD_DIM)
    pad = (-seq_len) % SB_BLOCK
    to_bhld = lambda t: jnp.pad(t.transpose(0, 2, 1, 3), [(0, 0), (0, 0), (pad, 0), (0, 0)])
    qh = to_bhld(rms_norm(heads(q), q_norm_w))
    kh = to_bhld(rms_norm(heads(k), k_norm_w))
    vh = to_bhld(heads(v).astype(jnp.float32))
    padded_len = seq_len + pad
    scale = SB_HEAD_DIM ** -0.5
    outs = []
    for blk in range(padded_len // SB_BLOCK):
        q0 = blk * SB_BLOCK
        q1 = q0 + SB_BLOCK
        z = jnp.einsum('bhqd,bhkd->bhqk', qh[:, :, q0:q1], kh[:, :, :q1]) * scale
        q_pos = jnp.arange(q0, q1)[:, None]
        k_pos = jnp.arange(q1)[None, :]
        visible = (k_pos < q_pos) & (k_pos >= pad)
        log_beta = jnp.where(visible, jax.nn.log_sigmoid(z), -jnp.inf)
        log_keep = jnp.where(visible, jax.nn.log_sigmoid(-z), 0.0)
        log_keep_after = lax.cumsum(log_keep, axis=3, reverse=True) - log_keep
        weight = jnp.exp(log_beta + log_keep_after)
        outs.append(jnp.einsum('bhqk,bhkd->bhqd', weight, vh[:, :, :q1]))
    o = jnp.concatenate(outs, axis=2)[:, :, pad:].transpose(0, 2, 1, 3)
    return rms_norm(o, out_norm_w).reshape(bsz, seq_len, SB_DIM)


def moe_ffn(h, router_w, router_b, w_gate_up, b_gate_up, w_down, b_down):
    n_tok, d = h.shape
    logits = (h @ router_w).astype(jnp.float32) + router_b.astype(jnp.float32)
    top_logit, top_idx = lax.top_k(logits, TOP_K)
    gate = jax.nn.softmax(top_logit, axis=-1)
    n_assign = n_tok * TOP_K
    flat_e = top_idx.reshape(-1)
    order = jnp.argsort(flat_e)
    sorted_e = flat_e[order]
    sorted_tok = order // TOP_K
    sorted_gate = gate.reshape(-1)[order]
    counts = jnp.bincount(flat_e, length=N_EXPERTS)
    padded = (counts + MOE_BLOCK - 1) // MOE_BLOCK * MOE_BLOCK
    start = jnp.cumsum(counts) - counts
    pstart = jnp.cumsum(padded) - padded
    slot = pstart[sorted_e] + jnp.arange(n_assign) - start[sorted_e]
    n_blocks = (n_assign + N_EXPERTS * (MOE_BLOCK - 1)) // MOE_BLOCK + 1
    n_slots = n_blocks * MOE_BLOCK
    slot_tok = jnp.full((n_slots,), n_tok, dtype=jnp.int32).at[slot].set(sorted_tok.astype(jnp.int32))
    slot_gate = jnp.zeros((n_slots,), jnp.float32).at[slot].set(sorted_gate)
    block_e = jnp.searchsorted(jnp.cumsum(padded), jnp.arange(n_blocks) * MOE_BLOCK, side='right')
    block_e = jnp.minimum(block_e, N_EXPERTS - 1)
    h_ext = jnp.concatenate([h, jnp.zeros((1, d), h.dtype)], axis=0)
    xb = h_ext[slot_tok].reshape(n_blocks, MOE_BLOCK, d)

    def expert_block(args):
        xe, e = args
        gu = xe @ w_gate_up[e] + b_gate_up[e]
        g_h, u_h = jnp.split(gu, 2, axis=-1)
        g_h = jnp.minimum(g_h, SWIGLU_LIMIT)
        u_h = jnp.clip(u_h, -SWIGLU_LIMIT, SWIGLU_LIMIT)
        act = (u_h + 1.0) * g_h * jax.nn.sigmoid(SWIGLU_ALPHA * g_h)
        return act @ w_down[e] + b_down[e]

    yb = lax.map(expert_block, (xb, block_e))
    y = yb.reshape(n_slots, d) * slot_gate[:, None].astype(yb.dtype)
    return jnp.zeros((n_tok + 1, d), y.dtype).at[slot_tok].add(y)[:n_tok]


def setup_inputs(seed: int = 0) -> dict:
    key = jax.random.key(seed)
    ks = jax.random.split(key, 19)
    nrm = jax.random.normal
    D = D_MODEL
    x = nrm(ks[0], (BATCH, SEQ, D), jnp.float32)
    meta_tokens = nrm(ks[1], (N_META, D), jnp.float32)
    mix_norm_w = 1.0 + 0.02 * nrm(ks[2], (DEPTH, D), jnp.float32)
    w_in = nrm(ks[3], (DEPTH, D, IN_PROJ_DIM), jnp.float32) * D ** -0.5
    conv_w = nrm(ks[4], (DEPTH, CONV_K, 3 * GDN_DIM), jnp.float32) * CONV_K ** -0.5
    a_log = jnp.log(jax.random.uniform(ks[5], (DEPTH, GDN_HEADS), jnp.float32, 1.0, 16.0))
    dt = jnp.exp(jax.random.uniform(ks[6], (DEPTH, GDN_HEADS), jnp.float32,
                                    math.log(1e-3), math.log(1e-1)))
    dt_bias = dt + jnp.log(-jnp.expm1(-dt))
    gdn_norm_w = 1.0 + 0.02 * nrm(ks[7], (DEPTH, GDN_DV), jnp.float32)
    sb_q_norm_w = 1.0 + 0.02 * nrm(ks[8], (DEPTH, SB_HEAD_DIM), jnp.float32)
    sb_k_norm_w = 1.0 + 0.02 * nrm(ks[9], (DEPTH, SB_HEAD_DIM), jnp.float32)
    sb_out_norm_w = 1.0 + 0.02 * nrm(ks[10], (DEPTH, SB_HEAD_DIM), jnp.float32)
    w_out = nrm(ks[11], (DEPTH, MIX_DIM, D), jnp.float32) * MIX_DIM ** -0.5
    ffn_norm_w = 1.0 + 0.02 * nrm(ks[12], (DEPTH, D), jnp.float32)
    router_w = nrm(ks[13], (DEPTH, D, N_EXPERTS), jnp.float32) * D ** -0.5
    router_b = 0.01 * nrm(ks[14], (DEPTH, N_EXPERTS), jnp.float32)
    w_gate_up = nrm(ks[15], (DEPTH, N_EXPERTS, D, 2 * D_EXPERT), jnp.float32) * D ** -0.5
    b_gate_up = 0.01 * nrm(ks[16], (DEPTH, N_EXPERTS, 2 * D_EXPERT), jnp.float32)
    w_down = nrm(ks[17], (DEPTH, N_EXPERTS, D_EXPERT, D), jnp.float32) * D_EXPERT ** -0.5
    b_down = 0.01 * nrm(ks[18], (DEPTH, N_EXPERTS, D), jnp.float32)
    return {"x": x, "meta_tokens": meta_tokens, "mix_norm_w": mix_norm_w, "w_in": w_in,
            "conv_w": conv_w, "a_log": a_log, "dt_bias": dt_bias, "gdn_norm_w": gdn_norm_w,
            "sb_q_norm_w": sb_q_norm_w, "sb_k_norm_w": sb_k_norm_w, "sb_out_norm_w": sb_out_norm_w,
            "w_out": w_out, "ffn_norm_w": ffn_norm_w, "router_w": router_w, "router_b": router_b,
            "w_gate_up": w_gate_up, "b_gate_up": b_gate_up, "w_down": w_down, "b_down": b_down}


def reference(x, meta_tokens, mix_norm_w, w_in, conv_w, a_log, dt_bias, gdn_norm_w,
              sb_q_norm_w, sb_k_norm_w, sb_out_norm_w, w_out, ffn_norm_w, router_w, router_b,
              w_gate_up, b_gate_up, w_down, b_down):
    bsz = x.shape[0]
    meta = jnp.broadcast_to(meta_tokens.astype(x.dtype)[None], (bsz, N_META, D_MODEL))
    h = jnp.concatenate([meta, x], axis=1)
    cuts = [3 * GDN_DIM, 4 * GDN_DIM, 4 * GDN_DIM + GDN_HEADS, 4 * GDN_DIM + 2 * GDN_HEADS,
            4 * GDN_DIM + 2 * GDN_HEADS + SB_DIM, 4 * GDN_DIM + 2 * GDN_HEADS + 2 * SB_DIM]
    for layer in range(DEPTH):
        hn = rms_norm(h, mix_norm_w[layer]).astype(h.dtype)
        proj = hn @ w_in[layer]
        gdn_qkv, gdn_z, gdn_b, gdn_a, sb_q, sb_k, sb_v = jnp.split(proj, cuts, axis=-1)
        o_gdn = gated_deltanet_group(gdn_qkv, gdn_z, gdn_b, gdn_a, conv_w[layer], a_log[layer],
                                     dt_bias[layer], gdn_norm_w[layer])
        o_sb = stick_breaking_group(sb_q, sb_k, sb_v, sb_q_norm_w[layer], sb_k_norm_w[layer],
                                    sb_out_norm_w[layer])
        mixed = jnp.concatenate([o_gdn, o_sb], axis=-1).astype(h.dtype) @ w_out[layer]
        h = h + mixed
        hn = rms_norm(h, ffn_norm_w[layer]).astype(h.dtype)
        ffn = moe_ffn(hn.reshape(-1, D_MODEL), router_w[layer], router_b[layer], w_gate_up[layer],
                      b_gate_up[layer], w_down[layer], b_down[layer])
        h = h + ffn.reshape(h.shape).astype(h.dtype)
    return h[:, N_META:]
```

```python
import functools

import jax
import jax.numpy as jnp
from jax import lax
from jax.experimental import pallas as pl
from jax.experimental.pallas import tpu as pltpu

F32 = jnp.float32
BF16 = jnp.bfloat16
U32 = jnp.uint32
I32 = jnp.int32

NORM_EPS = 1e-6
TOP_K = 4
SWIGLU_LIMIT = 7.0
SWIGLU_ALPHA = 1.702
LANES = 128
ROW_ALIGN = 256
GDN_CHUNK = 64
SB_BLOCK = 256
MIX_ROWS = 256
EXP_SUB = 256
EXP_UNIT = 1024
EXP_FTILE = 256
COMB_ROWS = 128
VMEM_LIMIT = 56 * 1024 * 1024


def _cparams(n_grid, vmem=VMEM_LIMIT):
    return pltpu.CompilerParams(dimension_semantics=("arbitrary",) * n_grid, vmem_limit_bytes=vmem)


def _dot(a, b):
    return jnp.dot(a, b, preferred_element_type=F32)


def _dot_nt(a, b):
    return lax.dot_general(a, b, (((1,), (1,)), ((), ())), preferred_element_type=F32)


def _dot_tn(a, b):
    return lax.dot_general(a, b, (((0,), (0,)), ((), ())), preferred_element_type=F32)


def _split3(x):
    hi = x.astype(BF16)
    r1 = x - hi.astype(F32)
    mid = r1.astype(BF16)
    lo = (r1 - mid.astype(F32)).astype(BF16)
    return hi, mid, lo


def _dot_exact_rhs01(x, t01):
    hi, mid, lo = _split3(x)
    return _dot(hi, t01) + _dot(mid, t01) + _dot(lo, t01)


def _dot_exact_lhs01(t01, x):
    hi, mid, lo = _split3(x)
    return _dot(t01, hi) + _dot(t01, mid) + _dot(t01, lo)


def _dot3(a, b):
    ah, am, al = _split3(a)
    bh, bm, bl = _split3(b)
    return (_dot(ah, bh) + (_dot(ah, bm) + _dot(am, bh))
            + (_dot(am, bm) + _dot(ah, bl) + _dot(al, bh)))


def _sigmoid(x):
    return 1.0 / (1.0 + jnp.exp(-x))


def _softplus(x):
    return jnp.maximum(x, 0.0) + jnp.log1p(jnp.exp(-jnp.abs(x)))


def _inproj_kernel(h_ref, nw_ref, w_ref, wba_ref, o_ref, ba_ref, xn_ref):
    @pl.when(pl.program_id(1) == 0)
    def _():
        x = h_ref[...]
        xn = x * lax.rsqrt(jnp.mean(x * x, axis=-1, keepdims=True) + NORM_EPS) * nw_ref[...]
        xn = xn.astype(BF16)
        xn_ref[...] = xn
        ba_ref[...] = _dot(xn, wba_ref[...])

    o_ref[...] = _dot(xn_ref[...], w_ref[...]).astype(o_ref.dtype)


def _in_proj(h0, norm_w, w_main, w_ba, tm, tn):
    rows, d = h0.shape
    n_main = w_main.shape[1]
    return pl.pallas_call(
        _inproj_kernel,
        grid=(rows // tm, n_main // tn),
        in_specs=[
            pl.BlockSpec((tm, d), lambda i, n: (i, 0)),
            pl.BlockSpec((1, d), lambda i, n: (0, 0)),
            pl.BlockSpec((d, tn), lambda i, n: (0, n)),
            pl.BlockSpec((d, LANES), lambda i, n: (0, 0)),
        ],
        out_specs=[
            pl.BlockSpec((tm, tn), lambda i, n: (i, n)),
            pl.BlockSpec((tm, LANES), lambda i, n: (i, 0)),
        ],
        out_shape=[
            jax.ShapeDtypeStruct((rows, n_main), BF16),
            jax.ShapeDtypeStruct((rows, LANES), F32),
        ],
        scratch_shapes=[pltpu.VMEM((tm, d), BF16)],
        compiler_params=_cparams(2),
        name="in_proj",
    )(h0, norm_w, w_main, w_ba)


def _gdn_kernel(qkv_ref, z_ref, ba_ref, cw_ref, gp_ref, nw_ref, o_ref, cbuf, state, *, heads, hd):
    c_rows = GDN_CHUNK
    gd = heads * hd
    taps = cw_ref.shape[0]

    @pl.when(pl.program_id(1) == 0)
    def _():
        cbuf[0:8, :] = jnp.zeros((8, 3 * gd), F32)
        state[...] = jnp.zeros_like(state)

    x = qkv_ref[...].astype(F32)
    cbuf[8:8 + c_rows, :] = x
    y = jnp.zeros((c_rows, 3 * gd), F32)
    for i in range(taps):
        y = y + cw_ref[i:i + 1, :] * cbuf[pl.ds(8 - (taps - 1) + i, c_rows), :]
    cbuf[0:8, :] = x[c_rows - 8:c_rows, :]
    y = y * _sigmoid(y)

    ba = ba_ref[...]
    beta_all = _sigmoid(ba)
    g_all = -jnp.exp(gp_ref[0:1, :]) * _softplus(ba + gp_ref[1:2, :])
    r_i = lax.broadcasted_iota(I32, (c_rows, c_rows), 0)
    c_i = lax.broadcasted_iota(I32, (c_rows, c_rows), 1)
    incl = r_i >= c_i
    strict = r_i > c_i
    tri = incl.astype(BF16)
    gcum_all = _dot_exact_lhs01(tri, g_all)
    gcum_t = gcum_all.T

    for h in range(heads):
        q = y[:, h * hd:(h + 1) * hd]
        k = y[:, gd + h * hd:gd + (h + 1) * hd]
        v = y[:, 2 * gd + h * hd:2 * gd + (h + 1) * hd]
        q = q * lax.rsqrt(jnp.sum(q * q, axis=-1, keepdims=True) + NORM_EPS) * (hd ** -0.5)
        k = k * lax.rsqrt(jnp.sum(k * k, axis=-1, keepdims=True) + NORM_EPS)
        beta = beta_all[:, h:h + 1]
        gc = gcum_all[:, heads + h:heads + h + 1]
        gr = gcum_t[heads + h:heads + h + 1, :]
        decay = jnp.exp(jnp.minimum(gc - gr, 0.0))
        kb = k * beta
        vb = v * beta
        low = jnp.where(strict, _dot3_nt(kb, k) * decay, 0.0)
        rhs = jnp.concatenate([vb, kb * jnp.exp(gc)], axis=1)
        p = -low
        sol = rhs
        n_fac = c_rows.bit_length() - 1
        for i in range(n_fac):
            sol = sol + _dot3(p, sol)
            if i + 1 < n_fac:
                p = _dot3(p, p)
        u = sol[:, :hd]
        w = sol[:, hd:]
        attn = jnp.where(incl, _dot_nt(q.astype(BF16), k.astype(BF16)) * decay, 0.0)
        s_h = state[h]
        s_b = s_h.astype(BF16)
        v_new = u - _dot(w.astype(BF16), s_b)
        o = (_dot((q * jnp.exp(gc)).astype(BF16), s_b)
             + _dot(attn.astype(BF16), v_new.astype(BF16)))
        g_last = gc[c_rows - 1:c_rows, :]
        k_dec = k * jnp.exp(g_last - gc)
        state[h] = s_h * jnp.exp(g_last) + _dot_tn(k_dec.astype(BF16), v_new.astype(BF16))
        zf = z_ref[:, h * hd:(h + 1) * hd].astype(F32)
        o = o * lax.rsqrt(jnp.mean(o * o, axis=-1, keepdims=True) + NORM_EPS) * nw_ref[...]
        o_ref[:, h * hd:(h + 1) * hd] = (o * (zf * _sigmoid(zf))).astype(o_ref.dtype)


def _dot3_nt(a, b):
    ah, am, al = _split3(a)
    bh, bm, bl = _split3(b)
    return (_dot_nt(ah, bh) + (_dot_nt(ah, bm) + _dot_nt(am, bh))
            + (_dot_nt(am, bm) + _dot_nt(ah, bl) + _dot_nt(al, bh)))


def _gdn(proj3, ba3, conv_w, gparams, norm_w, heads, hd):
    bsz, lp, _ = proj3.shape
    gd = heads * hd
    c = GDN_CHUNK
    return pl.pallas_call(
        functools.partial(_gdn_kernel, heads=heads, hd=hd),
        grid=(bsz, lp // c),
        in_specs=[
            pl.BlockSpec((None, c, 3 * gd), lambda b, i: (b, i, 0)),
            pl.BlockSpec((None, c, gd), lambda b, i: (b, i, 3)),
            pl.BlockSpec((None, c, LANES), lambda b, i: (b, i, 0)),
            pl.BlockSpec(conv_w.shape, lambda b, i: (0, 0)),
            pl.BlockSpec((2, LANES), lambda b, i: (0, 0)),
            pl.BlockSpec((1, hd), lambda b, i: (0, 0)),
        ],
        out_specs=pl.BlockSpec((None, c, gd), lambda b, i: (b, i, 0)),
        out_shape=jax.ShapeDtypeStruct((bsz, lp, gd), BF16),
        scratch_shapes=[pltpu.VMEM((c + 8, 3 * gd), F32), pltpu.VMEM((heads, hd, hd), F32)],
        compiler_params=_cparams(2),
        name="gdn",
    )(proj3, proj3, ba3, conv_w, gparams, norm_w)


def _sb_kernel(q_ref, k_ref, v_ref, qw_ref, kw_ref, ow_ref, o_ref, kn_ref, acc_ref, carry_ref,
               *, front, hd):
    t = SB_BLOCK
    qi = pl.program_id(2)
    n_blocks = k_ref.shape[0] // t

    @pl.when(qi == 0)
    def _():
        def norm_keys(i, _):
            kb = k_ref[pl.ds(i * t, t), :].astype(F32)
            kn = kb * lax.rsqrt(jnp.mean(kb * kb, axis=-1, keepdims=True) + NORM_EPS) * kw_ref[...]
            kn_ref[pl.ds(i * t, t), :] = kn.astype(BF16)
            return 0
        lax.fori_loop(0, n_blocks, norm_keys, 0)

    q = q_ref[...].astype(F32)
    qn = q * lax.rsqrt(jnp.mean(q * q, axis=-1, keepdims=True) + NORM_EPS) * qw_ref[...]
    qn = (qn * (hd ** -0.5)).astype(BF16)

    r_i = lax.broadcasted_iota(I32, (t, t), 0)
    c_i = lax.broadcasted_iota(I32, (t, t), 1)
    later = (r_i > c_i).astype(BF16)

    acc_ref[...] = jnp.zeros_like(acc_ref)
    carry_ref[...] = jnp.zeros_like(carry_ref)

    def tile(kj, masked):
        ks = kn_ref[pl.ds(kj * t, t), :]
        vs = v_ref[pl.ds(kj * t, t), :]
        z = _dot_nt(qn, ks)
        soft = jnp.log1p(jnp.exp(-jnp.abs(z)))
        log_beta = jnp.minimum(z, 0.0) - soft
        log_keep = jnp.minimum(-z, 0.0) - soft
        if masked:
            visible = ((kj * t + c_i) < (qi * t + r_i)) & ((kj * t + c_i) >= front)
            log_keep = jnp.where(visible, log_keep, 0.0)
        hi = log_keep.astype(BF16)
        lo = (log_keep - hi.astype(F32)).astype(BF16)
        keep_after = _dot(hi, later) + _dot(lo, later)
        weight = jnp.exp(log_beta + keep_after + carry_ref[...])
        if masked:
            weight = jnp.where(visible, weight, 0.0)
        acc_ref[...] += _dot(weight.astype(BF16), vs)
        carry_ref[...] += jnp.sum(log_keep, axis=1, keepdims=True)

    tile(qi, True)

    def body(j, _):
        tile(qi - j, False)
        return 0
    lax.fori_loop(1, qi, body, 0)

    @pl.when(qi > 0)
    def _():
        tile(0, True)

    o = acc_ref[...]
    o = o * lax.rsqrt(jnp.mean(o * o, axis=-1, keepdims=True) + NORM_EPS) * ow_ref[...]
    o_ref[...] = o.astype(o_ref.dtype)


def _sb(proj3, qw, kw, ow, heads, hd, col0, front):
    bsz, lp, _ = proj3.shape
    t = SB_BLOCK
    cb = col0 // hd
    return pl.pallas_call(
        functools.partial(_sb_kernel, front=front, hd=hd),
        grid=(bsz, heads, lp // t),
        in_specs=[
            pl.BlockSpec((None, t, hd), lambda b, h, i: (b, i, cb + h)),
            pl.BlockSpec((None, lp, hd), lambda b, h, i: (b, 0, cb + heads + h)),
            pl.BlockSpec((None, lp, hd), lambda b, h, i: (b, 0, cb + 2 * heads + h)),
            pl.BlockSpec((1, hd), lambda b, h, i: (0, 0)),
            pl.BlockSpec((1, hd), lambda b, h, i: (0, 0)),
            pl.BlockSpec((1, hd), lambda b, h, i: (0, 0)),
        ],
        out_specs=pl.BlockSpec((None, t, hd), lambda b, h, i: (b, i, h)),
        out_shape=jax.ShapeDtypeStruct((bsz, lp, heads * hd), BF16),
        scratch_shapes=[pltpu.VMEM((lp, hd), BF16), pltpu.VMEM((t, hd), F32),
                        pltpu.VMEM((t, 1), F32)],
        compiler_params=_cparams(3),
        name="sb",
    )(proj3, proj3, proj3, qw, kw, ow)


def _mix_kernel(og_ref, os_ref, h_ref, wo_ref, nw_ref, rw_ref, rb_ref,
                h1_ref, hnp_ref, info_ref, gate_ref, cnt_ref, cnt_acc,
                *, front, blocks_per_batch, pack_rows):
    tm = MIX_ROWS
    i = pl.program_id(0)

    @pl.when(i == 0)
    def _():
        cnt_acc[...] = jnp.zeros_like(cnt_acc)

    gd = og_ref.shape[1]
    h1 = h_ref[...] + _dot(og_ref[...], wo_ref[0:gd, :]) + _dot(os_ref[...], wo_ref[gd:, :])
    h1_ref[...] = h1
    hn = h1 * lax.rsqrt(jnp.mean(h1 * h1, axis=-1, keepdims=True) + NORM_EPS) * nw_ref[...]

    half = hn.shape[1] // 2
    lo = pltpu.bitcast(hn[:, :half].astype(BF16).astype(F32), U32) >> 16
    hi = pltpu.bitcast(hn[:, half:].astype(BF16).astype(F32), U32) & jnp.uint32(0xFFFF0000)
    word = lo | hi
    for s in range(pack_rows):
        hnp_ref[pl.ds(s, tm, stride=pack_rows), :] = word[:, s * LANES:(s + 1) * LANES]

    logits = _dot3(hn, rw_ref[...]) + rb_ref[...]
    lane = lax.broadcasted_iota(I32, (tm, LANES), 1)
    row = lax.broadcasted_iota(I32, (tm, 1), 0)
    valid = ((i % blocks_per_batch) * tm + row) >= front
    work = logits
    tops, idxs, hots = [], [], []
    for _ in range(TOP_K):
        m = jnp.max(work, axis=1, keepdims=True)
        idx = jnp.min(jnp.where(work == m, lane, LANES), axis=1, keepdims=True)
        hot = lane == idx
        tops.append(m)
        idxs.append(idx)
        hots.append(hot)
        work = jnp.where(hot, -jnp.inf, work)
    exps = [jnp.exp(m - tops[0]) for m in tops]
    denom = exps[0] + exps[1] + exps[2] + exps[3]
    sel = jnp.zeros((tm, LANES), F32)
    for hot in hots:
        sel = sel + hot.astype(F32)
    sel = jnp.where(valid, sel, 0.0)

    r_i = lax.broadcasted_iota(I32, (tm, tm), 0)
    c_i = lax.broadcasted_iota(I32, (tm, tm), 1)
    before = (r_i > c_i).astype(BF16)
    rank = cnt_acc[0:1, :] + _dot(before, sel.astype(BF16))
    info = jnp.zeros((tm, LANES), I32)
    gates = jnp.zeros((tm, LANES), F32)
    for j in range(TOP_K):
        rank_j = jnp.sum(jnp.where(hots[j], rank, 0.0), axis=1, keepdims=True).astype(I32)
        info = jnp.where(lane == j, idxs[j], info)
        info = jnp.where(lane == TOP_K + j, rank_j, info)
        gates = jnp.where(lane == j, exps[j] / denom, gates)
    info_ref[...] = info
    gate_ref[...] = gates
    cnt_acc[...] = cnt_acc[...] + jnp.sum(sel, axis=0, keepdims=True)
    cnt_ref[...] = cnt_acc[...]


def _mix(o_gdn, o_sb, h0, w_out, norm_w, router_w, router_b, front, lp):
    rows, d = h0.shape
    gd = o_gdn.shape[1]
    tm = MIX_ROWS
    pack_rows = d // 2 // LANES
    kern = functools.partial(_mix_kernel, front=front, blocks_per_batch=lp // tm, pack_rows=pack_rows)
    return pl.pallas_call(
        kern,
        grid=(rows // tm,),
        in_specs=[
            pl.BlockSpec((tm, gd), lambda i: (i, 0)),
            pl.BlockSpec((tm, gd), lambda i: (i, 0)),
            pl.BlockSpec((tm, d), lambda i: (i, 0)),
            pl.BlockSpec(w_out.shape, lambda i: (0, 0)),
            pl.BlockSpec((1, d), lambda i: (0, 0)),
            pl.BlockSpec((d, LANES), lambda i: (0, 0)),
            pl.BlockSpec((1, LANES), lambda i: (0, 0)),
        ],
        out_specs=[
            pl.BlockSpec((tm, d), lambda i: (i, 0)),
            pl.BlockSpec((tm * pack_rows, LANES), lambda i: (i, 0)),
            pl.BlockSpec((tm, LANES), lambda i: (i, 0)),
            pl.BlockSpec((tm, LANES), lambda i: (i, 0)),
            pl.BlockSpec((8, LANES), lambda i: (0, 0)),
        ],
        out_shape=[
            jax.ShapeDtypeStruct((rows, d), F32),
            jax.ShapeDtypeStruct((rows * pack_rows, LANES), U32),
            jax.ShapeDtypeStruct((rows, LANES), I32),
            jax.ShapeDtypeStruct((rows, LANES), F32),
            jax.ShapeDtypeStruct((8, LANES), F32),
        ],
        scratch_shapes=[pltpu.VMEM((8, LANES), F32)],
        compiler_params=_cparams(1),
        name="mix_router",
    )(o_gdn, o_sb, h0, w_out, norm_w, router_w, router_b)


def _dispatch_kernel(slot_ref, hnp_ref, xs_in_ref, xs_ref, sem, *, pack_rows):
    del xs_in_ref
    tm = MIX_ROWS
    base = pl.program_id(0) * tm

    def copy(tok, j):
        s = slot_ref[0, tok * TOP_K + j]
        return s, pltpu.make_async_copy(
            hnp_ref.at[pl.ds((base + tok) * pack_rows, pack_rows), :],
            xs_ref.at[pl.ds(s * pack_rows, pack_rows), :], sem)

    def start(tok, _):
        for j in range(TOP_K):
            s, cp = copy(tok, j)

            @pl.when(s >= 0)
            def _():
                cp.start()
        return 0

    def wait(tok, _):
        for j in range(TOP_K):
            s, cp = copy(tok, j)

            @pl.when(s >= 0)
            def _():
                cp.wait()
        return 0

    lax.fori_loop(0, tm, start, 0)
    lax.fori_loop(0, tm, wait, 0)


def _dispatch(slots, hnp, xs_zero, pack_rows):
    rows = slots.shape[0]
    tm = MIX_ROWS
    slots3 = slots.reshape(rows // tm, 1, tm * TOP_K)
    return pl.pallas_call(
        functools.partial(_dispatch_kernel, pack_rows=pack_rows),
        grid=(rows // tm,),
        in_specs=[
            pl.BlockSpec((None, 1, tm * TOP_K), lambda i: (i, 0, 0), memory_space=pltpu.SMEM),
            pl.BlockSpec(memory_space=pl.ANY),
            pl.BlockSpec(memory_space=pl.ANY),
        ],
        out_specs=pl.BlockSpec(memory_space=pl.ANY),
        out_shape=jax.ShapeDtypeStruct(xs_zero.shape, xs_zero.dtype),
        scratch_shapes=[pltpu.SemaphoreType.DMA(())],
        input_output_aliases={2: 0},
        compiler_params=_cparams(1),
        name="dispatch",
    )(slots3, hnp, xs_zero)


def _expert_kernel(ue_ref, ur_ref, un_ref, used_ref, xs_ref, wg_ref, wu_ref, bg_ref, bu_ref, wd_ref,
                   bd_ref, ys_ref, xbuf, xb, acc, wg_b, wu_b, wd_b, ystage, sem_in, sem_out,
                   *, pack_rows, out_rows, n_slots):
    del ue_ref
    ts = EXP_SUB
    n_sub_max = EXP_UNIT // ts
    u = pl.program_id(0)
    f = pl.program_id(1)
    n_f = pl.num_programs(1)
    nsub = un_ref[u]
    row0 = ur_ref[u]
    half = xb.shape[1] // 2

    @pl.when((u == 0) & (f == 0))
    def _():
        ystage[...] = jnp.zeros_like(ystage)
        used = used_ref[0]

        def tail_copy(i):
            return pltpu.make_async_copy(
                ystage, ys_ref.at[pl.ds((used + i * ts) * out_rows, ts * out_rows), :], sem_out)

        n_tail = (n_slots - used) // ts
        lax.fori_loop(0, n_tail, lambda i, c: (tail_copy(i).start(), c)[1], 0)
        lax.fori_loop(0, n_tail, lambda i, c: (tail_copy(i).wait(), c)[1], 0)

    def in_copy(s):
        return pltpu.make_async_copy(
            xs_ref.at[pl.ds((row0 + s * ts) * pack_rows, ts * pack_rows), :],
            xbuf.at[pl.ds(s * ts * pack_rows, ts * pack_rows), :], sem_in.at[s])

    @pl.when((f == 0) & (nsub > 0))
    def _():
        for s in range(n_sub_max):
            @pl.when(s < nsub)
            def _():
                in_copy(s).start()
        for s in range(n_sub_max):
            @pl.when(s < nsub)
            def _():
                in_copy(s).wait()
                words = jnp.concatenate(
                    [xbuf[pl.ds(s * ts * pack_rows + c, ts, stride=pack_rows), :]
                     for c in range(pack_rows)], axis=1)
                xb[s * ts:(s + 1) * ts, :half] = pltpu.bitcast(words << 16, F32).astype(BF16)
                xb[s * ts:(s + 1) * ts, half:] = pltpu.bitcast(
                    words & jnp.uint32(0xFFFF0000), F32).astype(BF16)

    @pl.when(nsub > 0)
    def _():
        wg_b[...] = wg_ref[...].astype(BF16)
        wu_b[...] = wu_ref[...].astype(BF16)
        wd_b[...] = wd_ref[...].astype(BF16)

        def sub(s, _):
            r = pl.multiple_of(s * ts, ts)
            x = xb[pl.ds(r, ts), :]
            g = _dot(x, wg_b[...]) + bg_ref[...]
            up = _dot(x, wu_b[...]) + bu_ref[...]
            g = jnp.minimum(g, SWIGLU_LIMIT)
            up = jnp.clip(up, -SWIGLU_LIMIT, SWIGLU_LIMIT)
            act = (up + 1.0) * g * _sigmoid(SWIGLU_ALPHA * g)
            y = _dot(act.astype(BF16), wd_b[...])

            @pl.when(f == 0)
            def _():
                acc[pl.ds(r, ts), :] = y + bd_ref[...]

            @pl.when(f > 0)
            def _():
                acc[pl.ds(r, ts), :] += y
            return 0
        lax.fori_loop(0, nsub, sub, 0)

    @pl.when((f == n_f - 1) & (nsub > 0))
    def _():
        for s in range(n_sub_max):
            @pl.when(s < nsub)
            def _():
                a = acc[s * ts:(s + 1) * ts, :]
                for c in range(out_rows):
                    ystage[pl.ds(c, ts, stride=out_rows), :] = a[:, c * LANES:(c + 1) * LANES]
                cp = pltpu.make_async_copy(
                    ystage, ys_ref.at[pl.ds((row0 + s * ts) * out_rows, ts * out_rows), :], sem_out)
                cp.start()
                cp.wait()


def _experts(unit_e, unit_row0, unit_nsub, used_rows, xs, w_gate_up, b_gate_up, w_down, b_down,
             n_slots, pack_rows):
    n_exp, d, two_de = w_gate_up.shape
    de = two_de // 2
    tf = EXP_FTILE
    n_f = de // tf
    out_rows = d // LANES
    n_units = unit_e.shape[0]
    last_f = n_f - 1

    def fidx(u, f, un):
        return jnp.where(un[u] > 0, f, last_f)

    grid_spec = pltpu.PrefetchScalarGridSpec(
        num_scalar_prefetch=4,
        grid=(n_units, n_f),
        in_specs=[
            pl.BlockSpec(memory_space=pl.ANY),
            pl.BlockSpec((None, d, tf), lambda u, f, ue, ur, un, us: (ue[u], 0, fidx(u, f, un))),
            pl.BlockSpec((None, d, tf),
                         lambda u, f, ue, ur, un, us: (ue[u], 0, n_f + fidx(u, f, un))),
            pl.BlockSpec((None, 1, tf), lambda u, f, ue, ur, un, us: (ue[u], 0, fidx(u, f, un))),
            pl.BlockSpec((None, 1, tf),
                         lambda u, f, ue, ur, un, us: (ue[u], 0, n_f + fidx(u, f, un))),
            pl.BlockSpec((None, tf, d), lambda u, f, ue, ur, un, us: (ue[u], fidx(u, f, un), 0)),
            pl.BlockSpec((None, 1, d), lambda u, f, ue, ur, un, us: (ue[u], 0, 0)),
        ],
        out_specs=pl.BlockSpec(memory_space=pl.ANY),
        scratch_shapes=[
            pltpu.VMEM((EXP_UNIT * pack_rows, LANES), U32),
            pltpu.VMEM((EXP_UNIT, d), BF16),
            pltpu.VMEM((EXP_UNIT, d), F32),
            pltpu.VMEM((d, tf), BF16),
            pltpu.VMEM((d, tf), BF16),
            pltpu.VMEM((tf, d), BF16),
            pltpu.VMEM((EXP_SUB * out_rows, LANES), F32),
            pltpu.SemaphoreType.DMA((EXP_UNIT // EXP_SUB,)),
            pltpu.SemaphoreType.DMA(()),
        ],
    )
    return pl.pallas_call(
        functools.partial(_expert_kernel, pack_rows=pack_rows, out_rows=out_rows, n_slots=n_slots),
        grid_spec=grid_spec,
        out_shape=jax.ShapeDtypeStruct((n_slots * out_rows, LANES), F32),
        compiler_params=_cparams(2),
        name="experts",
    )(unit_e, unit_row0, unit_nsub, used_rows, xs, w_gate_up, w_gate_up,
      b_gate_up.reshape(n_exp, 1, two_de), b_gate_up.reshape(n_exp, 1, two_de),
      w_down, b_down.reshape(n_exp, 1, d))


def _combine_kernel(slot_ref, gate_ref, h1_ref, ys_ref, o_ref, ybuf, sem, *, out_rows):
    tc = COMB_ROWS

    def copy(tok, j):
        s = slot_ref[0, tok * TOP_K + j]
        return pltpu.make_async_copy(
            ys_ref.at[pl.ds(s * out_rows, out_rows), :],
            ybuf.at[j, pl.ds(tok * out_rows, out_rows), :], sem)

    def start(tok, _):
        for j in range(TOP_K):
            copy(tok, j).start()
        return 0

    def wait(tok, _):
        for j in range(TOP_K):
            copy(tok, j).wait()
        return 0

    lax.fori_loop(0, tc, start, 0)
    lax.fori_loop(0, tc, wait, 0)

    gates = gate_ref[...]
    for c in range(out_rows):
        acc = h1_ref[:, c * LANES:(c + 1) * LANES]
        for j in range(TOP_K):
            acc = acc + gates[:, j:j + 1] * ybuf[j, pl.ds(c, tc, stride=out_rows), :]
        o_ref[:, c * LANES:(c + 1) * LANES] = acc


def _combine(slots, gates, h1, ys, bsz, seq, lp, d):
    tc = COMB_ROWS
    rows = slots.shape[0]
    out_rows = d // LANES
    slots3 = slots.reshape(rows // tc, 1, tc * TOP_K)
    nb = lp // tc
    first = (lp - seq) // tc
    return pl.pallas_call(
        functools.partial(_combine_kernel, out_rows=out_rows),
        grid=(bsz, seq // tc),
        in_specs=[
            pl.BlockSpec((None, 1, tc * TOP_K), lambda b, i: (b * nb + first + i, 0, 0),
                         memory_space=pltpu.SMEM),
            pl.BlockSpec((tc, LANES), lambda b, i: (b * nb + first + i, 0)),
            pl.BlockSpec((tc, d), lambda b, i: (b * nb + first + i, 0)),
            pl.BlockSpec(memory_space=pl.ANY),
        ],
        out_specs=pl.BlockSpec((None, tc, d), lambda b, i: (b, i, 0)),
        out_shape=jax.ShapeDtypeStruct((bsz, seq, d), F32),
        scratch_shapes=[pltpu.VMEM((TOP_K, tc * out_rows, LANES), F32), pltpu.SemaphoreType.DMA(())],
        compiler_params=_cparams(2),
        name="combine",
    )(slots3, gates, h1, ys)


def _pick(n, candidates):
    for c in candidates:
        if n % c == 0:
            return c
    raise ValueError(f"no block size in {candidates} divides {n}")


def _plan(info, cnt, n_exp, rows, lp, front, n_assign):
    counts = cnt[0, :n_exp].astype(I32)
    padded = (counts + EXP_SUB - 1) // EXP_SUB * EXP_SUB
    pstart = jnp.cumsum(padded) - padded
    eid = info[:, :TOP_K]
    rank = info[:, TOP_K:2 * TOP_K]
    onehot = eid[:, :, None] == jnp.arange(n_exp, dtype=I32)[None, None, :]
    slot = rank + jnp.sum(jnp.where(onehot, pstart[None, None, :], 0), axis=-1)
    row_valid = (jnp.arange(rows, dtype=I32) % lp) >= front
    slots = jnp.where(row_valid[:, None], slot, -1).astype(I32)

    n_slots = (n_assign + n_exp * (EXP_SUB - 1)) // EXP_SUB * EXP_SUB
    units_per_e = (padded + EXP_UNIT - 1) // EXP_UNIT
    cum_units = jnp.cumsum(units_per_e)
    n_units = n_slots // EXP_UNIT + n_exp
    uidx = jnp.arange(n_units, dtype=I32)
    ue = jnp.sum(cum_units[None, :] <= uidx[:, None], axis=1).astype(I32)
    live = ue < n_exp
    last_e = jnp.max(jnp.where(counts > 0, jnp.arange(n_exp, dtype=I32), 0))
    ue = jnp.where(live, ue, last_e)
    k_in_e = uidx - (cum_units - units_per_e)[ue]
    unit_row0 = jnp.where(live, pstart[ue] + k_in_e * EXP_UNIT, 0).astype(I32)
    unit_nsub = jnp.where(
        live, jnp.clip((padded[ue] - k_in_e * EXP_UNIT) // EXP_SUB, 0, EXP_UNIT // EXP_SUB),
        0).astype(I32)
    used_rows = jnp.sum(padded).astype(I32)[None]
    return slots, (ue, unit_row0, unit_nsub, used_rows), n_slots


def kernel(x, meta_tokens, mix_norm_w, w_in, conv_w, a_log, dt_bias, gdn_norm_w, sb_q_norm_w,
           sb_k_norm_w, sb_out_norm_w, w_out, ffn_norm_w, router_w, router_b, w_gate_up, b_gate_up,
           w_down, b_down):
    bsz, seq, d = x.shape
    n_meta = meta_tokens.shape[0]
    depth = mix_norm_w.shape[0]
    heads = a_log.shape[1]
    hd = gdn_norm_w.shape[1]
    gd = heads * hd
    sbd = (w_in.shape[2] - 4 * gd - 2 * heads) // 3
    sb_heads = sbd // hd
    n_exp = router_w.shape[2]
    assert seq % ROW_ALIGN == 0 and d % (2 * 8 * LANES) == 0 and hd == LANES
    assert 2 * heads <= LANES and n_exp <= LANES and sb_heads == heads
    assert depth == 1, "a second layer would need the meta rows carried through the combine stage"
    front = (-n_meta) % ROW_ALIGN
    lp = front + n_meta + seq
    rows = bsz * lp
    pack_rows = d // 2 // LANES

    h = jnp.concatenate([
        jnp.zeros((bsz, front, d), x.dtype),
        jnp.broadcast_to(meta_tokens.astype(x.dtype)[None], (bsz, n_meta, d)),
        x], axis=1).reshape(rows, d)

    wl = w_in[0]
    n_ba = 4 * gd
    w_main = jnp.concatenate([wl[:, :n_ba], wl[:, n_ba + 2 * heads:]], axis=1).astype(BF16)
    w_ba = jnp.pad(wl[:, n_ba:n_ba + 2 * heads], ((0, 0), (0, LANES - 2 * heads))).astype(BF16)
    tm = _pick(rows, (1024, 512, 256))
    tn = _pick(w_main.shape[1], (512, 256, 128))
    proj, ba = _in_proj(h, mix_norm_w[0][None], w_main, w_ba, tm, tn)
    proj3 = proj.reshape(bsz, lp, -1)
    ba3 = ba.reshape(bsz, lp, LANES)

    gparams = jnp.zeros((2, LANES), F32)
    gparams = gparams.at[0, heads:2 * heads].set(a_log[0].astype(F32))
    gparams = gparams.at[1, heads:2 * heads].set(dt_bias[0].astype(F32))
    o_gdn = _gdn(proj3, ba3, conv_w[0].astype(F32), gparams, gdn_norm_w[0][None], heads, hd)
    o_sb = _sb(proj3, sb_q_norm_w[0][None], sb_k_norm_w[0][None], sb_out_norm_w[0][None],
               sb_heads, hd, 4 * gd, front)

    rw = jnp.pad(router_w[0].astype(F32), ((0, 0), (0, LANES - n_exp)))
    rb = jnp.pad(router_b[0].astype(F32), (0, LANES - n_exp), constant_values=-1e30)[None]
    h1, hnp, info, gates, cnt = _mix(
        o_gdn.reshape(rows, gd), o_sb.reshape(rows, sbd), h, w_out[0].astype(BF16),
        ffn_norm_w[0][None], rw, rb, front, lp)

    slots, units, n_slots = _plan(info, cnt, n_exp, rows, lp, front, bsz * (n_meta + seq) * TOP_K)
    xs = _dispatch(slots, hnp, jnp.zeros((n_slots * pack_rows, LANES), U32), pack_rows)
    ys = _experts(*units, xs, w_gate_up[0], b_gate_up[0], w_down[0], b_down[0], n_slots, pack_rows)
    return _combine(slots, gates, h1, ys, bsz, seq, lp, d)
```

```python
import functools

import jax
import jax.numpy as jnp
from jax import lax
from jax.experimental import pallas as pl
from jax.experimental.pallas import tpu as pltpu

F32 = jnp.float32
BF16 = jnp.bfloat16
U32 = jnp.uint32
I32 = jnp.int32

NORM_EPS = 1e-6
TOP_K = 4
SWIGLU_LIMIT = 7.0
SWIGLU_ALPHA = 1.702
LANES = 128
ROW_ALIGN = 256
GDN_PREP_ROWS = 256
GDN_CHUNK = 64
SB_BLOCK = 256
MIX_ROWS = 256
EXP_SUB = 256
EXP_UNIT = 1024
EXP_FTILE = 512
COMB_ROWS = 128
VMEM_LIMIT = 56 * 1024 * 1024
HIGH16 = 0xFFFF0000


def _cparams(n_grid, vmem=VMEM_LIMIT):
    return pltpu.CompilerParams(dimension_semantics=("arbitrary",) * n_grid, vmem_limit_bytes=vmem)


def _dot(a, b):
    return jnp.dot(a, b, preferred_element_type=F32)


def _dot_nt(a, b):
    return lax.dot_general(a, b, (((1,), (1,)), ((), ())), preferred_element_type=F32)


def _dot_tn(a, b):
    return lax.dot_general(a, b, (((0,), (0,)), ((), ())), preferred_element_type=F32)


def _split3(x):
    hi = x.astype(BF16)
    r1 = x - hi.astype(F32)
    mid = r1.astype(BF16)
    lo = (r1 - mid.astype(F32)).astype(BF16)
    return hi, mid, lo


def _dot3(a, b):
    ah, am, al = _split3(a)
    bh, bm, bl = _split3(b)
    return (_dot(ah, bh) + (_dot(ah, bm) + _dot(am, bh))
            + (_dot(am, bm) + _dot(ah, bl) + _dot(al, bh)))


def _sigmoid(x):
    return 1.0 / (1.0 + jnp.exp(-x))


def _softplus(x):
    return jnp.maximum(x, 0.0) + jnp.log1p(jnp.exp(-jnp.abs(x)))


def _inproj_kernel(h_ref, nw_ref, w_ref, wba_ref, o_ref, ba_ref, xn_ref):
    @pl.when(pl.program_id(1) == 0)
    def _():
        x = h_ref[...]
        xn = x * lax.rsqrt(jnp.mean(x * x, axis=-1, keepdims=True) + NORM_EPS) * nw_ref[...]
        xn = xn.astype(BF16)
        xn_ref[...] = xn
        ba_ref[...] = _dot(xn, wba_ref[...])

    o_ref[...] = _dot(xn_ref[...], w_ref[...]).astype(o_ref.dtype)


def _in_proj(h0, norm_w, w_main, w_ba, tm, tn):
    rows, d = h0.shape
    n_main = w_main.shape[1]
    return pl.pallas_call(
        _inproj_kernel,
        grid=(rows // tm, n_main // tn),
        in_specs=[
            pl.BlockSpec((tm, d), lambda i, n: (i, 0)),
            pl.BlockSpec((1, d), lambda i, n: (0, 0)),
            pl.BlockSpec((d, tn), lambda i, n: (0, n)),
            pl.BlockSpec((d, LANES), lambda i, n: (0, 0)),
        ],
        out_specs=[
            pl.BlockSpec((tm, tn), lambda i, n: (i, n)),
            pl.BlockSpec((tm, LANES), lambda i, n: (i, 0)),
        ],
        out_shape=[
            jax.ShapeDtypeStruct((rows, n_main), BF16),
            jax.ShapeDtypeStruct((rows, LANES), F32),
        ],
        scratch_shapes=[pltpu.VMEM((tm, d), BF16)],
        compiler_params=_cparams(2),
        name="in_proj",
    )(h0, norm_w, w_main, w_ba)


def _gdn_prep_kernel(x_ref, prev_ref, ba_ref, cw_ref, gp_ref, o_ref, bg_ref, *, heads, hd):
    tp = x_ref.shape[0]
    gd = heads * hd
    taps = cw_ref.shape[0]
    x = x_ref[...].astype(F32)
    prev = jnp.where(pl.program_id(1) > 0, prev_ref[8:16, :].astype(F32), 0.0)
    xs = jnp.concatenate([prev, x], axis=0)
    y = cw_ref[taps - 1:taps, :] * x
    for s in range(1, taps):
        y = y + cw_ref[taps - 1 - s:taps - s, :] * pltpu.roll(xs, s, axis=0)[8:8 + tp, :]
    y = y * _sigmoid(y)
    for h in range(heads):
        q = y[:, h * hd:(h + 1) * hd]
        k = y[:, gd + h * hd:gd + (h + 1) * hd]
        q = q * (lax.rsqrt(jnp.sum(q * q, axis=-1, keepdims=True) + NORM_EPS) * (hd ** -0.5))
        k = k * lax.rsqrt(jnp.sum(k * k, axis=-1, keepdims=True) + NORM_EPS)
        o_ref[:, h * hd:(h + 1) * hd] = q.astype(o_ref.dtype)
        o_ref[:, gd + h * hd:gd + (h + 1) * hd] = k.astype(o_ref.dtype)
    o_ref[:, 2 * gd:] = y[:, 2 * gd:].astype(o_ref.dtype)
    ba = ba_ref[...]
    lane = lax.broadcasted_iota(I32, ba.shape, 1)
    decay = -jnp.exp(gp_ref[0:1, :]) * _softplus(ba + gp_ref[1:2, :])
    bg_ref[...] = jnp.where(lane < heads, _sigmoid(ba), decay)


def _gdn_prep(proj3, ba3, conv_w, gparams, heads, hd):
    bsz, lp, _ = proj3.shape
    gd = heads * hd
    tp = GDN_PREP_ROWS
    return pl.pallas_call(
        functools.partial(_gdn_prep_kernel, heads=heads, hd=hd),
        grid=(bsz, lp // tp),
        in_specs=[
            pl.BlockSpec((None, tp, 3 * gd), lambda b, i: (b, i, 0)),
            pl.BlockSpec((None, 16, 3 * gd), lambda b, i: (b, jnp.maximum(i * (tp // 16) - 1, 0), 0)),
            pl.BlockSpec((None, tp, LANES), lambda b, i: (b, i, 0)),
            pl.BlockSpec(conv_w.shape, lambda b, i: (0, 0)),
            pl.BlockSpec((2, LANES), lambda b, i: (0, 0)),
        ],
        out_specs=[
            pl.BlockSpec((None, tp, 3 * gd), lambda b, i: (b, i, 0)),
            pl.BlockSpec((None, tp, LANES), lambda b, i: (b, i, 0)),
        ],
        out_shape=[
            jax.ShapeDtypeStruct((bsz, lp, 3 * gd), BF16),
            jax.ShapeDtypeStruct((bsz, lp, LANES), F32),
        ],
        compiler_params=_cparams(2),
        name="gdn_prep",
    )(proj3, proj3, ba3, conv_w, gparams)


def _gdn_kernel(qkv_ref, z_ref, bg_ref, nw_ref, o_ref, state, *, heads, hd):
    c_rows = GDN_CHUNK
    gd = heads * hd

    @pl.when(pl.program_id(1) == 0)
    def _():
        state[...] = jnp.zeros_like(state)

    bg = bg_ref[...]
    r_i = lax.broadcasted_iota(I32, (c_rows, c_rows), 0)
    c_i = lax.broadcasted_iota(I32, (c_rows, c_rows), 1)
    incl = r_i >= c_i
    strict = r_i > c_i
    tri = incl.astype(BF16)
    g_parts = _split3(bg)
    gcum = _dot(tri, g_parts[0]) + _dot(tri, g_parts[1]) + _dot(tri, g_parts[2])
    n_sel = -(-heads // 8) * 8
    pick = (lax.broadcasted_iota(I32, (n_sel, LANES), 1)
            == lax.broadcasted_iota(I32, (n_sel, LANES), 0) + heads).astype(BF16)
    c_parts = _split3(gcum)
    gcum_rows = _dot_nt(pick, c_parts[0]) + _dot_nt(pick, c_parts[1]) + _dot_nt(pick, c_parts[2])

    hs = range(heads)
    q16 = [qkv_ref[:, h * hd:(h + 1) * hd] for h in hs]
    k16 = [qkv_ref[:, gd + h * hd:gd + (h + 1) * hd] for h in hs]
    k = [k16[h].astype(F32) for h in hs]
    gc = [gcum[:, heads + h:heads + h + 1] for h in hs]
    decay = [jnp.exp(jnp.minimum(gc[h] - gcum_rows[h:h + 1, :], 0.0)) for h in hs]
    kb = [k[h] * bg[:, h:h + 1] for h in hs]
    p = [jnp.where(strict, _dot_nt(kb[h].astype(BF16), k16[h]) * decay[h], 0.0) for h in hs]
    p = [(-p[h]).astype(BF16) for h in hs]
    sol = [jnp.concatenate(
        [qkv_ref[:, 2 * gd + h * hd:2 * gd + (h + 1) * hd].astype(F32) * bg[:, h:h + 1],
         kb[h] * jnp.exp(gc[h])], axis=1) for h in hs]
    n_fac = c_rows.bit_length() - 1
    for i in range(n_fac):
        sol = [sol[h] + _dot(p[h], sol[h].astype(BF16)) for h in hs]
        if i + 1 < n_fac:
            p = [_dot(p[h], p[h]).astype(BF16) for h in hs]
    attn = [jnp.where(incl, _dot_nt(q16[h], k16[h]) * decay[h], 0.0).astype(BF16) for h in hs]
    s_old = [state[h] for h in hs]
    s_b = [s_old[h].astype(BF16) for h in hs]
    v_new = [(sol[h][:, :hd] - _dot(sol[h][:, hd:].astype(BF16), s_b[h])).astype(BF16) for h in hs]
    o = [_dot((q16[h].astype(F32) * jnp.exp(gc[h])).astype(BF16), s_b[h]) + _dot(attn[h], v_new[h])
         for h in hs]
    for h in hs:
        g_last = gc[h][c_rows - 1:c_rows, :]
        k_dec = (k[h] * jnp.exp(g_last - gc[h])).astype(BF16)
        state[h] = s_old[h] * jnp.exp(g_last) + _dot_tn(k_dec, v_new[h])
    for h in hs:
        zf = z_ref[:, h * hd:(h + 1) * hd].astype(F32)
        o_n = o[h] * lax.rsqrt(jnp.mean(o[h] * o[h], axis=-1, keepdims=True) + NORM_EPS) * nw_ref[...]
        o_ref[:, h * hd:(h + 1) * hd] = (o_n * (zf * _sigmoid(zf))).astype(o_ref.dtype)


def _gdn(qkv3, proj3, bg3, norm_w, heads, hd):
    bsz, lp, _ = qkv3.shape
    gd = heads * hd
    c = GDN_CHUNK
    return pl.pallas_call(
        functools.partial(_gdn_kernel, heads=heads, hd=hd),
        grid=(bsz, lp // c),
        in_specs=[
            pl.BlockSpec((None, c, 3 * gd), lambda b, i: (b, i, 0)),
            pl.BlockSpec((None, c, gd), lambda b, i: (b, i, 3)),
            pl.BlockSpec((None, c, LANES), lambda b, i: (b, i, 0)),
            pl.BlockSpec((1, hd), lambda b, i: (0, 0)),
        ],
        out_specs=pl.BlockSpec((None, c, gd), lambda b, i: (b, i, 0)),
        out_shape=jax.ShapeDtypeStruct((bsz, lp, gd), BF16),
        scratch_shapes=[pltpu.VMEM((heads, hd, hd), F32)],
        compiler_params=_cparams(2),
        name="gdn",
    )(qkv3, proj3, bg3, norm_w)


def _sb_kernel(q_ref, k_ref, v_ref, qw_ref, kw_ref, ow_ref, tri_ref, o_ref, kn_ref, acc_ref,
               drop_ref, *, front, hd):
    t = SB_BLOCK
    qi = pl.program_id(2)
    n_blocks = k_ref.shape[0] // t

    @pl.when(qi == 0)
    def _():
        def norm_keys(i, _):
            kb = k_ref[pl.ds(i * t, t), :].astype(F32)
            kn = kb * lax.rsqrt(jnp.mean(kb * kb, axis=-1, keepdims=True) + NORM_EPS) * kw_ref[...]
            kn_ref[pl.ds(i * t, t), :] = kn.astype(BF16)
            return 0
        lax.fori_loop(0, n_blocks, norm_keys, 0)

    q = q_ref[...].astype(F32)
    qn = q * lax.rsqrt(jnp.mean(q * q, axis=-1, keepdims=True) + NORM_EPS) * qw_ref[...]
    qn = (qn * (hd ** -0.5)).astype(BF16)

    acc_ref[...] = jnp.zeros_like(acc_ref)
    drop_ref[...] = jnp.zeros_like(drop_ref)

    def scores(kj, masked):
        start = pl.multiple_of(kj * t, t)
        z = _dot_nt(qn, kn_ref[pl.ds(start, t), :])
        soft = jnp.log(1.0 + jnp.exp(-jnp.abs(z)))
        pos = jnp.maximum(z, 0.0)
        log_beta = (z - pos) - soft
        drop = pos + soft
        visible = None
        if masked:
            r_i = lax.broadcasted_iota(I32, (t, t), 0)
            c_i = lax.broadcasted_iota(I32, (t, t), 1)
            visible = ((kj * t + c_i) < (qi * t + r_i)) & ((kj * t + c_i) >= front)
            drop = jnp.where(visible, drop, 0.0)
        hi = pltpu.bitcast(pltpu.bitcast(drop, U32) & jnp.uint32(HIGH16), F32)
        lo = drop - hi
        sums = _dot(jnp.concatenate([hi.astype(BF16), lo.astype(BF16)], axis=1), tri_ref[...])
        return log_beta, sums[:, :t], sums[:, t:], visible

    def run(kjs, masked):
        parts = [scores(kj, masked) for kj in kjs]
        dropped = drop_ref[...]
        acc = acc_ref[...]
        for kj, (log_beta, after, total, visible) in zip(kjs, parts):
            dropped_t = jnp.concatenate([dropped] * (t // LANES), axis=1)
            weight = jnp.exp(log_beta - after - dropped_t)
            if masked:
                weight = jnp.where(visible, weight, 0.0)
            start = pl.multiple_of(kj * t, t)
            acc = acc + _dot(weight.astype(BF16), v_ref[pl.ds(start, t), :])
            dropped = dropped + total
        acc_ref[...] = acc
        drop_ref[...] = dropped

    run([qi], True)
    n_mid = jnp.maximum(qi - 1, 0)

    def pair(p, _):
        run([qi - 1 - 2 * p, qi - 2 - 2 * p], False)
        return 0
    lax.fori_loop(0, n_mid // 2, pair, 0)

    @pl.when(n_mid % 2 == 1)
    def _():
        run([1], False)

    @pl.when(qi > 0)
    def _():
        run([0], True)

    o = acc_ref[...]
    o = o * lax.rsqrt(jnp.mean(o * o, axis=-1, keepdims=True) + NORM_EPS) * ow_ref[...]
    o_ref[...] = o.astype(o_ref.dtype)


def _sb(proj3, qw, kw, ow, heads, hd, col0, front):
    bsz, lp, _ = proj3.shape
    t = SB_BLOCK
    cb = col0 // hd
    later = (jnp.arange(t)[:, None] > jnp.arange(t)[None, :])
    tri = jnp.concatenate([later, jnp.ones((t, LANES), bool)], axis=1)
    tri = jnp.concatenate([tri, tri], axis=0).astype(BF16)
    return pl.pallas_call(
        functools.partial(_sb_kernel, front=front, hd=hd),
        grid=(bsz, heads, lp // t),
        in_specs=[
            pl.BlockSpec((None, t, hd), lambda b, h, i: (b, i, cb + h)),
            pl.BlockSpec((None, lp, hd), lambda b, h, i: (b, 0, cb + heads + h)),
            pl.BlockSpec((None, lp, hd), lambda b, h, i: (b, 0, cb + 2 * heads + h)),
            pl.BlockSpec((1, hd), lambda b, h, i: (0, 0)),
            pl.BlockSpec((1, hd), lambda b, h, i: (0, 0)),
            pl.BlockSpec((1, hd), lambda b, h, i: (0, 0)),
            pl.BlockSpec(tri.shape, lambda b, h, i: (0, 0)),
        ],
        out_specs=pl.BlockSpec((None, t, hd), lambda b, h, i: (b, i, h)),
        out_shape=jax.ShapeDtypeStruct((bsz, lp, heads * hd), BF16),
        scratch_shapes=[pltpu.VMEM((lp, hd), BF16), pltpu.VMEM((t, hd), F32),
                        pltpu.VMEM((t, LANES), F32)],
        compiler_params=_cparams(3),
        name="sb",
    )(proj3, proj3, proj3, qw, kw, ow, tri)


def _mix_kernel(og_ref, os_ref, h_ref, wo_ref, nw_ref, rw_ref, rb_ref,
                h1_ref, hnp_ref, info_ref, gate_ref, cnt_ref, cnt_acc,
                *, front, blocks_per_batch, pack_rows):
    tm = MIX_ROWS
    i = pl.program_id(0)

    @pl.when(i == 0)
    def _():
        cnt_acc[...] = jnp.zeros_like(cnt_acc)

    gd = og_ref.shape[1]
    h1 = h_ref[...] + _dot(og_ref[...], wo_ref[0:gd, :]) + _dot(os_ref[...], wo_ref[gd:, :])
    h1_ref[...] = h1
    hn = h1 * lax.rsqrt(jnp.mean(h1 * h1, axis=-1, keepdims=True) + NORM_EPS) * nw_ref[...]

    half = hn.shape[1] // 2
    lo = pltpu.bitcast(hn[:, :half].astype(BF16).astype(F32), U32) >> 16
    hi = pltpu.bitcast(hn[:, half:].astype(BF16).astype(F32), U32) & jnp.uint32(HIGH16)
    word = lo | hi
    for s in range(pack_rows):
        hnp_ref[pl.ds(s, tm, stride=pack_rows), :] = word[:, s * LANES:(s + 1) * LANES]

    logits = _dot3(hn, rw_ref[...]) + rb_ref[...]
    lane = lax.broadcasted_iota(I32, (tm, LANES), 1)
    row = lax.broadcasted_iota(I32, (tm, 1), 0)
    valid = ((i % blocks_per_batch) * tm + row) >= front
    work = logits
    tops, idxs, hots = [], [], []
    for _ in range(TOP_K):
        m = jnp.max(work, axis=1, keepdims=True)
        idx = jnp.min(jnp.where(work == m, lane, LANES), axis=1, keepdims=True)
        hot = lane == idx
        tops.append(m)
        idxs.append(idx)
        hots.append(hot)
        work = jnp.where(hot, -jnp.inf, work)
    exps = [jnp.exp(m - tops[0]) for m in tops]
    denom = exps[0] + exps[1] + exps[2] + exps[3]
    sel = jnp.zeros((tm, LANES), F32)
    for hot in hots:
        sel = sel + hot.astype(F32)
    sel = jnp.where(valid, sel, 0.0)

    r_i = lax.broadcasted_iota(I32, (tm, tm), 0)
    c_i = lax.broadcasted_iota(I32, (tm, tm), 1)
    before = (r_i > c_i).astype(BF16)
    rank = cnt_acc[0:1, :] + _dot(before, sel.astype(BF16))
    info = jnp.zeros((tm, LANES), I32)
    gates = jnp.zeros((tm, LANES), F32)
    for j in range(TOP_K):
        rank_j = jnp.sum(jnp.where(hots[j], rank, 0.0), axis=1, keepdims=True).astype(I32)
        info = jnp.where(lane == j, idxs[j], info)
        info = jnp.where(lane == TOP_K + j, rank_j, info)
        gates = jnp.where(lane == j, exps[j] / denom, gates)
    info_ref[...] = info
    gate_ref[...] = gates
    cnt_acc[...] = cnt_acc[...] + jnp.sum(sel, axis=0, keepdims=True)
    cnt_ref[...] = cnt_acc[...]


def _mix(o_gdn, o_sb, h0, w_out, norm_w, router_w, router_b, front, lp):
    rows, d = h0.shape
    gd = o_gdn.shape[1]
    tm = MIX_ROWS
    pack_rows = d // 2 // LANES
    kern = functools.partial(_mix_kernel, front=front, blocks_per_batch=lp // tm, pack_rows=pack_rows)
    return pl.pallas_call(
        kern,
        grid=(rows // tm,),
        in_specs=[
            pl.BlockSpec((tm, gd), lambda i: (i, 0)),
            pl.BlockSpec((tm, gd), lambda i: (i, 0)),
            pl.BlockSpec((tm, d), lambda i: (i, 0)),
            pl.BlockSpec(w_out.shape, lambda i: (0, 0)),
            pl.BlockSpec((1, d), lambda i: (0, 0)),
            pl.BlockSpec((d, LANES), lambda i: (0, 0)),
            pl.BlockSpec((1, LANES), lambda i: (0, 0)),
        ],
        out_specs=[
            pl.BlockSpec((tm, d), lambda i: (i, 0)),
            pl.BlockSpec((tm * pack_rows, LANES), lambda i: (i, 0)),
            pl.BlockSpec((tm, LANES), lambda i: (i, 0)),
            pl.BlockSpec((tm, LANES), lambda i: (i, 0)),
            pl.BlockSpec((8, LANES), lambda i: (0, 0)),
        ],
        out_shape=[
            jax.ShapeDtypeStruct((rows, d), F32),
            jax.ShapeDtypeStruct((rows * pack_rows, LANES), U32),
            jax.ShapeDtypeStruct((rows, LANES), I32),
            jax.ShapeDtypeStruct((rows, LANES), F32),
            jax.ShapeDtypeStruct((8, LANES), F32),
        ],
        scratch_shapes=[pltpu.VMEM((8, LANES), F32)],
        compiler_params=_cparams(1),
        name="mix_router",
    )(o_gdn, o_sb, h0, w_out, norm_w, router_w, router_b)


def _dispatch_kernel(pstart_ref, padded_ref, used_ref, slot_ref, hnp_ref, xs_ref, zbuf, sem, zsem,
                     *, pack_rows, front, blocks_per_batch, n_slots):
    tm = MIX_ROWS
    ts = EXP_SUB
    i = pl.program_id(0)
    n_exp = pstart_ref.shape[0]

    @pl.when(i == 0)
    def _():
        zbuf[...] = jnp.zeros_like(zbuf)

        def zero_copy(row):
            return pltpu.make_async_copy(
                zbuf, xs_ref.at[pl.ds(row * pack_rows, ts * pack_rows), :], zsem)

        def pad_block(e, _, *, start):
            @pl.when(padded_ref[e] > 0)
            def _():
                cp = zero_copy(pstart_ref[e] + padded_ref[e] - ts)
                cp.start() if start else cp.wait()
            return 0

        used = used_ref[0]
        n_tail = (n_slots - used) // ts
        lax.fori_loop(0, n_exp, functools.partial(pad_block, start=True), 0)
        lax.fori_loop(0, n_tail, lambda j, c: (zero_copy(used + j * ts).start(), c)[1], 0)
        lax.fori_loop(0, n_exp, functools.partial(pad_block, start=False), 0)
        lax.fori_loop(0, n_tail, lambda j, c: (zero_copy(used + j * ts).wait(), c)[1], 0)

    def copy(tok, j):
        src = pl.multiple_of(tok * pack_rows, pack_rows)
        return pltpu.make_async_copy(
            hnp_ref.at[pl.ds(src, pack_rows), :],
            xs_ref.at[pl.ds(slot_ref[0, tok * TOP_K + j] * pack_rows, pack_rows), :], sem)

    def start(tok, _):
        for j in range(TOP_K):
            copy(tok, j).start()
        return 0

    def wait(tok, _):
        for j in range(TOP_K):
            copy(tok, j).wait()
        return 0

    first = jnp.where(i % blocks_per_batch == 0, front, 0)
    lax.fori_loop(first, tm, start, 0)
    lax.fori_loop(first, tm, wait, 0)


def _dispatch(slots, hnp, pstart, padded, used_rows, n_slots, pack_rows, front, lp):
    rows = slots.shape[0]
    tm = MIX_ROWS
    slots3 = slots.reshape(rows // tm, 1, tm * TOP_K)
    grid_spec = pltpu.PrefetchScalarGridSpec(
        num_scalar_prefetch=3,
        grid=(rows // tm,),
        in_specs=[
            pl.BlockSpec((None, 1, tm * TOP_K), lambda i, *_: (i, 0, 0), memory_space=pltpu.SMEM),
            pl.BlockSpec((tm * pack_rows, LANES), lambda i, *_: (i, 0)),
        ],
        out_specs=pl.BlockSpec(memory_space=pl.ANY),
        scratch_shapes=[pltpu.VMEM((EXP_SUB * pack_rows, LANES), U32),
                        pltpu.SemaphoreType.DMA(()), pltpu.SemaphoreType.DMA(())],
    )
    return pl.pallas_call(
        functools.partial(_dispatch_kernel, pack_rows=pack_rows, front=front,
                          blocks_per_batch=lp // tm, n_slots=n_slots),
        grid_spec=grid_spec,
        out_shape=jax.ShapeDtypeStruct((n_slots * pack_rows, LANES), U32),
        compiler_params=_cparams(1),
        name="dispatch",
    )(pstart, padded, used_rows, slots3, hnp)


def _expert_kernel(ue_ref, ur_ref, un_ref, used_ref, xs_ref, wg_ref, wu_ref, bg_ref, bu_ref, wd_ref,
                   bd_ref, ys_ref, xbuf, xb, acc, wg_b, wu_b, wd_b, ystage, sem_in, sem_out,
                   *, pack_rows, out_rows, n_slots):
    del ue_ref
    ts = EXP_SUB
    n_sub_max = EXP_UNIT // ts
    u = pl.program_id(0)
    f = pl.program_id(1)
    n_f = pl.num_programs(1)
    nsub = un_ref[u]
    row0 = ur_ref[u]
    d = xb.shape[1]
    half = d // 2
    tf = wg_b.shape[1]

    @pl.when((u == 0) & (f == 0))
    def _():
        ystage[...] = jnp.zeros_like(ystage)
        used = used_ref[0]

        def tail_copy(i):
            return pltpu.make_async_copy(
                ystage, ys_ref.at[pl.ds((used + i * ts) * out_rows, ts * out_rows), :], sem_out)

        n_tail = (n_slots - used) // ts
        lax.fori_loop(0, n_tail, lambda i, c: (tail_copy(i).start(), c)[1], 0)
        lax.fori_loop(0, n_tail, lambda i, c: (tail_copy(i).wait(), c)[1], 0)

    def in_copy(s):
        return pltpu.make_async_copy(
            xs_ref.at[pl.ds((row0 + s * ts) * pack_rows, ts * pack_rows), :],
            xbuf.at[pl.ds(s * ts * pack_rows, ts * pack_rows), :], sem_in.at[s])

    @pl.when((f == 0) & (nsub > 0))
    def _():
        for s in range(n_sub_max):
            @pl.when(s < nsub)
            def _():
                in_copy(s).start()
        for s in range(n_sub_max):
            @pl.when(s < nsub)
            def _():
                in_copy(s).wait()
                words = jnp.concatenate(
                    [xbuf[pl.ds(s * ts * pack_rows + c, ts, stride=pack_rows), :]
                     for c in range(pack_rows)], axis=1)
                xb[s * ts:(s + 1) * ts, :half] = pltpu.bitcast(words << 16, F32).astype(BF16)
                xb[s * ts:(s + 1) * ts, half:] = pltpu.bitcast(
                    words & jnp.uint32(HIGH16), F32).astype(BF16)
                acc[s * ts:(s + 1) * ts, :] = jnp.broadcast_to(bd_ref[...], (ts, d))

    @pl.when(nsub > 0)
    def _():
        wg_b[...] = wg_ref[...].astype(BF16)
        wu_b[...] = wu_ref[...].astype(BF16)
        wd_b[...] = wd_ref[...].astype(BF16)

        def sub_block(r):
            x = xb[pl.ds(r, ts), :]
            acts = []
            for c0 in range(0, tf, ROW_ALIGN):
                g = _dot(x, wg_b[:, c0:c0 + ROW_ALIGN]) + bg_ref[:, c0:c0 + ROW_ALIGN]
                up = _dot(x, wu_b[:, c0:c0 + ROW_ALIGN]) + bu_ref[:, c0:c0 + ROW_ALIGN]
                g = jnp.minimum(g, SWIGLU_LIMIT)
                up = jnp.clip(up, -SWIGLU_LIMIT, SWIGLU_LIMIT)
                acts.append(((up + 1.0) * g * _sigmoid(SWIGLU_ALPHA * g)).astype(BF16))
            acc[pl.ds(r, ts), :] += _dot(jnp.concatenate(acts, axis=1), wd_b[...])

        def pair(p, _):
            r = pl.multiple_of(p * (2 * ts), 2 * ts)
            sub_block(r)
            sub_block(r + ts)
            return 0
        lax.fori_loop(0, nsub // 2, pair, 0)

        @pl.when(nsub % 2 == 1)
        def _():
            sub_block(pl.multiple_of((nsub - 1) * ts, ts))

    @pl.when((f == n_f - 1) & (nsub > 0))
    def _():
        for s in range(n_sub_max):
            @pl.when(s < nsub)
            def _():
                a = acc[s * ts:(s + 1) * ts, :]
                for c in range(out_rows):
                    ystage[pl.ds(c, ts, stride=out_rows), :] = a[:, c * LANES:(c + 1) * LANES]
                cp = pltpu.make_async_copy(
                    ystage, ys_ref.at[pl.ds((row0 + s * ts) * out_rows, ts * out_rows), :], sem_out)
                cp.start()
                cp.wait()


def _experts(unit_e, unit_row0, unit_nsub, used_rows, xs, w_gate_up, b_gate_up, w_down, b_down,
             n_slots, pack_rows):
    n_exp, d, two_de = w_gate_up.shape
    de = two_de // 2
    tf = EXP_FTILE
    n_f = de // tf
    out_rows = d // LANES
    n_units = unit_e.shape[0]
    last_f = n_f - 1

    def fidx(u, f, un):
        return jnp.where(un[u] > 0, f, last_f)

    grid_spec = pltpu.PrefetchScalarGridSpec(
        num_scalar_prefetch=4,
        grid=(n_units, n_f),
        in_specs=[
            pl.BlockSpec(memory_space=pl.ANY),
            pl.BlockSpec((None, d, tf), lambda u, f, ue, ur, un, us: (ue[u], 0, fidx(u, f, un))),
            pl.BlockSpec((None, d, tf),
                         lambda u, f, ue, ur, un, us: (ue[u], 0, n_f + fidx(u, f, un))),
            pl.BlockSpec((None, 1, tf), lambda u, f, ue, ur, un, us: (ue[u], 0, fidx(u, f, un))),
            pl.BlockSpec((None, 1, tf),
                         lambda u, f, ue, ur, un, us: (ue[u], 0, n_f + fidx(u, f, un))),
            pl.BlockSpec((None, tf, d), lambda u, f, ue, ur, un, us: (ue[u], fidx(u, f, un), 0)),
            pl.BlockSpec((None, 1, d), lambda u, f, ue, ur, un, us: (ue[u], 0, 0)),
        ],
        out_specs=pl.BlockSpec(memory_space=pl.ANY),
        scratch_shapes=[
            pltpu.VMEM((EXP_UNIT * pack_rows, LANES), U32),
            pltpu.VMEM((EXP_UNIT, d), BF16),
            pltpu.VMEM((EXP_UNIT, d), F32),
            pltpu.VMEM((d, tf), BF16),
            pltpu.VMEM((d, tf), BF16),
            pltpu.VMEM((tf, d), BF16),
            pltpu.VMEM((EXP_SUB * out_rows, LANES), F32),
            pltpu.SemaphoreType.DMA((EXP_UNIT // EXP_SUB,)),
            pltpu.SemaphoreType.DMA(()),
        ],
    )
    return pl.pallas_call(
        functools.partial(_expert_kernel, pack_rows=pack_rows, out_rows=out_rows, n_slots=n_slots),
        grid_spec=grid_spec,
        out_shape=jax.ShapeDtypeStruct((n_slots * out_rows, LANES), F32),
        compiler_params=_cparams(2),
        name="experts",
    )(unit_e, unit_row0, unit_nsub, used_rows, xs, w_gate_up, w_gate_up,
      b_gate_up.reshape(n_exp, 1, two_de), b_gate_up.reshape(n_exp, 1, two_de),
      w_down, b_down.reshape(n_exp, 1, d))


def _combine_kernel(slot_ref, gate_ref, h1_ref, ys_ref, o_ref, ybuf, sem, *, out_rows):
    tc = COMB_ROWS

    def copy(tok, j):
        s = slot_ref[0, tok * TOP_K + j]
        return pltpu.make_async_copy(
            ys_ref.at[pl.ds(s * out_rows, out_rows), :],
            ybuf.at[j, pl.ds(tok * out_rows, out_rows), :], sem)

    def start(tok, _):
        for j in range(TOP_K):
            copy(tok, j).start()
        return 0

    def wait(tok, _):
        for j in range(TOP_K):
            copy(tok, j).wait()
        return 0

    lax.fori_loop(0, tc, start, 0)
    lax.fori_loop(0, tc, wait, 0)

    gates = gate_ref[...]
    for c in range(out_rows):
        acc = h1_ref[:, c * LANES:(c + 1) * LANES]
        for j in range(TOP_K):
            acc = acc + gates[:, j:j + 1] * ybuf[j, pl.ds(c, tc, stride=out_rows), :]
        o_ref[:, c * LANES:(c + 1) * LANES] = acc


def _combine(slots, gates, h1, ys, bsz, seq, lp, d):
    tc = COMB_ROWS
    rows = slots.shape[0]
    out_rows = d // LANES
    slots3 = slots.reshape(rows // tc, 1, tc * TOP_K)
    nb = lp // tc
    first = (lp - seq) // tc
    return pl.pallas_call(
        functools.partial(_combine_kernel, out_rows=out_rows),
        grid=(bsz, seq // tc),
        in_specs=[
            pl.BlockSpec((None, 1, tc * TOP_K), lambda b, i: (b * nb + first + i, 0, 0),
                         memory_space=pltpu.SMEM),
            pl.BlockSpec((tc, LANES), lambda b, i: (b * nb + first + i, 0)),
            pl.BlockSpec((tc, d), lambda b, i: (b * nb + first + i, 0)),
            pl.BlockSpec(memory_space=pl.ANY),
        ],
        out_specs=pl.BlockSpec((None, tc, d), lambda b, i: (b, i, 0)),
        out_shape=jax.ShapeDtypeStruct((bsz, seq, d), F32),
        scratch_shapes=[pltpu.VMEM((TOP_K, tc * out_rows, LANES), F32), pltpu.SemaphoreType.DMA(())],
        compiler_params=_cparams(2),
        name="combine",
    )(slots3, gates, h1, ys)


def _pick(n, candidates):
    for c in candidates:
        if n % c == 0:
            return c
    raise ValueError(f"no block size in {candidates} divides {n}")


def _plan(info, cnt, n_exp, rows, lp, front, n_assign):
    counts = cnt[0, :n_exp].astype(I32)
    padded = (counts + EXP_SUB - 1) // EXP_SUB * EXP_SUB
    pstart = jnp.cumsum(padded) - padded
    eid = info[:, :TOP_K]
    rank = info[:, TOP_K:2 * TOP_K]
    onehot = eid[:, :, None] == jnp.arange(n_exp, dtype=I32)[None, None, :]
    slot = rank + jnp.sum(jnp.where(onehot, pstart[None, None, :], 0), axis=-1)
    row_valid = (jnp.arange(rows, dtype=I32) % lp) >= front
    slots = jnp.where(row_valid[:, None], slot, -1).astype(I32)

    n_slots = (n_assign + n_exp * (EXP_SUB - 1)) // EXP_SUB * EXP_SUB
    units_per_e = (padded + EXP_UNIT - 1) // EXP_UNIT
    cum_units = jnp.cumsum(units_per_e)
    n_units = n_slots // EXP_UNIT + n_exp
    uidx = jnp.arange(n_units, dtype=I32)
    ue = jnp.sum(cum_units[None, :] <= uidx[:, None], axis=1).astype(I32)
    live = ue < n_exp
    last_e = jnp.max(jnp.where(counts > 0, jnp.arange(n_exp, dtype=I32), 0))
    ue = jnp.where(live, ue, last_e)
    k_in_e = uidx - (cum_units - units_per_e)[ue]
    unit_row0 = jnp.where(live, pstart[ue] + k_in_e * EXP_UNIT, 0).astype(I32)
    unit_nsub = jnp.where(
        live, jnp.clip((padded[ue] - k_in_e * EXP_UNIT) // EXP_SUB, 0, EXP_UNIT // EXP_SUB),
        0).astype(I32)
    used_rows = jnp.sum(padded).astype(I32)[None]
    return slots, (pstart.astype(I32), padded.astype(I32)), (ue, unit_row0, unit_nsub, used_rows), n_slots


def kernel(x, meta_tokens, mix_norm_w, w_in, conv_w, a_log, dt_bias, gdn_norm_w, sb_q_norm_w,
           sb_k_norm_w, sb_out_norm_w, w_out, ffn_norm_w, router_w, router_b, w_gate_up, b_gate_up,
           w_down, b_down):
    bsz, seq, d = x.shape
    n_meta = meta_tokens.shape[0]
    depth = mix_norm_w.shape[0]
    heads = a_log.shape[1]
    hd = gdn_norm_w.shape[1]
    gd = heads * hd
    sbd = (w_in.shape[2] - 4 * gd - 2 * heads) // 3
    sb_heads = sbd // hd
    n_exp = router_w.shape[2]
    assert seq % ROW_ALIGN == 0 and d % (2 * 8 * LANES) == 0 and hd == LANES
    assert 2 * heads <= LANES and n_exp <= LANES and sb_heads == heads
    assert depth == 1, "a second layer would need the meta rows carried through the combine stage"
    front = (-n_meta) % ROW_ALIGN
    lp = front + n_meta + seq
    rows = bsz * lp
    pack_rows = d // 2 // LANES

    h = jnp.concatenate([
        jnp.zeros((bsz, front, d), x.dtype),
        jnp.broadcast_to(meta_tokens.astype(x.dtype)[None], (bsz, n_meta, d)),
        x], axis=1).reshape(rows, d)

    wl = w_in[0]
    n_ba = 4 * gd
    w_main = jnp.concatenate([wl[:, :n_ba], wl[:, n_ba + 2 * heads:]], axis=1).astype(BF16)
    w_ba = jnp.pad(wl[:, n_ba:n_ba + 2 * heads], ((0, 0), (0, LANES - 2 * heads))).astype(BF16)
    tm = _pick(rows, (1024, 512, 256))
    tn = _pick(w_main.shape[1], (512, 256, 128))
    proj, ba = _in_proj(h, mix_norm_w[0][None], w_main, w_ba, tm, tn)
    proj3 = proj.reshape(bsz, lp, -1)
    ba3 = ba.reshape(bsz, lp, LANES)

    gparams = jnp.zeros((2, LANES), F32)
    gparams = gparams.at[0, heads:2 * heads].set(a_log[0].astype(F32))
    gparams = gparams.at[1, heads:2 * heads].set(dt_bias[0].astype(F32))
    qkv3, bg3 = _gdn_prep(proj3, ba3, conv_w[0].astype(F32), gparams, heads, hd)
    o_gdn = _gdn(qkv3, proj3, bg3, gdn_norm_w[0][None], heads, hd)
    o_sb = _sb(proj3, sb_q_norm_w[0][None], sb_k_norm_w[0][None], sb_out_norm_w[0][None],
               sb_heads, hd, 4 * gd, front)

    rw = jnp.pad(router_w[0].astype(F32), ((0, 0), (0, LANES - n_exp)))
    rb = jnp.pad(router_b[0].astype(F32), (0, LANES - n_exp), constant_values=-1e30)[None]
    h1, hnp, info, gates, cnt = _mix(
        o_gdn.reshape(rows, gd), o_sb.reshape(rows, sbd), h, w_out[0].astype(BF16),
        ffn_norm_w[0][None], rw, rb, front, lp)

    slots, (pstart, padded), units, n_slots = _plan(
        info, cnt, n_exp, rows, lp, front, bsz * (n_meta + seq) * TOP_K)
    xs = _dispatch(slots, hnp, pstart, padded, units[3], n_slots, pack_rows, front, lp)
    ys = _experts(*units, xs, w_gate_up[0], b_gate_up[0], w_down[0], b_down[0], n_slots, pack_rows)
    return _combine(slots, gates, h1, ys, bsz, seq, lp, d)
```

```python
import functools

import jax
import jax.numpy as jnp
from jax import lax
from jax.experimental import pallas as pl
from jax.experimental.pallas import tpu as pltpu

F32 = jnp.float32
BF16 = jnp.bfloat16
U32 = jnp.uint32
I32 = jnp.int32

NORM_EPS = 1e-6
TOP_K = 4
SWIGLU_LIMIT = 7.0
SWIGLU_ALPHA = 1.702
LANES = 128
ROW_ALIGN = 256
GDN_PREP_ROWS = 256
GDN_CHUNK = 64
SB_BLOCK = 256
MIX_ROWS = 256
EXP_SUB = 256
EXP_UNIT = 1024
EXP_FTILE = 512
COMB_ROWS = 128
VMEM_LIMIT = 56 * 1024 * 1024
HIGH16 = 0xFFFF0000
LOG2E = 1.4426950408889634


def _cparams(n_grid, vmem=VMEM_LIMIT):
    return pltpu.CompilerParams(dimension_semantics=("arbitrary",) * n_grid, vmem_limit_bytes=vmem)


def _dot(a, b):
    return jnp.dot(a, b, preferred_element_type=F32)


def _dot_nt(a, b):
    return lax.dot_general(a, b, (((1,), (1,)), ((), ())), preferred_element_type=F32)


def _dot_tn(a, b):
    return lax.dot_general(a, b, (((0,), (0,)), ((), ())), preferred_element_type=F32)


def _split3(x):
    hi = x.astype(BF16)
    r1 = x - hi.astype(F32)
    mid = r1.astype(BF16)
    lo = (r1 - mid.astype(F32)).astype(BF16)
    return hi, mid, lo


def _dot3(a, b):
    ah, am, al = _split3(a)
    bh, bm, bl = _split3(b)
    return (_dot(ah, bh) + (_dot(ah, bm) + _dot(am, bh))
            + (_dot(am, bm) + _dot(ah, bl) + _dot(al, bh)))


def _sigmoid(x):
    return 1.0 / (1.0 + jnp.exp(-x))


def _softplus(x):
    return jnp.maximum(x, 0.0) + jnp.log1p(jnp.exp(-jnp.abs(x)))


def _inproj_kernel(h_ref, nw_ref, w_ref, wba_ref, o_ref, ba_ref, xn_ref):
    @pl.when(pl.program_id(1) == 0)
    def _():
        x = h_ref[...]
        xn = x * lax.rsqrt(jnp.mean(x * x, axis=-1, keepdims=True) + NORM_EPS) * nw_ref[...]
        xn = xn.astype(BF16)
        xn_ref[...] = xn
        ba_ref[...] = _dot(xn, wba_ref[...])

    o_ref[...] = _dot(xn_ref[...], w_ref[...]).astype(o_ref.dtype)


def _in_proj(h0, norm_w, w_main, w_ba, tm, tn):
    rows, d = h0.shape
    n_main = w_main.shape[1]
    return pl.pallas_call(
        _inproj_kernel,
        grid=(rows // tm, n_main // tn),
        in_specs=[
            pl.BlockSpec((tm, d), lambda i, n: (i, 0)),
            pl.BlockSpec((1, d), lambda i, n: (0, 0)),
            pl.BlockSpec((d, tn), lambda i, n: (0, n)),
            pl.BlockSpec((d, LANES), lambda i, n: (0, 0)),
        ],
        out_specs=[
            pl.BlockSpec((tm, tn), lambda i, n: (i, n)),
            pl.BlockSpec((tm, LANES), lambda i, n: (i, 0)),
        ],
        out_shape=[
            jax.ShapeDtypeStruct((rows, n_main), BF16),
            jax.ShapeDtypeStruct((rows, LANES), F32),
        ],
        scratch_shapes=[pltpu.VMEM((tm, d), BF16)],
        compiler_params=_cparams(2),
        name="in_proj",
    )(h0, norm_w, w_main, w_ba)


def _gdn_prep_kernel(x_ref, prev_ref, ba_ref, cw_ref, gp_ref, o_ref, bg_ref, *, heads, hd):
    tp = x_ref.shape[0]
    gd = heads * hd
    taps = cw_ref.shape[0]
    x = x_ref[...].astype(F32)
    prev = jnp.where(pl.program_id(1) > 0, prev_ref[8:16, :].astype(F32), 0.0)
    xs = jnp.concatenate([prev, x], axis=0)
    y = cw_ref[taps - 1:taps, :] * x
    for s in range(1, taps):
        y = y + cw_ref[taps - 1 - s:taps - s, :] * pltpu.roll(xs, s, axis=0)[8:8 + tp, :]
    y = y * _sigmoid(y)
    for h in range(heads):
        q = y[:, h * hd:(h + 1) * hd]
        k = y[:, gd + h * hd:gd + (h + 1) * hd]
        q = q * (lax.rsqrt(jnp.sum(q * q, axis=-1, keepdims=True) + NORM_EPS) * (hd ** -0.5))
        k = k * lax.rsqrt(jnp.sum(k * k, axis=-1, keepdims=True) + NORM_EPS)
        o_ref[:, h * hd:(h + 1) * hd] = q.astype(o_ref.dtype)
        o_ref[:, gd + h * hd:gd + (h + 1) * hd] = k.astype(o_ref.dtype)
    o_ref[:, 2 * gd:] = y[:, 2 * gd:].astype(o_ref.dtype)
    ba = ba_ref[...]
    lane = lax.broadcasted_iota(I32, ba.shape, 1)
    decay = -jnp.exp(gp_ref[0:1, :]) * _softplus(ba + gp_ref[1:2, :])
    bg_ref[...] = jnp.where(lane < heads, _sigmoid(ba), decay)


def _gdn_prep(proj3, ba3, conv_w, gparams, heads, hd):
    bsz, lp, _ = proj3.shape
    gd = heads * hd
    tp = GDN_PREP_ROWS
    return pl.pallas_call(
        functools.partial(_gdn_prep_kernel, heads=heads, hd=hd),
        grid=(bsz, lp // tp),
        in_specs=[
            pl.BlockSpec((None, tp, 3 * gd), lambda b, i: (b, i, 0)),
            pl.BlockSpec((None, 16, 3 * gd), lambda b, i: (b, jnp.maximum(i * (tp // 16) - 1, 0), 0)),
            pl.BlockSpec((None, tp, LANES), lambda b, i: (b, i, 0)),
            pl.BlockSpec(conv_w.shape, lambda b, i: (0, 0)),
            pl.BlockSpec((2, LANES), lambda b, i: (0, 0)),
        ],
        out_specs=[
            pl.BlockSpec((None, tp, 3 * gd), lambda b, i: (b, i, 0)),
            pl.BlockSpec((None, tp, LANES), lambda b, i: (b, i, 0)),
        ],
        out_shape=[
            jax.ShapeDtypeStruct((bsz, lp, 3 * gd), BF16),
            jax.ShapeDtypeStruct((bsz, lp, LANES), F32),
        ],
        compiler_params=_cparams(2),
        name="gdn_prep",
    )(proj3, proj3, ba3, conv_w, gparams)


def _gdn_kernel(qkv_ref, z_ref, bg_ref, nw_ref, o_ref, state, *, heads, hd):
    c_rows = GDN_CHUNK
    gd = heads * hd

    @pl.when(pl.program_id(1) == 0)
    def _():
        state[...] = jnp.zeros_like(state)

    bg = bg_ref[...]
    r_i = lax.broadcasted_iota(I32, (c_rows, c_rows), 0)
    c_i = lax.broadcasted_iota(I32, (c_rows, c_rows), 1)
    incl = r_i >= c_i
    strict = r_i > c_i
    tri = incl.astype(BF16)
    g_parts = _split3(bg)
    gcum = _dot(tri, g_parts[0]) + _dot(tri, g_parts[1]) + _dot(tri, g_parts[2])
    n_sel = -(-heads // 8) * 8
    pick = (lax.broadcasted_iota(I32, (n_sel, LANES), 1)
            == lax.broadcasted_iota(I32, (n_sel, LANES), 0) + heads).astype(BF16)
    c_parts = _split3(gcum)
    gcum_rows = _dot_nt(pick, c_parts[0]) + _dot_nt(pick, c_parts[1]) + _dot_nt(pick, c_parts[2])

    hs = range(heads)
    q16 = [qkv_ref[:, h * hd:(h + 1) * hd] for h in hs]
    k16 = [qkv_ref[:, gd + h * hd:gd + (h + 1) * hd] for h in hs]
    k = [k16[h].astype(F32) for h in hs]
    gc = [gcum[:, heads + h:heads + h + 1] for h in hs]
    decay = [jnp.exp(jnp.minimum(gc[h] - gcum_rows[h:h + 1, :], 0.0)) for h in hs]
    kb = [k[h] * bg[:, h:h + 1] for h in hs]
    p = [jnp.where(strict, _dot_nt(kb[h].astype(BF16), k16[h]) * decay[h], 0.0) for h in hs]
    p = [(-p[h]).astype(BF16) for h in hs]
    sol = [jnp.concatenate(
        [qkv_ref[:, 2 * gd + h * hd:2 * gd + (h + 1) * hd].astype(F32) * bg[:, h:h + 1],
         kb[h] * jnp.exp(gc[h])], axis=1) for h in hs]
    n_fac = c_rows.bit_length() - 1
    for i in range(n_fac):
        sol = [sol[h] + _dot(p[h], sol[h].astype(BF16)) for h in hs]
        if i + 1 < n_fac:
            p = [_dot(p[h], p[h]).astype(BF16) for h in hs]
    attn = [jnp.where(incl, _dot_nt(q16[h], k16[h]) * decay[h], 0.0).astype(BF16) for h in hs]
    s_old = [state[h] for h in hs]
    s_b = [s_old[h].astype(BF16) for h in hs]
    v_new = [(sol[h][:, :hd] - _dot(sol[h][:, hd:].astype(BF16), s_b[h])).astype(BF16) for h in hs]
    o = [_dot((q16[h].astype(F32) * jnp.exp(gc[h])).astype(BF16), s_b[h]) + _dot(attn[h], v_new[h])
         for h in hs]
    for h in hs:
        g_last = gc[h][c_rows - 1:c_rows, :]
        k_dec = (k[h] * jnp.exp(g_last - gc[h])).astype(BF16)
        state[h] = s_old[h] * jnp.exp(g_last) + _dot_tn(k_dec, v_new[h])
    for h in hs:
        zf = z_ref[:, h * hd:(h + 1) * hd].astype(F32)
        o_n = o[h] * lax.rsqrt(jnp.mean(o[h] * o[h], axis=-1, keepdims=True) + NORM_EPS) * nw_ref[...]
        o_ref[:, h * hd:(h + 1) * hd] = (o_n * (zf * _sigmoid(zf))).astype(o_ref.dtype)


def _gdn(qkv3, proj3, bg3, norm_w, heads, hd):
    bsz, lp, _ = qkv3.shape
    gd = heads * hd
    c = GDN_CHUNK
    return pl.pallas_call(
        functools.partial(_gdn_kernel, heads=heads, hd=hd),
        grid=(bsz, lp // c),
        in_specs=[
            pl.BlockSpec((None, c, 3 * gd), lambda b, i: (b, i, 0)),
            pl.BlockSpec((None, c, gd), lambda b, i: (b, i, 3)),
            pl.BlockSpec((None, c, LANES), lambda b, i: (b, i, 0)),
            pl.BlockSpec((1, hd), lambda b, i: (0, 0)),
        ],
        out_specs=pl.BlockSpec((None, c, gd), lambda b, i: (b, i, 0)),
        out_shape=jax.ShapeDtypeStruct((bsz, lp, gd), BF16),
        scratch_shapes=[pltpu.VMEM((heads, hd, hd), F32)],
        compiler_params=_cparams(2),
        name="gdn",
    )(qkv3, proj3, bg3, norm_w)


def _sb_kernel(q_ref, k_ref, v_ref, qw_ref, kw_ref, ow_ref, tri_ref, o_ref, kn_ref, acc_ref,
               drop_ref, *, front, hd):
    t = SB_BLOCK
    qi = pl.program_id(2)
    n_blocks = k_ref.shape[0] // t

    @pl.when(qi == 0)
    def _():
        def norm_keys(i, _):
            kb = k_ref[pl.ds(i * t, t), :].astype(F32)
            kn = kb * lax.rsqrt(jnp.mean(kb * kb, axis=-1, keepdims=True) + NORM_EPS) * kw_ref[...]
            kn_ref[pl.ds(i * t, t), :] = kn.astype(BF16)
            return 0
        lax.fori_loop(0, n_blocks, norm_keys, 0)

    q = q_ref[...].astype(F32)
    qn = q * lax.rsqrt(jnp.mean(q * q, axis=-1, keepdims=True) + NORM_EPS) * qw_ref[...]
    qn = (qn * (hd ** -0.5 * LOG2E)).astype(BF16)

    acc_ref[...] = jnp.zeros_like(acc_ref)
    drop_ref[...] = jnp.zeros_like(drop_ref)

    def mask_of(kind):
        r_i = lax.broadcasted_iota(I32, (t, t), 0)
        c_i = lax.broadcasted_iota(I32, (t, t), 1)
        if kind == "causal":
            return c_i < r_i
        if kind == "front":
            return c_i >= front
        return (c_i < r_i) & (c_i >= front)

    def scores(kj, kind):
        start = pl.multiple_of(kj * t, t)
        s = _dot_nt(qn, kn_ref[pl.ds(start, t), :])
        drop = jnp.maximum(s, 0.0) + jnp.log2(1.0 + jnp.exp2(-jnp.abs(s)))
        visible = None
        if kind is not None:
            visible = mask_of(kind)
            drop = jnp.where(visible, drop, 0.0)
        hi = pltpu.bitcast(pltpu.bitcast(drop, U32) & jnp.uint32(HIGH16), F32)
        lo = drop - hi
        tail = _dot(jnp.concatenate([hi.astype(BF16), lo.astype(BF16)], axis=1), tri_ref[...])
        return s, tail, visible

    def run(tiles):
        parts = [scores(kj, kind) for kj, kind in tiles]
        dropped = drop_ref[...]
        acc = acc_ref[...]
        for (kj, kind), (s, tail, visible) in zip(tiles, parts):
            weight = jnp.exp2(s - tail - jnp.concatenate([dropped] * (t // LANES), axis=1))
            if kind is not None:
                weight = jnp.where(visible, weight, 0.0)
            start = pl.multiple_of(kj * t, t)
            acc = acc + _dot(weight.astype(BF16), v_ref[pl.ds(start, t), :])
            dropped = dropped + jnp.broadcast_to(tail[:, 0:1], dropped.shape)
        acc_ref[...] = acc
        drop_ref[...] = dropped

    @pl.when(qi == 0)
    def _():
        run([(0, "causal_front")])

    @pl.when(qi == 1)
    def _():
        run([(1, "causal"), (0, "front")])

    @pl.when(qi >= 2)
    def _():
        run([(qi, "causal"), (qi - 1, None)])
        rest = qi - 2

        def pair(p, _):
            run([(qi - 2 - 2 * p, None), (qi - 3 - 2 * p, None)])
            return 0
        lax.fori_loop(0, rest // 2, pair, 0)

        @pl.when(rest % 2 == 1)
        def _():
            run([(1, None), (0, "front")])

        @pl.when(rest % 2 == 0)
        def _():
            run([(0, "front")])

    o = acc_ref[...]
    o = o * lax.rsqrt(jnp.mean(o * o, axis=-1, keepdims=True) + NORM_EPS) * ow_ref[...]
    o_ref[...] = o.astype(o_ref.dtype)


def _sb(proj3, qw, kw, ow, heads, hd, col0, front):
    bsz, lp, _ = proj3.shape
    t = SB_BLOCK
    cb = col0 // hd
    tri = (jnp.arange(t)[:, None] >= jnp.arange(t)[None, :])
    tri = jnp.concatenate([tri, tri], axis=0).astype(BF16)
    return pl.pallas_call(
        functools.partial(_sb_kernel, front=front, hd=hd),
        grid=(bsz, heads, lp // t),
        in_specs=[
            pl.BlockSpec((None, t, hd), lambda b, h, i: (b, i, cb + h)),
            pl.BlockSpec((None, lp, hd), lambda b, h, i: (b, 0, cb + heads + h)),
            pl.BlockSpec((None, lp, hd), lambda b, h, i: (b, 0, cb + 2 * heads + h)),
            pl.BlockSpec((1, hd), lambda b, h, i: (0, 0)),
            pl.BlockSpec((1, hd), lambda b, h, i: (0, 0)),
            pl.BlockSpec((1, hd), lambda b, h, i: (0, 0)),
            pl.BlockSpec(tri.shape, lambda b, h, i: (0, 0)),
        ],
        out_specs=pl.BlockSpec((None, t, hd), lambda b, h, i: (b, i, h)),
        out_shape=jax.ShapeDtypeStruct((bsz, lp, heads * hd), BF16),
        scratch_shapes=[pltpu.VMEM((lp, hd), BF16), pltpu.VMEM((t, hd), F32),
                        pltpu.VMEM((t, LANES), F32)],
        compiler_params=_cparams(3),
        name="sb",
    )(proj3, proj3, proj3, qw, kw, ow, tri)


def _mix_kernel(og_ref, os_ref, h_ref, wo_ref, nw_ref, rw_ref, rb_ref,
                h1_ref, hnp_ref, info_ref, gate_ref, cnt_ref, cnt_acc,
                *, front, blocks_per_batch, pack_rows):
    tm = MIX_ROWS
    i = pl.program_id(0)

    @pl.when(i == 0)
    def _():
        cnt_acc[...] = jnp.zeros_like(cnt_acc)

    gd = og_ref.shape[1]
    h1 = h_ref[...] + _dot(og_ref[...], wo_ref[0:gd, :]) + _dot(os_ref[...], wo_ref[gd:, :])
    h1_ref[...] = h1
    hn = h1 * lax.rsqrt(jnp.mean(h1 * h1, axis=-1, keepdims=True) + NORM_EPS) * nw_ref[...]

    half = hn.shape[1] // 2
    lo = pltpu.bitcast(hn[:, :half].astype(BF16).astype(F32), U32) >> 16
    hi = pltpu.bitcast(hn[:, half:].astype(BF16).astype(F32), U32) & jnp.uint32(HIGH16)
    word = lo | hi
    for s in range(pack_rows):
        hnp_ref[pl.ds(s, tm, stride=pack_rows), :] = word[:, s * LANES:(s + 1) * LANES]

    logits = _dot3(hn, rw_ref[...]) + rb_ref[...]
    lane = lax.broadcasted_iota(I32, (tm, LANES), 1)
    row = lax.broadcasted_iota(I32, (tm, 1), 0)
    valid = ((i % blocks_per_batch) * tm + row) >= front
    work = logits
    tops, idxs, hots = [], [], []
    for _ in range(TOP_K):
        m = jnp.max(work, axis=1, keepdims=True)
        idx = jnp.min(jnp.where(work == m, lane, LANES), axis=1, keepdims=True)
        hot = lane == idx
        tops.append(m)
        idxs.append(idx)
        hots.append(hot)
        work = jnp.where(hot, -jnp.inf, work)
    exps = [jnp.exp(m - tops[0]) for m in tops]
    denom = exps[0] + exps[1] + exps[2] + exps[3]
    sel = jnp.zeros((tm, LANES), F32)
    for hot in hots:
        sel = sel + hot.astype(F32)
    sel = jnp.where(valid, sel, 0.0)

    r_i = lax.broadcasted_iota(I32, (tm, tm), 0)
    c_i = lax.broadcasted_iota(I32, (tm, tm), 1)
    before = (r_i > c_i).astype(BF16)
    rank = cnt_acc[0:1, :] + _dot(before, sel.astype(BF16))
    info = jnp.zeros((tm, LANES), I32)
    gates = jnp.zeros((tm, LANES), F32)
    for j in range(TOP_K):
        rank_j = jnp.sum(jnp.where(hots[j], rank, 0.0), axis=1, keepdims=True).astype(I32)
        info = jnp.where(lane == j, idxs[j], info)
        info = jnp.where(lane == TOP_K + j, rank_j, info)
        gates = jnp.where(lane == j, exps[j] / denom, gates)
    info_ref[...] = info
    gate_ref[...] = gates
    cnt_acc[...] = cnt_acc[...] + jnp.sum(sel, axis=0, keepdims=True)
    cnt_ref[...] = cnt_acc[...]


def _mix(o_gdn, o_sb, h0, w_out, norm_w, router_w, router_b, front, lp):
    rows, d = h0.shape
    gd = o_gdn.shape[1]
    tm = MIX_ROWS
    pack_rows = d // 2 // LANES
    kern = functools.partial(_mix_kernel, front=front, blocks_per_batch=lp // tm, pack_rows=pack_rows)
    return pl.pallas_call(
        kern,
        grid=(rows // tm,),
        in_specs=[
            pl.BlockSpec((tm, gd), lambda i: (i, 0)),
            pl.BlockSpec((tm, gd), lambda i: (i, 0)),
            pl.BlockSpec((tm, d), lambda i: (i, 0)),
            pl.BlockSpec(w_out.shape, lambda i: (0, 0)),
            pl.BlockSpec((1, d), lambda i: (0, 0)),
            pl.BlockSpec((d, LANES), lambda i: (0, 0)),
            pl.BlockSpec((1, LANES), lambda i: (0, 0)),
        ],
        out_specs=[
            pl.BlockSpec((tm, d), lambda i: (i, 0)),
            pl.BlockSpec((tm * pack_rows, LANES), lambda i: (i, 0)),
            pl.BlockSpec((tm, LANES), lambda i: (i, 0)),
            pl.BlockSpec((tm, LANES), lambda i: (i, 0)),
            pl.BlockSpec((8, LANES), lambda i: (0, 0)),
        ],
        out_shape=[
            jax.ShapeDtypeStruct((rows, d), F32),
            jax.ShapeDtypeStruct((rows * pack_rows, LANES), U32),
            jax.ShapeDtypeStruct((rows, LANES), I32),
            jax.ShapeDtypeStruct((rows, LANES), F32),
            jax.ShapeDtypeStruct((8, LANES), F32),
        ],
        scratch_shapes=[pltpu.VMEM((8, LANES), F32)],
        compiler_params=_cparams(1),
        name="mix_router",
    )(o_gdn, o_sb, h0, w_out, norm_w, router_w, router_b)


def _dispatch_kernel(pstart_ref, padded_ref, used_ref, slot_ref, hnp_ref, xs_ref, zbuf, sem, zsem,
                     *, pack_rows, front, blocks_per_batch, n_slots):
    tm = MIX_ROWS
    ts = EXP_SUB
    i = pl.program_id(0)
    n_exp = pstart_ref.shape[0]

    @pl.when(i == 0)
    def _():
        zbuf[...] = jnp.zeros_like(zbuf)

        def zero_copy(row):
            return pltpu.make_async_copy(
                zbuf, xs_ref.at[pl.ds(row * pack_rows, ts * pack_rows), :], zsem)

        def pad_block(e, _, *, start):
            @pl.when(padded_ref[e] > 0)
            def _():
                cp = zero_copy(pstart_ref[e] + padded_ref[e] - ts)
                cp.start() if start else cp.wait()
            return 0

        used = used_ref[0]
        n_tail = (n_slots - used) // ts
        lax.fori_loop(0, n_exp, functools.partial(pad_block, start=True), 0)
        lax.fori_loop(0, n_tail, lambda j, c: (zero_copy(used + j * ts).start(), c)[1], 0)
        lax.fori_loop(0, n_exp, functools.partial(pad_block, start=False), 0)
        lax.fori_loop(0, n_tail, lambda j, c: (zero_copy(used + j * ts).wait(), c)[1], 0)

    def copy(tok, j):
        src = pl.multiple_of(tok * pack_rows, pack_rows)
        return pltpu.make_async_copy(
            hnp_ref.at[pl.ds(src, pack_rows), :],
            xs_ref.at[pl.ds(slot_ref[0, tok * TOP_K + j] * pack_rows, pack_rows), :], sem)

    def start(tok, _):
        for j in range(TOP_K):
            copy(tok, j).start()
        return 0

    def wait_all(n_tok):
        for _ in range(TOP_K):
            pltpu.make_async_copy(
                hnp_ref.at[pl.ds(0, n_tok * pack_rows), :],
                xs_ref.at[pl.ds(0, n_tok * pack_rows), :], sem).wait()

    @pl.when(i % blocks_per_batch == 0)
    def _():
        lax.fori_loop(front, tm, start, 0)
        wait_all(tm - front)

    @pl.when(i % blocks_per_batch != 0)
    def _():
        lax.fori_loop(0, tm, start, 0)
        wait_all(tm)


def _dispatch(slots, hnp, pstart, padded, used_rows, n_slots, pack_rows, front, lp):
    rows = slots.shape[0]
    tm = MIX_ROWS
    slots3 = slots.reshape(rows // tm, 1, tm * TOP_K)
    grid_spec = pltpu.PrefetchScalarGridSpec(
        num_scalar_prefetch=3,
        grid=(rows // tm,),
        in_specs=[
            pl.BlockSpec((None, 1, tm * TOP_K), lambda i, *_: (i, 0, 0), memory_space=pltpu.SMEM),
            pl.BlockSpec((tm * pack_rows, LANES), lambda i, *_: (i, 0)),
        ],
        out_specs=pl.BlockSpec(memory_space=pl.ANY),
        scratch_shapes=[pltpu.VMEM((EXP_SUB * pack_rows, LANES), U32),
                        pltpu.SemaphoreType.DMA(()), pltpu.SemaphoreType.DMA(())],
    )
    return pl.pallas_call(
        functools.partial(_dispatch_kernel, pack_rows=pack_rows, front=front,
                          blocks_per_batch=lp // tm, n_slots=n_slots),
        grid_spec=grid_spec,
        out_shape=jax.ShapeDtypeStruct((n_slots * pack_rows, LANES), U32),
        compiler_params=_cparams(1),
        name="dispatch",
    )(pstart, padded, used_rows, slots3, hnp)


def _expert_kernel(ue_ref, ur_ref, un_ref, used_ref, xs_ref, wg_ref, wu_ref, bg_ref, bu_ref, wd_ref,
                   bd_ref, ys_ref, xbuf, xb, acc, wg_b, wu_b, wd_b, ystage, sem_in, sem_out,
                   *, pack_rows, out_rows, n_slots):
    del ue_ref
    ts = EXP_SUB
    n_sub_max = EXP_UNIT // ts
    u = pl.program_id(0)
    f = pl.program_id(1)
    n_f = pl.num_programs(1)
    nsub = un_ref[u]
    row0 = ur_ref[u]
    d = xb.shape[1]
    half = d // 2
    tf = wg_b.shape[1]

    @pl.when((u == 0) & (f == 0))
    def _():
        ystage[...] = jnp.zeros_like(ystage)
        used = used_ref[0]

        def tail_copy(i):
            return pltpu.make_async_copy(
                ystage.at[0], ys_ref.at[pl.ds((used + i * ts) * out_rows, ts * out_rows), :],
                sem_out.at[0])

        n_tail = (n_slots - used) // ts
        lax.fori_loop(0, n_tail, lambda i, c: (tail_copy(i).start(), c)[1], 0)
        lax.fori_loop(0, n_tail, lambda i, c: (tail_copy(i).wait(), c)[1], 0)

    def in_copy(s):
        return pltpu.make_async_copy(
            xs_ref.at[pl.ds((row0 + s * ts) * pack_rows, ts * pack_rows), :],
            xbuf.at[pl.ds(s * ts * pack_rows, ts * pack_rows), :], sem_in.at[s])

    @pl.when((f == 0) & (nsub > 0))
    def _():
        for s in range(n_sub_max):
            @pl.when(s < nsub)
            def _():
                in_copy(s).start()
        for s in range(n_sub_max):
            @pl.when(s < nsub)
            def _():
                in_copy(s).wait()
                words = jnp.concatenate(
                    [xbuf[pl.ds(s * ts * pack_rows + c, ts, stride=pack_rows), :]
                     for c in range(pack_rows)], axis=1)
                xb[s * ts:(s + 1) * ts, :half] = pltpu.bitcast(words << 16, F32).astype(BF16)
                xb[s * ts:(s + 1) * ts, half:] = pltpu.bitcast(
                    words & jnp.uint32(HIGH16), F32).astype(BF16)
                acc[s * ts:(s + 1) * ts, :] = jnp.broadcast_to(bd_ref[...], (ts, d))

    @pl.when(nsub > 0)
    def _():
        wg_b[...] = wg_ref[...].astype(BF16)
        wu_b[...] = wu_ref[...].astype(BF16)
        wd_b[...] = wd_ref[...].astype(BF16)

        def sub_block(r):
            x = xb[pl.ds(r, ts), :]
            acts = []
            for c0 in range(0, tf, ROW_ALIGN):
                g = _dot(x, wg_b[:, c0:c0 + ROW_ALIGN]) + bg_ref[:, c0:c0 + ROW_ALIGN]
                up = _dot(x, wu_b[:, c0:c0 + ROW_ALIGN]) + bu_ref[:, c0:c0 + ROW_ALIGN]
                g = jnp.minimum(g, SWIGLU_LIMIT)
                up = jnp.clip(up, -SWIGLU_LIMIT, SWIGLU_LIMIT)
                acts.append(((up + 1.0) * g * _sigmoid(SWIGLU_ALPHA * g)).astype(BF16))
            acc[pl.ds(r, ts), :] += _dot(jnp.concatenate(acts, axis=1), wd_b[...])

        def pair(p, _):
            r = pl.multiple_of(p * (2 * ts), 2 * ts)
            sub_block(r)
            sub_block(r + ts)
            return 0
        lax.fori_loop(0, nsub // 2, pair, 0)

        @pl.when(nsub % 2 == 1)
        def _():
            sub_block(pl.multiple_of((nsub - 1) * ts, ts))

    def out_copy(s):
        return pltpu.make_async_copy(
            ystage.at[s % 2], ys_ref.at[pl.ds((row0 + s * ts) * out_rows, ts * out_rows), :],
            sem_out.at[s % 2])

    @pl.when((f == n_f - 1) & (nsub > 0))
    def _():
        for s in range(n_sub_max):
            @pl.when(s < nsub)
            def _():
                if s >= 2:
                    out_copy(s - 2).wait()
                a = acc[s * ts:(s + 1) * ts, :]
                for c in range(out_rows):
                    ystage[s % 2, pl.ds(c, ts, stride=out_rows), :] = a[:, c * LANES:(c + 1) * LANES]
                out_copy(s).start()
        for s in range(n_sub_max):
            @pl.when((s < nsub) & (s + 2 >= nsub))
            def _():
                out_copy(s).wait()


def _experts(unit_e, unit_row0, unit_nsub, used_rows, xs, w_gate_up, b_gate_up, w_down, b_down,
             n_slots, pack_rows):
    n_exp, d, two_de = w_gate_up.shape
    de = two_de // 2
    tf = EXP_FTILE
    n_f = de // tf
    out_rows = d // LANES
    n_units = unit_e.shape[0]
    last_f = n_f - 1

    def fidx(u, f, un):
        return jnp.where(un[u] > 0, f, last_f)

    grid_spec = pltpu.PrefetchScalarGridSpec(
        num_scalar_prefetch=4,
        grid=(n_units, n_f),
        in_specs=[
            pl.BlockSpec(memory_space=pl.ANY),
            pl.BlockSpec((None, d, tf), lambda u, f, ue, ur, un, us: (ue[u], 0, fidx(u, f, un))),
            pl.BlockSpec((None, d, tf),
                         lambda u, f, ue, ur, un, us: (ue[u], 0, n_f + fidx(u, f, un))),
            pl.BlockSpec((None, 1, tf), lambda u, f, ue, ur, un, us: (ue[u], 0, fidx(u, f, un))),
            pl.BlockSpec((None, 1, tf),
                         lambda u, f, ue, ur, un, us: (ue[u], 0, n_f + fidx(u, f, un))),
            pl.BlockSpec((None, tf, d), lambda u, f, ue, ur, un, us: (ue[u], fidx(u, f, un), 0)),
            pl.BlockSpec((None, 1, d), lambda u, f, ue, ur, un, us: (ue[u], 0, 0)),
        ],
        out_specs=pl.BlockSpec(memory_space=pl.ANY),
        scratch_shapes=[
            pltpu.VMEM((EXP_UNIT * pack_rows, LANES), U32),
            pltpu.VMEM((EXP_UNIT, d), BF16),
            pltpu.VMEM((EXP_UNIT, d), F32),
            pltpu.VMEM((d, tf), BF16),
            pltpu.VMEM((d, tf), BF16),
            pltpu.VMEM((tf, d), BF16),
            pltpu.VMEM((2, EXP_SUB * out_rows, LANES), F32),
            pltpu.SemaphoreType.DMA((EXP_UNIT // EXP_SUB,)),
            pltpu.SemaphoreType.DMA((2,)),
        ],
    )
    return pl.pallas_call(
        functools.partial(_expert_kernel, pack_rows=pack_rows, out_rows=out_rows, n_slots=n_slots),
        grid_spec=grid_spec,
        out_shape=jax.ShapeDtypeStruct((n_slots * out_rows, LANES), F32),
        compiler_params=_cparams(2),
        name="experts",
    )(unit_e, unit_row0, unit_nsub, used_rows, xs, w_gate_up, w_gate_up,
      b_gate_up.reshape(n_exp, 1, two_de), b_gate_up.reshape(n_exp, 1, two_de),
      w_down, b_down.reshape(n_exp, 1, d))


def _combine_kernel(slot_ref, gate_ref, h1_ref, ys_ref, o_ref, ybuf, sem, *, out_rows):
    tc = COMB_ROWS

    def copy(tok, j):
        s = slot_ref[0, tok * TOP_K + j]
        return pltpu.make_async_copy(
            ys_ref.at[pl.ds(s * out_rows, out_rows), :],
            ybuf.at[j, pl.ds(tok * out_rows, out_rows), :], sem)

    def start(tok, _):
        for j in range(TOP_K):
            copy(tok, j).start()
        return 0

    lax.fori_loop(0, tc, start, 0)
    for j in range(TOP_K):
        pltpu.make_async_copy(ys_ref.at[pl.ds(0, tc * out_rows), :], ybuf.at[j], sem).wait()

    gates = gate_ref[...]
    for c in range(out_rows):
        acc = h1_ref[:, c * LANES:(c + 1) * LANES]
        for j in range(TOP_K):
            acc = acc + gates[:, j:j + 1] * ybuf[j, pl.ds(c, tc, stride=out_rows), :]
        o_ref[:, c * LANES:(c + 1) * LANES] = acc


def _combine(slots, gates, h1, ys, bsz, seq, lp, d):
    tc = COMB_ROWS
    rows = slots.shape[0]
    out_rows = d // LANES
    slots3 = slots.reshape(rows // tc, 1, tc * TOP_K)
    nb = lp // tc
    first = (lp - seq) // tc
    return pl.pallas_call(
        functools.partial(_combine_kernel, out_rows=out_rows),
        grid=(bsz, seq // tc),
        in_specs=[
            pl.BlockSpec((None, 1, tc * TOP_K), lambda b, i: (b * nb + first + i, 0, 0),
                         memory_space=pltpu.SMEM),
            pl.BlockSpec((tc, LANES), lambda b, i: (b * nb + first + i, 0)),
            pl.BlockSpec((tc, d), lambda b, i: (b * nb + first + i, 0)),
            pl.BlockSpec(memory_space=pl.ANY),
        ],
        out_specs=pl.BlockSpec((None, tc, d), lambda b, i: (b, i, 0)),
        out_shape=jax.ShapeDtypeStruct((bsz, seq, d), F32),
        scratch_shapes=[pltpu.VMEM((TOP_K, tc * out_rows, LANES), F32), pltpu.SemaphoreType.DMA(())],
        compiler_params=_cparams(2),
        name="combine",
    )(slots3, gates, h1, ys)


def _pick(n, candidates):
    for c in candidates:
        if n % c == 0:
            return c
    raise ValueError(f"no block size in {candidates} divides {n}")


def _plan(info, cnt, n_exp, rows, lp, front, n_assign):
    counts = cnt[0, :n_exp].astype(I32)
    padded = (counts + EXP_SUB - 1) // EXP_SUB * EXP_SUB
    pstart = jnp.cumsum(padded) - padded
    eid = info[:, :TOP_K]
    rank = info[:, TOP_K:2 * TOP_K]
    onehot = eid[:, :, None] == jnp.arange(n_exp, dtype=I32)[None, None, :]
    slot = rank + jnp.sum(jnp.where(onehot, pstart[None, None, :], 0), axis=-1)
    row_valid = (jnp.arange(rows, dtype=I32) % lp) >= front
    slots = jnp.where(row_valid[:, None], slot, -1).astype(I32)

    n_slots = (n_assign + n_exp * (EXP_SUB - 1)) // EXP_SUB * EXP_SUB
    units_per_e = (padded + EXP_UNIT - 1) // EXP_UNIT
    cum_units = jnp.cumsum(units_per_e)
    n_units = n_slots // EXP_UNIT + n_exp
    uidx = jnp.arange(n_units, dtype=I32)
    ue = jnp.sum(cum_units[None, :] <= uidx[:, None], axis=1).astype(I32)
    live = ue < n_exp
    last_e = jnp.max(jnp.where(counts > 0, jnp.arange(n_exp, dtype=I32), 0))
    ue = jnp.where(live, ue, last_e)
    k_in_e = uidx - (cum_units - units_per_e)[ue]
    nsub_e = (padded // EXP_SUB)[ue]
    units_e = jnp.maximum(units_per_e[ue], 1)
    base = nsub_e // units_e
    extra = nsub_e - base * units_e
    first_sub = k_in_e * base + jnp.minimum(k_in_e, extra)
    unit_row0 = jnp.where(live, pstart[ue] + first_sub * EXP_SUB, 0).astype(I32)
    unit_nsub = jnp.where(live, base + (k_in_e < extra), 0).astype(I32)
    used_rows = jnp.sum(padded).astype(I32)[None]
    return slots, (pstart.astype(I32), padded.astype(I32)), (ue, unit_row0, unit_nsub, used_rows), n_slots


def kernel(x, meta_tokens, mix_norm_w, w_in, conv_w, a_log, dt_bias, gdn_norm_w, sb_q_norm_w,
           sb_k_norm_w, sb_out_norm_w, w_out, ffn_norm_w, router_w, router_b, w_gate_up, b_gate_up,
           w_down, b_down):
    bsz, seq, d = x.shape
    n_meta = meta_tokens.shape[0]
    depth = mix_norm_w.shape[0]
    heads = a_log.shape[1]
    hd = gdn_norm_w.shape[1]
    gd = heads * hd
    sbd = (w_in.shape[2] - 4 * gd - 2 * heads) // 3
    sb_heads = sbd // hd
    n_exp = router_w.shape[2]
    assert seq % ROW_ALIGN == 0 and d % (2 * 8 * LANES) == 0 and hd == LANES
    assert 2 * heads <= LANES and n_exp <= LANES and sb_heads == heads
    assert depth == 1, "a second layer would need the meta rows carried through the combine stage"
    front = (-n_meta) % ROW_ALIGN
    lp = front + n_meta + seq
    rows = bsz * lp
    pack_rows = d // 2 // LANES

    h = jnp.concatenate([
        jnp.zeros((bsz, front, d), x.dtype),
        jnp.broadcast_to(meta_tokens.astype(x.dtype)[None], (bsz, n_meta, d)),
        x], axis=1).reshape(rows, d)

    wl = w_in[0]
    n_ba = 4 * gd
    w_main = jnp.concatenate([wl[:, :n_ba], wl[:, n_ba + 2 * heads:]], axis=1).astype(BF16)
    w_ba = jnp.pad(wl[:, n_ba:n_ba + 2 * heads], ((0, 0), (0, LANES - 2 * heads))).astype(BF16)
    tm = _pick(rows, (1024, 512, 256))
    tn = _pick(w_main.shape[1], (1024, 512, 256, 128))
    proj, ba = _in_proj(h, mix_norm_w[0][None], w_main, w_ba, tm, tn)
    proj3 = proj.reshape(bsz, lp, -1)
    ba3 = ba.reshape(bsz, lp, LANES)

    gparams = jnp.zeros((2, LANES), F32)
    gparams = gparams.at[0, heads:2 * heads].set(a_log[0].astype(F32))
    gparams = gparams.at[1, heads:2 * heads].set(dt_bias[0].astype(F32))
    qkv3, bg3 = _gdn_prep(proj3, ba3, conv_w[0].astype(F32), gparams, heads, hd)
    o_gdn = _gdn(qkv3, proj3, bg3, gdn_norm_w[0][None], heads, hd)
    o_sb = _sb(proj3, sb_q_norm_w[0][None], sb_k_norm_w[0][None], sb_out_norm_w[0][None],
               sb_heads, hd, 4 * gd, front)

    rw = jnp.pad(router_w[0].astype(F32), ((0, 0), (0, LANES - n_exp)))
    rb = jnp.pad(router_b[0].astype(F32), (0, LANES - n_exp), constant_values=-1e30)[None]
    h1, hnp, info, gates, cnt = _mix(
        o_gdn.reshape(rows, gd), o_sb.reshape(rows, sbd), h, w_out[0].astype(BF16),
        ffn_norm_w[0][None], rw, rb, front, lp)

    slots, (pstart, padded), units, n_slots = _plan(
        info, cnt, n_exp, rows, lp, front, bsz * (n_meta + seq) * TOP_K)
    xs = _dispatch(slots, hnp, pstart, padded, units[3], n_slots, pack_rows, front, lp)
    ys = _experts(*units, xs, w_gate_up[0], b_gate_up[0], w_down[0], b_down[0], n_slots, pack_rows)
    return _combine(slots, gates, h1, ys, bsz, seq, lp, d)
```

```python
import functools

import jax
import jax.numpy as jnp
from jax import lax
from jax.experimental import pallas as pl
from jax.experimental.pallas import tpu as pltpu

F32 = jnp.float32
BF16 = jnp.bfloat16
U32 = jnp.uint32
I32 = jnp.int32

NORM_EPS = 1e-6
TOP_K = 4
SWIGLU_LIMIT = 7.0
SWIGLU_ALPHA = 1.702
LANES = 128
ROW_ALIGN = 256
GDN_PREP_ROWS = 256
GDN_CHUNK = 64
SB_BLOCK = 256
MIX_ROWS = 256
EXP_SUB = 256
EXP_UNIT = 9 * EXP_SUB
EXP_FTILE = 256
COMB_ROWS = 256
VMEM_LIMIT = 56 * 1024 * 1024
HIGH16 = 0xFFFF0000
SIGN_BIT = 0x80000000
LOG2E = 1.4426950408889634


def _cparams(n_grid, vmem=VMEM_LIMIT):
    return pltpu.CompilerParams(dimension_semantics=("arbitrary",) * n_grid, vmem_limit_bytes=vmem)


def _dot(a, b):
    return jnp.dot(a, b, preferred_element_type=F32)


def _dot_nt(a, b):
    return lax.dot_general(a, b, (((1,), (1,)), ((), ())), preferred_element_type=F32)


def _dot_tn(a, b):
    return lax.dot_general(a, b, (((0,), (0,)), ((), ())), preferred_element_type=F32)


def _split3(x):
    hi = x.astype(BF16)
    r1 = x - hi.astype(F32)
    mid = r1.astype(BF16)
    lo = (r1 - mid.astype(F32)).astype(BF16)
    return hi, mid, lo


def _dot3(a, b):
    ah = a.astype(BF16)
    al = (a - ah.astype(F32)).astype(BF16)
    bh = b.astype(BF16)
    bl = (b - bh.astype(F32)).astype(BF16)
    return _dot(ah, bh) + (_dot(ah, bl) + _dot(al, bh))


def _sigmoid(x):
    return 1.0 / (1.0 + jnp.exp(-x))


def _softplus(x):
    return jnp.maximum(x, 0.0) + jnp.log1p(jnp.exp(-jnp.abs(x)))


def _inproj_kernel(h_ref, nw_ref, w_ref, wba_ref, o_ref, ba_ref, xn_ref):
    @pl.when(pl.program_id(1) == 0)
    def _():
        x = h_ref[...]
        xn = x * lax.rsqrt(jnp.mean(x * x, axis=-1, keepdims=True) + NORM_EPS) * nw_ref[...]
        xn = xn.astype(BF16)
        xn_ref[...] = xn
        ba_ref[...] = _dot(xn, wba_ref[...])

    o_ref[...] = _dot(xn_ref[...], w_ref[...]).astype(o_ref.dtype)


def _in_proj(h0, norm_w, w_main, w_ba, tm, tn):
    rows, d = h0.shape
    n_main = w_main.shape[1]
    return pl.pallas_call(
        _inproj_kernel,
        grid=(rows // tm, n_main // tn),
        in_specs=[
            pl.BlockSpec((tm, d), lambda i, n: (i, 0)),
            pl.BlockSpec((1, d), lambda i, n: (0, 0)),
            pl.BlockSpec((d, tn), lambda i, n: (0, n)),
            pl.BlockSpec((d, LANES), lambda i, n: (0, 0)),
        ],
        out_specs=[
            pl.BlockSpec((tm, tn), lambda i, n: (i, n)),
            pl.BlockSpec((tm, LANES), lambda i, n: (i, 0)),
        ],
        out_shape=[
            jax.ShapeDtypeStruct((rows, n_main), BF16),
            jax.ShapeDtypeStruct((rows, LANES), F32),
        ],
        scratch_shapes=[pltpu.VMEM((tm, d), BF16)],
        compiler_params=_cparams(2),
        name="in_proj",
    )(h0, norm_w, w_main, w_ba)


def _gdn_prep_kernel(x_ref, prev_ref, ba_ref, cw_ref, gp_ref, o_ref, bg_ref, *, heads, hd):
    tp = x_ref.shape[0]
    gd = heads * hd
    taps = cw_ref.shape[0]
    x = x_ref[...].astype(F32)
    prev = jnp.where(pl.program_id(1) > 0, prev_ref[8:16, :].astype(F32), 0.0)
    xs = jnp.concatenate([prev, x], axis=0)
    y = cw_ref[taps - 1:taps, :] * x
    for s in range(1, taps):
        y = y + cw_ref[taps - 1 - s:taps - s, :] * pltpu.roll(xs, s, axis=0)[8:8 + tp, :]
    y = y * _sigmoid(y)
    for h in range(heads):
        q = y[:, h * hd:(h + 1) * hd]
        k = y[:, gd + h * hd:gd + (h + 1) * hd]
        q = q * (lax.rsqrt(jnp.sum(q * q, axis=-1, keepdims=True) + NORM_EPS) * (hd ** -0.5))
        k = k * lax.rsqrt(jnp.sum(k * k, axis=-1, keepdims=True) + NORM_EPS)
        o_ref[:, h * hd:(h + 1) * hd] = q.astype(o_ref.dtype)
        o_ref[:, gd + h * hd:gd + (h + 1) * hd] = k.astype(o_ref.dtype)
    o_ref[:, 2 * gd:] = y[:, 2 * gd:].astype(o_ref.dtype)
    ba = ba_ref[...]
    lane = lax.broadcasted_iota(I32, ba.shape, 1)
    decay = -jnp.exp(gp_ref[0:1, :]) * _softplus(ba + gp_ref[1:2, :])
    bg_ref[...] = jnp.where(lane < heads, _sigmoid(ba), decay)


def _gdn_prep(proj3, ba3, conv_w, gparams, heads, hd):
    bsz, lp, _ = proj3.shape
    gd = heads * hd
    tp = GDN_PREP_ROWS
    return pl.pallas_call(
        functools.partial(_gdn_prep_kernel, heads=heads, hd=hd),
        grid=(bsz, lp // tp),
        in_specs=[
            pl.BlockSpec((None, tp, 3 * gd), lambda b, i: (b, i, 0)),
            pl.BlockSpec((None, 16, 3 * gd), lambda b, i: (b, jnp.maximum(i * (tp // 16) - 1, 0), 0)),
            pl.BlockSpec((None, tp, LANES), lambda b, i: (b, i, 0)),
            pl.BlockSpec(conv_w.shape, lambda b, i: (0, 0)),
            pl.BlockSpec((2, LANES), lambda b, i: (0, 0)),
        ],
        out_specs=[
            pl.BlockSpec((None, tp, 3 * gd), lambda b, i: (b, i, 0)),
            pl.BlockSpec((None, tp, LANES), lambda b, i: (b, i, 0)),
        ],
        out_shape=[
            jax.ShapeDtypeStruct((bsz, lp, 3 * gd), BF16),
            jax.ShapeDtypeStruct((bsz, lp, LANES), F32),
        ],
        compiler_params=_cparams(2),
        name="gdn_prep",
    )(proj3, proj3, ba3, conv_w, gparams)


def _gdn_kernel(qkv_ref, z_ref, bg_ref, nw_ref, o_ref, state, *, heads, hd):
    c_rows = GDN_CHUNK
    gd = heads * hd

    @pl.when(pl.program_id(1) == 0)
    def _():
        state[...] = jnp.zeros_like(state)

    bg = bg_ref[...]
    r_i = lax.broadcasted_iota(I32, (c_rows, c_rows), 0)
    c_i = lax.broadcasted_iota(I32, (c_rows, c_rows), 1)
    incl = r_i >= c_i
    strict = r_i > c_i
    tri = incl.astype(BF16)
    g_parts = _split3(bg)
    gcum = _dot(tri, g_parts[0]) + _dot(tri, g_parts[1]) + _dot(tri, g_parts[2])
    n_sel = -(-heads // 8) * 8
    pick = (lax.broadcasted_iota(I32, (n_sel, LANES), 1)
            == lax.broadcasted_iota(I32, (n_sel, LANES), 0) + heads).astype(BF16)
    c_parts = _split3(gcum)
    gcum_rows = _dot_nt(pick, c_parts[0]) + _dot_nt(pick, c_parts[1]) + _dot_nt(pick, c_parts[2])

    hs = range(heads)
    q16 = [qkv_ref[:, h * hd:(h + 1) * hd] for h in hs]
    k16 = [qkv_ref[:, gd + h * hd:gd + (h + 1) * hd] for h in hs]
    k = [k16[h].astype(F32) for h in hs]
    gc = [gcum[:, heads + h:heads + h + 1] for h in hs]
    decay = [jnp.exp(jnp.minimum(gc[h] - gcum_rows[h:h + 1, :], 0.0)) for h in hs]
    kb = [k[h] * bg[:, h:h + 1] for h in hs]
    p = [jnp.where(strict, _dot_nt(kb[h].astype(BF16), k16[h]) * decay[h], 0.0) for h in hs]
    p = [(-p[h]).astype(BF16) for h in hs]
    sol = [jnp.concatenate(
        [qkv_ref[:, 2 * gd + h * hd:2 * gd + (h + 1) * hd].astype(F32) * bg[:, h:h + 1],
         kb[h] * jnp.exp(gc[h])], axis=1) for h in hs]
    n_fac = c_rows.bit_length() - 1
    for i in range(n_fac):
        sol = [sol[h] + _dot(p[h], sol[h].astype(BF16)) for h in hs]
        if i + 1 < n_fac:
            p = [_dot(p[h], p[h]).astype(BF16) for h in hs]
    attn = [jnp.where(incl, _dot_nt(q16[h], k16[h]) * decay[h], 0.0).astype(BF16) for h in hs]
    s_old = [state[h] for h in hs]
    s_b = [s_old[h].astype(BF16) for h in hs]
    v_new = [(sol[h][:, :hd] - _dot(sol[h][:, hd:].astype(BF16), s_b[h])).astype(BF16) for h in hs]
    o = [_dot((q16[h].astype(F32) * jnp.exp(gc[h])).astype(BF16), s_b[h]) + _dot(attn[h], v_new[h])
         for h in hs]
    for h in hs:
        g_last = gc[h][c_rows - 1:c_rows, :]
        k_dec = (k[h] * jnp.exp(g_last - gc[h])).astype(BF16)
        state[h] = s_old[h] * jnp.exp(g_last) + _dot_tn(k_dec, v_new[h])
    for h in hs:
        zf = z_ref[:, h * hd:(h + 1) * hd].astype(F32)
        o_n = o[h] * lax.rsqrt(jnp.mean(o[h] * o[h], axis=-1, keepdims=True) + NORM_EPS) * nw_ref[...]
        o_ref[:, h * hd:(h + 1) * hd] = (o_n * (zf * _sigmoid(zf))).astype(o_ref.dtype)


def _gdn(qkv3, proj3, bg3, norm_w, heads, hd):
    bsz, lp, _ = qkv3.shape
    gd = heads * hd
    c = GDN_CHUNK
    return pl.pallas_call(
        functools.partial(_gdn_kernel, heads=heads, hd=hd),
        grid=(bsz, lp // c),
        in_specs=[
            pl.BlockSpec((None, c, 3 * gd), lambda b, i: (b, i, 0)),
            pl.BlockSpec((None, c, gd), lambda b, i: (b, i, 3)),
            pl.BlockSpec((None, c, LANES), lambda b, i: (b, i, 0)),
            pl.BlockSpec((1, hd), lambda b, i: (0, 0)),
        ],
        out_specs=pl.BlockSpec((None, c, gd), lambda b, i: (b, i, 0)),
        out_shape=jax.ShapeDtypeStruct((bsz, lp, gd), BF16),
        scratch_shapes=[pltpu.VMEM((heads, hd, hd), F32)],
        compiler_params=_cparams(2),
        name="gdn",
    )(qkv3, proj3, bg3, norm_w)


def _sb_kernel(q_ref, k_ref, v_ref, qw_ref, kw_ref, ow_ref, tri_ref, o_ref, kn_ref, acc_ref,
               drop_ref, *, front, hd):
    t = SB_BLOCK
    qi = pl.program_id(2)
    n_blocks = k_ref.shape[0] // t

    @pl.when(qi == 0)
    def _():
        def norm_keys(i, _):
            kb = k_ref[pl.ds(i * t, t), :].astype(F32)
            kn = kb * lax.rsqrt(jnp.mean(kb * kb, axis=-1, keepdims=True) + NORM_EPS) * kw_ref[...]
            kn_ref[pl.ds(i * t, t), :] = kn.astype(BF16)
            return 0
        lax.fori_loop(0, n_blocks, norm_keys, 0)

    q = q_ref[...].astype(F32)
    qn = q * lax.rsqrt(jnp.mean(q * q, axis=-1, keepdims=True) + NORM_EPS) * qw_ref[...]
    qn = (qn * (hd ** -0.5 * LOG2E)).astype(BF16)

    acc_ref[...] = jnp.zeros_like(acc_ref)
    drop_ref[...] = jnp.zeros_like(drop_ref)

    def mask_of(kind):
        r_i = lax.broadcasted_iota(I32, (t, t), 0)
        c_i = lax.broadcasted_iota(I32, (t, t), 1)
        if kind == "causal":
            return c_i < r_i
        if kind == "front":
            return c_i >= front
        return (c_i < r_i) & (c_i >= front)

    def scores(kj, kind):
        start = pl.multiple_of(kj * t, t)
        s = _dot_nt(qn, kn_ref[pl.ds(start, t), :])
        neg_abs = pltpu.bitcast(pltpu.bitcast(s, U32) | jnp.uint32(SIGN_BIT), F32)
        drop = jnp.maximum(s, 0.0) + jnp.log2(1.0 + jnp.exp2(neg_abs))
        visible = None
        if kind is not None:
            visible = mask_of(kind)
            drop = jnp.where(visible, drop, 0.0)
        hi = pltpu.bitcast(pltpu.bitcast(drop, U32) & jnp.uint32(HIGH16), F32)
        lo = drop - hi
        tail = _dot(jnp.concatenate([hi.astype(BF16), lo.astype(BF16)], axis=1), tri_ref[...])
        return s, tail, visible

    def run(tiles):
        parts = [scores(kj, kind) for kj, kind in tiles]
        dropped = drop_ref[...]
        acc = acc_ref[...]
        for (kj, kind), (s, tail, visible) in zip(tiles, parts):
            weight = jnp.exp2(s - tail - jnp.concatenate([dropped] * (t // LANES), axis=1))
            if kind is not None:
                weight = jnp.where(visible, weight, 0.0)
            start = pl.multiple_of(kj * t, t)
            acc = acc + _dot(weight.astype(BF16), v_ref[pl.ds(start, t), :])
            dropped = dropped + jnp.broadcast_to(tail[:, 0:1], dropped.shape)
        acc_ref[...] = acc
        drop_ref[...] = dropped

    @pl.when(qi == 0)
    def _():
        run([(0, "causal_front")])

    @pl.when(qi == 1)
    def _():
        run([(1, "causal"), (0, "front")])

    @pl.when(qi >= 2)
    def _():
        run([(qi, "causal"), (qi - 1, None)])
        rest = qi - 2

        def pair(p, _):
            run([(qi - 2 - 2 * p, None), (qi - 3 - 2 * p, None)])
            return 0
        lax.fori_loop(0, rest // 2, pair, 0)

        @pl.when(rest % 2 == 1)
        def _():
            run([(1, None), (0, "front")])

        @pl.when(rest % 2 == 0)
        def _():
            run([(0, "front")])

    o = acc_ref[...]
    o = o * lax.rsqrt(jnp.mean(o * o, axis=-1, keepdims=True) + NORM_EPS) * ow_ref[...]
    o_ref[...] = o.astype(o_ref.dtype)


def _sb(proj3, qw, kw, ow, heads, hd, col0, front):
    bsz, lp, _ = proj3.shape
    t = SB_BLOCK
    cb = col0 // hd
    tri = (jnp.arange(t)[:, None] >= jnp.arange(t)[None, :])
    tri = jnp.concatenate([tri, tri], axis=0).astype(BF16)
    return pl.pallas_call(
        functools.partial(_sb_kernel, front=front, hd=hd),
        grid=(bsz, heads, lp // t),
        in_specs=[
            pl.BlockSpec((None, t, hd), lambda b, h, i: (b, i, cb + h)),
            pl.BlockSpec((None, lp, hd), lambda b, h, i: (b, 0, cb + heads + h)),
            pl.BlockSpec((None, lp, hd), lambda b, h, i: (b, 0, cb + 2 * heads + h)),
            pl.BlockSpec((1, hd), lambda b, h, i: (0, 0)),
            pl.BlockSpec((1, hd), lambda b, h, i: (0, 0)),
            pl.BlockSpec((1, hd), lambda b, h, i: (0, 0)),
            pl.BlockSpec(tri.shape, lambda b, h, i: (0, 0)),
        ],
        out_specs=pl.BlockSpec((None, t, hd), lambda b, h, i: (b, i, h)),
        out_shape=jax.ShapeDtypeStruct((bsz, lp, heads * hd), BF16),
        scratch_shapes=[pltpu.VMEM((lp, hd), BF16), pltpu.VMEM((t, hd), F32),
                        pltpu.VMEM((t, LANES), F32)],
        compiler_params=_cparams(3),
        name="sb",
    )(proj3, proj3, proj3, qw, kw, ow, tri)


def _mix_kernel(og_ref, os_ref, h_ref, wo_ref, nw_ref, rw_ref, rb_ref,
                h1_ref, hnp_ref, info_ref, gate_ref, cnt_ref, cnt_acc,
                *, front, blocks_per_batch, pack_rows):
    tm = MIX_ROWS
    i = pl.program_id(0)

    @pl.when(i == 0)
    def _():
        cnt_acc[...] = jnp.zeros_like(cnt_acc)

    gd = og_ref.shape[1]
    h1 = h_ref[...] + _dot(og_ref[...], wo_ref[0:gd, :]) + _dot(os_ref[...], wo_ref[gd:, :])
    h1_ref[...] = h1
    hn = h1 * lax.rsqrt(jnp.mean(h1 * h1, axis=-1, keepdims=True) + NORM_EPS) * nw_ref[...]

    half = hn.shape[1] // 2
    lo = pltpu.bitcast(hn[:, :half].astype(BF16).astype(F32), U32) >> 16
    hi = pltpu.bitcast(hn[:, half:].astype(BF16).astype(F32), U32) & jnp.uint32(HIGH16)
    word = lo | hi
    for s in range(pack_rows):
        hnp_ref[pl.ds(s, tm, stride=pack_rows), :] = word[:, s * LANES:(s + 1) * LANES]

    logits = _dot3(hn, rw_ref[...]) + rb_ref[...]
    lane = lax.broadcasted_iota(I32, (tm, LANES), 1)
    row = lax.broadcasted_iota(I32, (tm, 1), 0)
    valid = ((i % blocks_per_batch) * tm + row) >= front
    work = logits
    tops, idxs, hots = [], [], []
    for _ in range(TOP_K):
        m = jnp.max(work, axis=1, keepdims=True)
        idx = jnp.min(jnp.where(work == m, lane, LANES), axis=1, keepdims=True)
        hot = lane == idx
        tops.append(m)
        idxs.append(idx)
        hots.append(hot)
        work = jnp.where(hot, -jnp.inf, work)
    exps = [jnp.exp(m - tops[0]) for m in tops]
    denom = exps[0] + exps[1] + exps[2] + exps[3]
    sel = jnp.zeros((tm, LANES), F32)
    for hot in hots:
        sel = sel + hot.astype(F32)
    sel = jnp.where(valid, sel, 0.0)

    r_i = lax.broadcasted_iota(I32, (tm, tm), 0)
    c_i = lax.broadcasted_iota(I32, (tm, tm), 1)
    before = (r_i > c_i).astype(BF16)
    rank = cnt_acc[0:1, :] + _dot(before, sel.astype(BF16))
    info = jnp.zeros((tm, LANES), I32)
    gates = jnp.zeros((tm, LANES), F32)
    for j in range(TOP_K):
        rank_j = jnp.sum(jnp.where(hots[j], rank, 0.0), axis=1, keepdims=True).astype(I32)
        info = jnp.where(lane == j, idxs[j], info)
        info = jnp.where(lane == TOP_K + j, rank_j, info)
        gates = jnp.where(lane == j, exps[j] / denom, gates)
    info_ref[...] = info
    gate_ref[...] = gates
    cnt_acc[...] = cnt_acc[...] + jnp.sum(sel, axis=0, keepdims=True)
    cnt_ref[...] = cnt_acc[...]


def _mix(o_gdn, o_sb, h0, w_out, norm_w, router_w, router_b, front, lp):
    rows, d = h0.shape
    gd = o_gdn.shape[1]
    tm = MIX_ROWS
    pack_rows = d // 2 // LANES
    kern = functools.partial(_mix_kernel, front=front, blocks_per_batch=lp // tm, pack_rows=pack_rows)
    return pl.pallas_call(
        kern,
        grid=(rows // tm,),
        in_specs=[
            pl.BlockSpec((tm, gd), lambda i: (i, 0)),
            pl.BlockSpec((tm, gd), lambda i: (i, 0)),
            pl.BlockSpec((tm, d), lambda i: (i, 0)),
            pl.BlockSpec(w_out.shape, lambda i: (0, 0)),
            pl.BlockSpec((1, d), lambda i: (0, 0)),
            pl.BlockSpec((d, LANES), lambda i: (0, 0)),
            pl.BlockSpec((1, LANES), lambda i: (0, 0)),
        ],
        out_specs=[
            pl.BlockSpec((tm, d), lambda i: (i, 0)),
            pl.BlockSpec((tm * pack_rows, LANES), lambda i: (i, 0)),
            pl.BlockSpec((tm, LANES), lambda i: (i, 0)),
            pl.BlockSpec((tm, LANES), lambda i: (i, 0)),
            pl.BlockSpec((8, LANES), lambda i: (0, 0)),
        ],
        out_shape=[
            jax.ShapeDtypeStruct((rows, d), F32),
            jax.ShapeDtypeStruct((rows * pack_rows, LANES), U32),
            jax.ShapeDtypeStruct((rows, LANES), I32),
            jax.ShapeDtypeStruct((rows, LANES), F32),
            jax.ShapeDtypeStruct((8, LANES), F32),
        ],
        scratch_shapes=[pltpu.VMEM((8, LANES), F32)],
        compiler_params=_cparams(1),
        name="mix_router",
    )(o_gdn, o_sb, h0, w_out, norm_w, router_w, router_b)


def _dispatch_kernel(pstart_ref, padded_ref, used_ref, slot_ref, hnp_ref, xs_ref, zbuf, sem, zsem,
                     *, pack_rows, front, blocks_per_batch, n_slots):
    tm = MIX_ROWS
    ts = EXP_SUB
    i = pl.program_id(0)
    n_exp = pstart_ref.shape[0]

    @pl.when(i == 0)
    def _():
        zbuf[...] = jnp.zeros_like(zbuf)

        def zero_copy(row):
            return pltpu.make_async_copy(
                zbuf, xs_ref.at[pl.ds(row * pack_rows, ts * pack_rows), :], zsem)

        def pad_block(e, _, *, start):
            @pl.when(padded_ref[e] > 0)
            def _():
                cp = zero_copy(pstart_ref[e] + padded_ref[e] - ts)
                cp.start() if start else cp.wait()
            return 0

        used = used_ref[0]
        n_tail = (n_slots - used) // ts
        lax.fori_loop(0, n_exp, functools.partial(pad_block, start=True), 0)
        lax.fori_loop(0, n_tail, lambda j, c: (zero_copy(used + j * ts).start(), c)[1], 0)
        lax.fori_loop(0, n_exp, functools.partial(pad_block, start=False), 0)
        lax.fori_loop(0, n_tail, lambda j, c: (zero_copy(used + j * ts).wait(), c)[1], 0)

    def copy(tok, j):
        src = pl.multiple_of(tok * pack_rows, pack_rows)
        return pltpu.make_async_copy(
            hnp_ref.at[pl.ds(src, pack_rows), :],
            xs_ref.at[pl.ds(slot_ref[0, tok * TOP_K + j] * pack_rows, pack_rows), :], sem)

    def start(tok, _):
        for j in range(TOP_K):
            copy(tok, j).start()
        return 0

    def wait_all(n_tok):
        for _ in range(TOP_K):
            pltpu.make_async_copy(
                hnp_ref.at[pl.ds(0, n_tok * pack_rows), :],
                xs_ref.at[pl.ds(0, n_tok * pack_rows), :], sem).wait()

    @pl.when(i % blocks_per_batch == 0)
    def _():
        lax.fori_loop(front, tm, start, 0)
        wait_all(tm - front)

    @pl.when(i % blocks_per_batch != 0)
    def _():
        lax.fori_loop(0, tm, start, 0)
        wait_all(tm)


def _dispatch(slots, hnp, pstart, padded, used_rows, n_slots, pack_rows, front, lp):
    rows = slots.shape[0]
    tm = MIX_ROWS
    slots3 = slots.reshape(rows // tm, 1, tm * TOP_K)
    grid_spec = pltpu.PrefetchScalarGridSpec(
        num_scalar_prefetch=3,
        grid=(rows // tm,),
        in_specs=[
            pl.BlockSpec((None, 1, tm * TOP_K), lambda i, *_: (i, 0, 0), memory_space=pltpu.SMEM),
            pl.BlockSpec((tm * pack_rows, LANES), lambda i, *_: (i, 0)),
        ],
        out_specs=pl.BlockSpec(memory_space=pl.ANY),
        scratch_shapes=[pltpu.VMEM((EXP_SUB * pack_rows, LANES), U32),
                        pltpu.SemaphoreType.DMA(()), pltpu.SemaphoreType.DMA(())],
    )
    return pl.pallas_call(
        functools.partial(_dispatch_kernel, pack_rows=pack_rows, front=front,
                          blocks_per_batch=lp // tm, n_slots=n_slots),
        grid_spec=grid_spec,
        out_shape=jax.ShapeDtypeStruct((n_slots * pack_rows, LANES), U32),
        compiler_params=_cparams(1),
        name="dispatch",
    )(pstart, padded, used_rows, slots3, hnp)


def _expert_kernel(ue_ref, ur_ref, un_ref, used_ref, xs_ref, wg_ref, wu_ref, bg_ref, bu_ref, wd_ref,
                   bd_ref, ys_ref, xbuf, xb, acc, wg_b, wu_b, wd_b, ystage, sem_in, sem_out,
                   *, pack_rows, n_slots):
    del ue_ref
    ts = EXP_SUB
    u = pl.program_id(0)
    f = pl.program_id(1)
    n_f = pl.num_programs(1)
    nsub = un_ref[u]
    row0 = ur_ref[u]
    d = xb.shape[1]
    half = d // 2
    tf = wg_b.shape[1]

    @pl.when((u == 0) & (f == 0))
    def _():
        ystage[...] = jnp.zeros_like(ystage)
        used = used_ref[0]

        def tail_copy(i):
            return pltpu.make_async_copy(
                ystage.at[0], ys_ref.at[pl.ds((used + i * ts) * pack_rows, ts * pack_rows), :],
                sem_out.at[0])

        n_tail = (n_slots - used) // ts
        lax.fori_loop(0, n_tail, lambda i, c: (tail_copy(i).start(), c)[1], 0)
        lax.fori_loop(0, n_tail, lambda i, c: (tail_copy(i).wait(), c)[1], 0)

    def in_copy(s):
        return pltpu.make_async_copy(
            xs_ref.at[pl.ds((row0 + s * ts) * pack_rows, ts * pack_rows), :],
            xbuf.at[s % 2], sem_in.at[s % 2])

    @pl.when((f == 0) & (nsub > 0))
    def _():
        in_copy(0).start()

        def load(s, _):
            in_copy(s).wait()

            @pl.when(s + 1 < nsub)
            def _():
                in_copy(s + 1).start()
            words = jnp.concatenate(
                [xbuf[s % 2, pl.ds(c, ts, stride=pack_rows), :] for c in range(pack_rows)], axis=1)
            r = pl.multiple_of(s * ts, ts)
            xb[pl.ds(r, ts), :half] = pltpu.bitcast(words << 16, F32).astype(BF16)
            xb[pl.ds(r, ts), half:] = pltpu.bitcast(words & jnp.uint32(HIGH16), F32).astype(BF16)
            acc[pl.ds(r, ts), :] = jnp.broadcast_to(bd_ref[...], (ts, d))
            return 0
        lax.fori_loop(0, nsub, load, 0)

    @pl.when(nsub > 0)
    def _():
        wg_b[...] = wg_ref[...].astype(BF16)
        wu_b[...] = wu_ref[...].astype(BF16)
        wd_b[...] = wd_ref[...].astype(BF16)

        def sub_block(r):
            x = xb[pl.ds(r, ts), :]
            acts = []
            for c0 in range(0, tf, ROW_ALIGN):
                g = _dot(x, wg_b[:, c0:c0 + ROW_ALIGN]) + bg_ref[:, c0:c0 + ROW_ALIGN]
                up = _dot(x, wu_b[:, c0:c0 + ROW_ALIGN]) + bu_ref[:, c0:c0 + ROW_ALIGN]
                g = jnp.minimum(g, SWIGLU_LIMIT)
                up = jnp.clip(up, -SWIGLU_LIMIT, SWIGLU_LIMIT)
                acts.append(((up + 1.0) * g * _sigmoid(SWIGLU_ALPHA * g)).astype(BF16))
            acc[pl.ds(r, ts), :] += _dot(jnp.concatenate(acts, axis=1), wd_b[...])

        group = 3

        def trip(p, _):
            r = pl.multiple_of(p * (group * ts), ts)
            for k in range(group):
                sub_block(r + k * ts)
            return 0
        lax.fori_loop(0, nsub // group, trip, 0)
        for k in range(1, group):
            @pl.when(nsub % group >= k)
            def _():
                sub_block(pl.multiple_of((nsub // group * group + k - 1) * ts, ts))

    def out_copy(s):
        return pltpu.make_async_copy(
            ystage.at[s % 2], ys_ref.at[pl.ds((row0 + s * ts) * pack_rows, ts * pack_rows), :],
            sem_out.at[s % 2])

    @pl.when((f == n_f - 1) & (nsub > 0))
    def _():
        def store(s, _):
            @pl.when(s >= 2)
            def _():
                out_copy(s - 2).wait()
            y = acc[pl.ds(pl.multiple_of(s * ts, ts), ts), :]
            lo = pltpu.bitcast(y[:, :half].astype(BF16).astype(F32), U32) >> 16
            hi = pltpu.bitcast(y[:, half:].astype(BF16).astype(F32), U32) & jnp.uint32(HIGH16)
            word = lo | hi
            for c in range(pack_rows):
                ystage[s % 2, pl.ds(c, ts, stride=pack_rows), :] = word[:, c * LANES:(c + 1) * LANES]
            out_copy(s).start()
            return 0
        lax.fori_loop(0, nsub, store, 0)

        @pl.when(nsub >= 2)
        def _():
            out_copy(nsub - 2).wait()
        out_copy(nsub - 1).wait()


def _experts(unit_e, unit_row0, unit_nsub, used_rows, xs, w_gate_up, b_gate_up, w_down, b_down,
             n_slots, pack_rows):
    n_exp, d, two_de = w_gate_up.shape
    de = two_de // 2
    tf = EXP_FTILE
    n_f = de // tf
    n_units = unit_e.shape[0]
    last_f = n_f - 1

    def fidx(u, f, un):
        return jnp.where(un[u] > 0, f, last_f)

    grid_spec = pltpu.PrefetchScalarGridSpec(
        num_scalar_prefetch=4,
        grid=(n_units, n_f),
        in_specs=[
            pl.BlockSpec(memory_space=pl.ANY),
            pl.BlockSpec((None, d, tf), lambda u, f, ue, ur, un, us: (ue[u], 0, fidx(u, f, un))),
            pl.BlockSpec((None, d, tf),
                         lambda u, f, ue, ur, un, us: (ue[u], 0, n_f + fidx(u, f, un))),
            pl.BlockSpec((None, 1, tf), lambda u, f, ue, ur, un, us: (ue[u], 0, fidx(u, f, un))),
            pl.BlockSpec((None, 1, tf),
                         lambda u, f, ue, ur, un, us: (ue[u], 0, n_f + fidx(u, f, un))),
            pl.BlockSpec((None, tf, d), lambda u, f, ue, ur, un, us: (ue[u], fidx(u, f, un), 0)),
            pl.BlockSpec((None, 1, d), lambda u, f, ue, ur, un, us: (ue[u], 0, 0)),
        ],
        out_specs=pl.BlockSpec(memory_space=pl.ANY),
        scratch_shapes=[
            pltpu.VMEM((2, EXP_SUB * pack_rows, LANES), U32),
            pltpu.VMEM((EXP_UNIT, d), BF16),
            pltpu.VMEM((EXP_UNIT, d), F32),
            pltpu.VMEM((d, tf), BF16),
            pltpu.VMEM((d, tf), BF16),
            pltpu.VMEM((tf, d), BF16),
            pltpu.VMEM((2, EXP_SUB * pack_rows, LANES), U32),
            pltpu.SemaphoreType.DMA((2,)),
            pltpu.SemaphoreType.DMA((2,)),
        ],
    )
    return pl.pallas_call(
        functools.partial(_expert_kernel, pack_rows=pack_rows, n_slots=n_slots),
        grid_spec=grid_spec,
        out_shape=jax.ShapeDtypeStruct((n_slots * pack_rows, LANES), U32),
        compiler_params=_cparams(2),
        name="experts",
    )(unit_e, unit_row0, unit_nsub, used_rows, xs, w_gate_up, w_gate_up,
      b_gate_up.reshape(n_exp, 1, two_de), b_gate_up.reshape(n_exp, 1, two_de),
      w_down, b_down.reshape(n_exp, 1, d))


def _combine_kernel(slot_ref, gate_ref, h1_ref, ys_ref, o_ref, ybuf, sem, *, pack_rows):
    tc = COMB_ROWS
    half = h1_ref.shape[1] // 2

    def copy(tok, j):
        s = slot_ref[0, tok * TOP_K + j]
        dst = pl.multiple_of(tok * pack_rows, pack_rows)
        return pltpu.make_async_copy(
            ys_ref.at[pl.ds(s * pack_rows, pack_rows), :],
            ybuf.at[j, pl.ds(dst, pack_rows), :], sem)

    def start(tok, _):
        for j in range(TOP_K):
            copy(tok, j).start()
        return 0

    lax.fori_loop(0, tc, start, 0)
    for j in range(TOP_K):
        pltpu.make_async_copy(ys_ref.at[pl.ds(0, tc * pack_rows), :], ybuf.at[j], sem).wait()

    gates = gate_ref[...]
    for c in range(pack_rows):
        lo = h1_ref[:, c * LANES:(c + 1) * LANES]
        hi = h1_ref[:, half + c * LANES:half + (c + 1) * LANES]
        for j in range(TOP_K):
            word = ybuf[j, pl.ds(c, tc, stride=pack_rows), :]
            gate = gates[:, j:j + 1]
            lo = lo + gate * pltpu.bitcast(word << 16, F32)
            hi = hi + gate * pltpu.bitcast(word & jnp.uint32(HIGH16), F32)
        o_ref[:, c * LANES:(c + 1) * LANES] = lo
        o_ref[:, half + c * LANES:half + (c + 1) * LANES] = hi


def _combine(slots, gates, h1, ys, bsz, seq, lp, d):
    tc = COMB_ROWS
    rows = slots.shape[0]
    pack_rows = d // 2 // LANES
    slots3 = slots.reshape(rows // tc, 1, tc * TOP_K)
    nb = lp // tc
    first = (lp - seq) // tc
    return pl.pallas_call(
        functools.partial(_combine_kernel, pack_rows=pack_rows),
        grid=(bsz, seq // tc),
        in_specs=[
            pl.BlockSpec((None, 1, tc * TOP_K), lambda b, i: (b * nb + first + i, 0, 0),
                         memory_space=pltpu.SMEM),
            pl.BlockSpec((tc, LANES), lambda b, i: (b * nb + first + i, 0)),
            pl.BlockSpec((tc, d), lambda b, i: (b * nb + first + i, 0)),
            pl.BlockSpec(memory_space=pl.ANY),
        ],
        out_specs=pl.BlockSpec((None, tc, d), lambda b, i: (b, i, 0)),
        out_shape=jax.ShapeDtypeStruct((bsz, seq, d), F32),
        scratch_shapes=[pltpu.VMEM((TOP_K, tc * pack_rows, LANES), U32), pltpu.SemaphoreType.DMA(())],
        compiler_params=_cparams(2),
        name="combine",
    )(slots3, gates, h1, ys)


def _pick(n, candidates):
    for c in candidates:
        if n % c == 0:
            return c
    raise ValueError(f"no block size in {candidates} divides {n}")


def _plan(info, cnt, n_exp, rows, lp, front, n_assign):
    counts = cnt[0, :n_exp].astype(I32)
    padded = (counts + EXP_SUB - 1) // EXP_SUB * EXP_SUB
    pstart = jnp.cumsum(padded) - padded
    eid = info[:, :TOP_K]
    rank = info[:, TOP_K:2 * TOP_K]
    onehot = eid[:, :, None] == jnp.arange(n_exp, dtype=I32)[None, None, :]
    slot = rank + jnp.sum(jnp.where(onehot, pstart[None, None, :], 0), axis=-1)
    row_valid = (jnp.arange(rows, dtype=I32) % lp) >= front
    slots = jnp.where(row_valid[:, None], slot, -1).astype(I32)

    n_slots = (n_assign + n_exp * (EXP_SUB - 1)) // EXP_SUB * EXP_SUB
    units_per_e = (padded + EXP_UNIT - 1) // EXP_UNIT
    cum_units = jnp.cumsum(units_per_e)
    n_units = n_slots // EXP_UNIT + n_exp
    uidx = jnp.arange(n_units, dtype=I32)
    ue = jnp.sum(cum_units[None, :] <= uidx[:, None], axis=1).astype(I32)
    live = ue < n_exp
    last_e = jnp.max(jnp.where(counts > 0, jnp.arange(n_exp, dtype=I32), 0))
    ue = jnp.where(live, ue, last_e)
    k_in_e = uidx - (cum_units - units_per_e)[ue]
    nsub_e = (padded // EXP_SUB)[ue]
    units_e = jnp.maximum(units_per_e[ue], 1)
    base = nsub_e // units_e
    extra = nsub_e - base * units_e
    first_sub = k_in_e * base + jnp.minimum(k_in_e, extra)
    unit_row0 = jnp.where(live, pstart[ue] + first_sub * EXP_SUB, 0).astype(I32)
    unit_nsub = jnp.where(live, base + (k_in_e < extra), 0).astype(I32)
    used_rows = jnp.sum(padded).astype(I32)[None]
    return slots, (pstart.astype(I32), padded.astype(I32)), (ue, unit_row0, unit_nsub, used_rows), n_slots


def kernel(x, meta_tokens, mix_norm_w, w_in, conv_w, a_log, dt_bias, gdn_norm_w, sb_q_norm_w,
           sb_k_norm_w, sb_out_norm_w, w_out, ffn_norm_w, router_w, router_b, w_gate_up, b_gate_up,
           w_down, b_down):
    bsz, seq, d = x.shape
    n_meta = meta_tokens.shape[0]
    depth = mix_norm_w.shape[0]
    heads = a_log.shape[1]
    hd = gdn_norm_w.shape[1]
    gd = heads * hd
    sbd = (w_in.shape[2] - 4 * gd - 2 * heads) // 3
    sb_heads = sbd // hd
    n_exp = router_w.shape[2]
    assert seq % ROW_ALIGN == 0 and d % (2 * 8 * LANES) == 0 and hd == LANES
    assert 2 * heads <= LANES and n_exp <= LANES and sb_heads == heads
    assert depth == 1, "a second layer would need the meta rows carried through the combine stage"
    front = (-n_meta) % ROW_ALIGN
    lp = front + n_meta + seq
    rows = bsz * lp
    pack_rows = d // 2 // LANES

    h = jnp.concatenate([
        jnp.zeros((bsz, front, d), x.dtype),
        jnp.broadcast_to(meta_tokens.astype(x.dtype)[None], (bsz, n_meta, d)),
        x], axis=1).reshape(rows, d)

    wl = w_in[0]
    n_ba = 4 * gd
    w_main = jnp.concatenate([wl[:, :n_ba], wl[:, n_ba + 2 * heads:]], axis=1).astype(BF16)
    w_ba = jnp.pad(wl[:, n_ba:n_ba + 2 * heads], ((0, 0), (0, LANES - 2 * heads))).astype(BF16)
    tm = _pick(rows, (1024, 512, 256))
    tn = _pick(w_main.shape[1], (1024, 512, 256, 128))
    proj, ba = _in_proj(h, mix_norm_w[0][None], w_main, w_ba, tm, tn)
    proj3 = proj.reshape(bsz, lp, -1)
    ba3 = ba.reshape(bsz, lp, LANES)

    gparams = jnp.zeros((2, LANES), F32)
    gparams = gparams.at[0, heads:2 * heads].set(a_log[0].astype(F32))
    gparams = gparams.at[1, heads:2 * heads].set(dt_bias[0].astype(F32))
    qkv3, bg3 = _gdn_prep(proj3, ba3, conv_w[0].astype(F32), gparams, heads, hd)
    o_gdn = _gdn(qkv3, proj3, bg3, gdn_norm_w[0][None], heads, hd)
    o_sb = _sb(proj3, sb_q_norm_w[0][None], sb_k_norm_w[0][None], sb_out_norm_w[0][None],
               sb_heads, hd, 4 * gd, front)

    rw = jnp.pad(router_w[0].astype(F32), ((0, 0), (0, LANES - n_exp)))
    rb = jnp.pad(router_b[0].astype(F32), (0, LANES - n_exp), constant_values=-1e30)[None]
    h1, hnp, info, gates, cnt = _mix(
        o_gdn.reshape(rows, gd), o_sb.reshape(rows, sbd), h, w_out[0].astype(BF16),
        ffn_norm_w[0][None], rw, rb, front, lp)

    slots, (pstart, padded), units, n_slots = _plan(
        info, cnt, n_exp, rows, lp, front, bsz * (n_meta + seq) * TOP_K)
    xs = _dispatch(slots, hnp, pstart, padded, units[3], n_slots, pack_rows, front, lp)
    ys = _experts(*units, xs, w_gate_up[0], b_gate_up[0], w_down[0], b_down[0], n_slots, pack_rows)
    return _combine(slots, gates, h1, ys, bsz, seq, lp, d)
```

```python
import functools

import jax
import jax.numpy as jnp
from jax import lax
from jax.experimental import pallas as pl
from jax.experimental.pallas import tpu as pltpu

F32 = jnp.float32
BF16 = jnp.bfloat16
U32 = jnp.uint32
I32 = jnp.int32

NORM_EPS = 1e-6
TOP_K = 4
SWIGLU_LIMIT = 7.0
SWIGLU_ALPHA = 1.702
LANES = 128
ROW_ALIGN = 256
GDN_PREP_ROWS = 256
GDN_CHUNK = 64
SB_BLOCK = 256
MIX_ROWS = 512
DISPATCH_ROWS = 256
EXP_SUB = 256
EXP_GROUP = 3
EXP_UNIT = 9 * EXP_SUB
EXP_FTILE = 256
COMB_ROWS = 256
VMEM_LIMIT = 56 * 1024 * 1024
EXPERT_VMEM_LIMIT = 60 * 1024 * 1024
HIGH16 = 0xFFFF0000
SIGN_BIT = 0x80000000
LOG2E = 1.4426950408889634


def _cparams(n_grid, vmem=VMEM_LIMIT):
    return pltpu.CompilerParams(dimension_semantics=("arbitrary",) * n_grid, vmem_limit_bytes=vmem)


def _dot(a, b):
    return jnp.dot(a, b, preferred_element_type=F32)


def _dot_nt(a, b):
    return lax.dot_general(a, b, (((1,), (1,)), ((), ())), preferred_element_type=F32)


def _dot_tn(a, b):
    return lax.dot_general(a, b, (((0,), (0,)), ((), ())), preferred_element_type=F32)


def _split3(x):
    hi = x.astype(BF16)
    r1 = x - hi.astype(F32)
    mid = r1.astype(BF16)
    lo = (r1 - mid.astype(F32)).astype(BF16)
    return hi, mid, lo


def _dot3(a, b):
    ah = a.astype(BF16)
    al = (a - ah.astype(F32)).astype(BF16)
    bh = b.astype(BF16)
    bl = (b - bh.astype(F32)).astype(BF16)
    return _dot(ah, bh) + (_dot(ah, bl) + _dot(al, bh))


def _sigmoid(x):
    return 1.0 / (1.0 + jnp.exp(-x))


def _softplus(x):
    return jnp.maximum(x, 0.0) + jnp.log1p(jnp.exp(-jnp.abs(x)))


def _inproj_kernel(h_ref, nw_ref, w_ref, wba_ref, o_ref, ba_ref, xn_ref):
    @pl.when(pl.program_id(1) == 0)
    def _():
        x = h_ref[...]
        xn = x * lax.rsqrt(jnp.mean(x * x, axis=-1, keepdims=True) + NORM_EPS) * nw_ref[...]
        xn = xn.astype(BF16)
        xn_ref[...] = xn
        ba_ref[...] = _dot(xn, wba_ref[...])

    o_ref[...] = _dot(xn_ref[...], w_ref[...]).astype(o_ref.dtype)


def _in_proj(h0, norm_w, w_main, w_ba, tm, tn):
    rows, d = h0.shape
    n_main = w_main.shape[1]
    return pl.pallas_call(
        _inproj_kernel,
        grid=(rows // tm, n_main // tn),
        in_specs=[
            pl.BlockSpec((tm, d), lambda i, n: (i, 0)),
            pl.BlockSpec((1, d), lambda i, n: (0, 0)),
            pl.BlockSpec((d, tn), lambda i, n: (0, n)),
            pl.BlockSpec((d, LANES), lambda i, n: (0, 0)),
        ],
        out_specs=[
            pl.BlockSpec((tm, tn), lambda i, n: (i, n)),
            pl.BlockSpec((tm, LANES), lambda i, n: (i, 0)),
        ],
        out_shape=[
            jax.ShapeDtypeStruct((rows, n_main), BF16),
            jax.ShapeDtypeStruct((rows, LANES), F32),
        ],
        scratch_shapes=[pltpu.VMEM((tm, d), BF16)],
        compiler_params=_cparams(2),
        name="in_proj",
    )(h0, norm_w, w_main, w_ba)


def _gdn_prep_kernel(x_ref, prev_ref, ba_ref, cw_ref, gp_ref, o_ref, bg_ref, *, heads, hd):
    tp = x_ref.shape[0]
    gd = heads * hd
    taps = cw_ref.shape[0]
    x = x_ref[...].astype(F32)
    prev = jnp.where(pl.program_id(1) > 0, prev_ref[8:16, :].astype(F32), 0.0)
    xs = jnp.concatenate([prev, x], axis=0)
    y = cw_ref[taps - 1:taps, :] * x
    for s in range(1, taps):
        y = y + cw_ref[taps - 1 - s:taps - s, :] * pltpu.roll(xs, s, axis=0)[8:8 + tp, :]
    y = y * _sigmoid(y)
    for h in range(heads):
        q = y[:, h * hd:(h + 1) * hd]
        k = y[:, gd + h * hd:gd + (h + 1) * hd]
        q = q * (lax.rsqrt(jnp.sum(q * q, axis=-1, keepdims=True) + NORM_EPS) * (hd ** -0.5))
        k = k * lax.rsqrt(jnp.sum(k * k, axis=-1, keepdims=True) + NORM_EPS)
        o_ref[:, h * hd:(h + 1) * hd] = q.astype(o_ref.dtype)
        o_ref[:, gd + h * hd:gd + (h + 1) * hd] = k.astype(o_ref.dtype)
    o_ref[:, 2 * gd:] = y[:, 2 * gd:].astype(o_ref.dtype)
    ba = ba_ref[...]
    lane = lax.broadcasted_iota(I32, ba.shape, 1)
    decay = -jnp.exp(gp_ref[0:1, :]) * _softplus(ba + gp_ref[1:2, :])
    bg_ref[...] = jnp.where(lane < heads, _sigmoid(ba), decay)


def _gdn_prep(proj3, ba3, conv_w, gparams, heads, hd):
    bsz, lp, _ = proj3.shape
    gd = heads * hd
    tp = GDN_PREP_ROWS
    return pl.pallas_call(
        functools.partial(_gdn_prep_kernel, heads=heads, hd=hd),
        grid=(bsz, lp // tp),
        in_specs=[
            pl.BlockSpec((None, tp, 3 * gd), lambda b, i: (b, i, 0)),
            pl.BlockSpec((None, 16, 3 * gd), lambda b, i: (b, jnp.maximum(i * (tp // 16) - 1, 0), 0)),
            pl.BlockSpec((None, tp, LANES), lambda b, i: (b, i, 0)),
            pl.BlockSpec(conv_w.shape, lambda b, i: (0, 0)),
            pl.BlockSpec((2, LANES), lambda b, i: (0, 0)),
        ],
        out_specs=[
            pl.BlockSpec((None, tp, 3 * gd), lambda b, i: (b, i, 0)),
            pl.BlockSpec((None, tp, LANES), lambda b, i: (b, i, 0)),
        ],
        out_shape=[
            jax.ShapeDtypeStruct((bsz, lp, 3 * gd), BF16),
            jax.ShapeDtypeStruct((bsz, lp, LANES), F32),
        ],
        compiler_params=_cparams(2),
        name="gdn_prep",
    )(proj3, proj3, ba3, conv_w, gparams)


def _gdn_kernel(qkv_ref, z_ref, bg_ref, nw_ref, o_ref, state, *, heads, hd):
    c_rows = GDN_CHUNK
    gd = heads * hd

    @pl.when(pl.program_id(1) == 0)
    def _():
        state[...] = jnp.zeros_like(state)

    bg = bg_ref[...]
    r_i = lax.broadcasted_iota(I32, (c_rows, c_rows), 0)
    c_i = lax.broadcasted_iota(I32, (c_rows, c_rows), 1)
    incl = r_i >= c_i
    strict = r_i > c_i
    tri = incl.astype(BF16)
    g_parts = _split3(bg)
    gcum = _dot(tri, g_parts[0]) + _dot(tri, g_parts[1]) + _dot(tri, g_parts[2])
    n_sel = -(-heads // 8) * 8
    pick = (lax.broadcasted_iota(I32, (n_sel, LANES), 1)
            == lax.broadcasted_iota(I32, (n_sel, LANES), 0) + heads).astype(BF16)
    c_parts = _split3(gcum)
    gcum_rows = _dot_nt(pick, c_parts[0]) + _dot_nt(pick, c_parts[1]) + _dot_nt(pick, c_parts[2])

    hs = range(heads)
    q16 = [qkv_ref[:, h * hd:(h + 1) * hd] for h in hs]
    k16 = [qkv_ref[:, gd + h * hd:gd + (h + 1) * hd] for h in hs]
    k = [k16[h].astype(F32) for h in hs]
    gc = [gcum[:, heads + h:heads + h + 1] for h in hs]
    decay = [jnp.exp(jnp.minimum(gc[h] - gcum_rows[h:h + 1, :], 0.0)) for h in hs]
    kb = [k[h] * bg[:, h:h + 1] for h in hs]
    p = [jnp.where(strict, _dot_nt(kb[h].astype(BF16), k16[h]) * decay[h], 0.0) for h in hs]
    p = [(-p[h]).astype(BF16) for h in hs]
    sol = [jnp.concatenate(
        [qkv_ref[:, 2 * gd + h * hd:2 * gd + (h + 1) * hd].astype(F32) * bg[:, h:h + 1],
         kb[h] * jnp.exp(gc[h])], axis=1) for h in hs]
    n_fac = c_rows.bit_length() - 1
    for i in range(n_fac):
        sol = [sol[h] + _dot(p[h], sol[h].astype(BF16)) for h in hs]
        if i + 1 < n_fac:
            p = [_dot(p[h], p[h]).astype(BF16) for h in hs]
    attn = [jnp.where(incl, _dot_nt(q16[h], k16[h]) * decay[h], 0.0).astype(BF16) for h in hs]
    s_old = [state[h] for h in hs]
    s_b = [s_old[h].astype(BF16) for h in hs]
    v_new = [(sol[h][:, :hd] - _dot(sol[h][:, hd:].astype(BF16), s_b[h])).astype(BF16) for h in hs]
    o = [_dot((q16[h].astype(F32) * jnp.exp(gc[h])).astype(BF16), s_b[h]) + _dot(attn[h], v_new[h])
         for h in hs]
    for h in hs:
        g_last = gc[h][c_rows - 1:c_rows, :]
        k_dec = (k[h] * jnp.exp(g_last - gc[h])).astype(BF16)
        state[h] = s_old[h] * jnp.exp(g_last) + _dot_tn(k_dec, v_new[h])
    for h in hs:
        zf = z_ref[:, h * hd:(h + 1) * hd].astype(F32)
        o_n = o[h] * lax.rsqrt(jnp.mean(o[h] * o[h], axis=-1, keepdims=True) + NORM_EPS) * nw_ref[...]
        o_ref[:, h * hd:(h + 1) * hd] = (o_n * (zf * _sigmoid(zf))).astype(o_ref.dtype)


def _gdn(qkv3, proj3, bg3, norm_w, heads, hd):
    bsz, lp, _ = qkv3.shape
    gd = heads * hd
    c = GDN_CHUNK
    return pl.pallas_call(
        functools.partial(_gdn_kernel, heads=heads, hd=hd),
        grid=(bsz, lp // c),
        in_specs=[
            pl.BlockSpec((None, c, 3 * gd), lambda b, i: (b, i, 0)),
            pl.BlockSpec((None, c, gd), lambda b, i: (b, i, 3)),
            pl.BlockSpec((None, c, LANES), lambda b, i: (b, i, 0)),
            pl.BlockSpec((1, hd), lambda b, i: (0, 0)),
        ],
        out_specs=pl.BlockSpec((None, c, gd), lambda b, i: (b, i, 0)),
        out_shape=jax.ShapeDtypeStruct((bsz, lp, gd), BF16),
        scratch_shapes=[pltpu.VMEM((heads, hd, hd), F32)],
        compiler_params=_cparams(2),
        name="gdn",
    )(qkv3, proj3, bg3, norm_w)


def _sb_kernel(q_ref, k_ref, v_ref, qw_ref, kw_ref, ow_ref, tri_ref, o_ref, kn_ref, acc_ref,
               drop_ref, *, front, hd):
    t = SB_BLOCK
    qi = pl.program_id(2)
    n_blocks = k_ref.shape[0] // t

    @pl.when(qi == 0)
    def _():
        def norm_keys(i, _):
            kb = k_ref[pl.ds(i * t, t), :].astype(F32)
            kn = kb * lax.rsqrt(jnp.mean(kb * kb, axis=-1, keepdims=True) + NORM_EPS) * kw_ref[...]
            kn_ref[pl.ds(i * t, t), :] = kn.astype(BF16)
            return 0
        lax.fori_loop(0, n_blocks, norm_keys, 0)

    q = q_ref[...].astype(F32)
    qn = q * lax.rsqrt(jnp.mean(q * q, axis=-1, keepdims=True) + NORM_EPS) * qw_ref[...]
    qn = (qn * (hd ** -0.5 * LOG2E)).astype(BF16)

    acc_ref[...] = jnp.zeros_like(acc_ref)
    drop_ref[...] = jnp.zeros_like(drop_ref)

    def mask_of(kind):
        r_i = lax.broadcasted_iota(I32, (t, t), 0)
        c_i = lax.broadcasted_iota(I32, (t, t), 1)
        if kind == "causal":
            return c_i < r_i
        if kind == "front":
            return c_i >= front
        return (c_i < r_i) & (c_i >= front)

    def scores(kj, kind):
        start = pl.multiple_of(kj * t, t)
        s = _dot_nt(qn, kn_ref[pl.ds(start, t), :])
        neg_abs = pltpu.bitcast(pltpu.bitcast(s, U32) | jnp.uint32(SIGN_BIT), F32)
        drop = jnp.maximum(s, 0.0) + jnp.log2(1.0 + jnp.exp2(neg_abs))
        visible = None
        if kind is not None:
            visible = mask_of(kind)
            drop = jnp.where(visible, drop, 0.0)
        hi = pltpu.bitcast(pltpu.bitcast(drop, U32) & jnp.uint32(HIGH16), F32)
        lo = drop - hi
        tail = _dot(jnp.concatenate([hi.astype(BF16), lo.astype(BF16)], axis=1), tri_ref[...])
        return s, tail, visible

    def run(tiles):
        parts = [scores(kj, kind) for kj, kind in tiles]
        dropped = drop_ref[...]
        acc = acc_ref[...]
        for (kj, kind), (s, tail, visible) in zip(tiles, parts):
            weight = jnp.exp2(s - tail - jnp.concatenate([dropped] * (t // LANES), axis=1))
            if kind is not None:
                weight = jnp.where(visible, weight, 0.0)
            start = pl.multiple_of(kj * t, t)
            acc = acc + _dot(weight.astype(BF16), v_ref[pl.ds(start, t), :])
            dropped = dropped + jnp.broadcast_to(tail[:, 0:1], dropped.shape)
        acc_ref[...] = acc
        drop_ref[...] = dropped

    @pl.when(qi == 0)
    def _():
        run([(0, "causal_front")])

    @pl.when(qi == 1)
    def _():
        run([(1, "causal"), (0, "front")])

    @pl.when(qi >= 2)
    def _():
        run([(qi, "causal"), (qi - 1, None)])
        rest = qi - 2

        def pair(p, _):
            run([(qi - 2 - 2 * p, None), (qi - 3 - 2 * p, None)])
            return 0
        lax.fori_loop(0, rest // 2, pair, 0)

        @pl.when(rest % 2 == 1)
        def _():
            run([(1, None), (0, "front")])

        @pl.when(rest % 2 == 0)
        def _():
            run([(0, "front")])

    o = acc_ref[...]
    o = o * lax.rsqrt(jnp.mean(o * o, axis=-1, keepdims=True) + NORM_EPS) * ow_ref[...]
    o_ref[...] = o.astype(o_ref.dtype)


def _sb(proj3, qw, kw, ow, heads, hd, col0, front):
    bsz, lp, _ = proj3.shape
    t = SB_BLOCK
    cb = col0 // hd
    tri = (jnp.arange(t)[:, None] >= jnp.arange(t)[None, :])
    tri = jnp.concatenate([tri, tri], axis=0).astype(BF16)
    return pl.pallas_call(
        functools.partial(_sb_kernel, front=front, hd=hd),
        grid=(bsz, heads, lp // t),
        in_specs=[
            pl.BlockSpec((None, t, hd), lambda b, h, i: (b, i, cb + h)),
            pl.BlockSpec((None, lp, hd), lambda b, h, i: (b, 0, cb + heads + h)),
            pl.BlockSpec((None, lp, hd), lambda b, h, i: (b, 0, cb + 2 * heads + h)),
            pl.BlockSpec((1, hd), lambda b, h, i: (0, 0)),
            pl.BlockSpec((1, hd), lambda b, h, i: (0, 0)),
            pl.BlockSpec((1, hd), lambda b, h, i: (0, 0)),
            pl.BlockSpec(tri.shape, lambda b, h, i: (0, 0)),
        ],
        out_specs=pl.BlockSpec((None, t, hd), lambda b, h, i: (b, i, h)),
        out_shape=jax.ShapeDtypeStruct((bsz, lp, heads * hd), BF16),
        scratch_shapes=[pltpu.VMEM((lp, hd), BF16), pltpu.VMEM((t, hd), F32),
                        pltpu.VMEM((t, LANES), F32)],
        compiler_params=_cparams(3),
        name="sb",
    )(proj3, proj3, proj3, qw, kw, ow, tri)


def _mix_kernel(og_ref, os_ref, h_ref, wo_ref, nw_ref, rw_ref, rb_ref,
                h1_ref, hnp_ref, info_ref, gate_ref, cnt_ref, cnt_acc,
                *, front, lp, bsz, pack_rows):
    tm = MIX_ROWS
    i = pl.program_id(0)

    @pl.when(i == 0)
    def _():
        cnt_acc[...] = jnp.zeros_like(cnt_acc)

    gd = og_ref.shape[1]
    h1 = h_ref[...] + _dot(og_ref[...], wo_ref[0:gd, :]) + _dot(os_ref[...], wo_ref[gd:, :])
    h1_ref[...] = h1
    hn = h1 * lax.rsqrt(jnp.mean(h1 * h1, axis=-1, keepdims=True) + NORM_EPS) * nw_ref[...]

    half = hn.shape[1] // 2
    lo = pltpu.bitcast(hn[:, :half].astype(BF16).astype(F32), U32) >> 16
    hi = pltpu.bitcast(hn[:, half:].astype(BF16).astype(F32), U32) & jnp.uint32(HIGH16)
    word = lo | hi
    for s in range(pack_rows):
        hnp_ref[pl.ds(s, tm, stride=pack_rows), :] = word[:, s * LANES:(s + 1) * LANES]

    logits = _dot3(hn, rw_ref[...]) + rb_ref[...]
    lane = lax.broadcasted_iota(I32, (tm, LANES), 1)
    row = lax.broadcasted_iota(I32, (tm, 1), 0)
    pos = i * tm + row
    valid = pos < 0
    for b in range(bsz):
        valid = valid | ((pos >= b * lp + front) & (pos < (b + 1) * lp))
    work = logits
    tops, idxs, hots = [], [], []
    for _ in range(TOP_K):
        m = jnp.max(work, axis=1, keepdims=True)
        idx = jnp.min(jnp.where(work == m, lane, LANES), axis=1, keepdims=True)
        hot = lane == idx
        tops.append(m)
        idxs.append(idx)
        hots.append(hot)
        work = jnp.where(hot, -jnp.inf, work)
    exps = [jnp.exp(m - tops[0]) for m in tops]
    denom = exps[0] + exps[1] + exps[2] + exps[3]
    sel = jnp.zeros((tm, LANES), F32)
    for hot in hots:
        sel = sel + hot.astype(F32)
    sel = jnp.where(valid, sel, 0.0)

    r_i = lax.broadcasted_iota(I32, (tm, tm), 0)
    c_i = lax.broadcasted_iota(I32, (tm, tm), 1)
    before = (r_i > c_i).astype(BF16)
    rank = cnt_acc[0:1, :] + _dot(before, sel.astype(BF16))
    info = jnp.zeros((tm, LANES), I32)
    gates = jnp.zeros((tm, LANES), F32)
    for j in range(TOP_K):
        rank_j = jnp.sum(jnp.where(hots[j], rank, 0.0), axis=1, keepdims=True).astype(I32)
        info = jnp.where(lane == j, idxs[j], info)
        info = jnp.where(lane == TOP_K + j, rank_j, info)
        gates = jnp.where(lane == j, exps[j] / denom, gates)
    info_ref[...] = info
    gate_ref[...] = gates
    cnt_acc[...] = cnt_acc[...] + jnp.sum(sel, axis=0, keepdims=True)
    cnt_ref[...] = cnt_acc[...]


def _mix(o_gdn, o_sb, h0, w_out, norm_w, router_w, router_b, front, lp):
    rows, d = h0.shape
    gd = o_gdn.shape[1]
    tm = MIX_ROWS
    pack_rows = d // 2 // LANES
    kern = functools.partial(_mix_kernel, front=front, lp=lp, bsz=rows // lp, pack_rows=pack_rows)
    return pl.pallas_call(
        kern,
        grid=(rows // tm,),
        in_specs=[
            pl.BlockSpec((tm, gd), lambda i: (i, 0)),
            pl.BlockSpec((tm, gd), lambda i: (i, 0)),
            pl.BlockSpec((tm, d), lambda i: (i, 0)),
            pl.BlockSpec(w_out.shape, lambda i: (0, 0)),
            pl.BlockSpec((1, d), lambda i: (0, 0)),
            pl.BlockSpec((d, LANES), lambda i: (0, 0)),
            pl.BlockSpec((1, LANES), lambda i: (0, 0)),
        ],
        out_specs=[
            pl.BlockSpec((tm, d), lambda i: (i, 0)),
            pl.BlockSpec((tm * pack_rows, LANES), lambda i: (i, 0)),
            pl.BlockSpec((tm, LANES), lambda i: (i, 0)),
            pl.BlockSpec((tm, LANES), lambda i: (i, 0)),
            pl.BlockSpec((8, LANES), lambda i: (0, 0)),
        ],
        out_shape=[
            jax.ShapeDtypeStruct((rows, d), F32),
            jax.ShapeDtypeStruct((rows * pack_rows, LANES), U32),
            jax.ShapeDtypeStruct((rows, LANES), I32),
            jax.ShapeDtypeStruct((rows, LANES), F32),
            jax.ShapeDtypeStruct((8, LANES), F32),
        ],
        scratch_shapes=[pltpu.VMEM((8, LANES), F32)],
        compiler_params=_cparams(1),
        name="mix_router",
    )(o_gdn, o_sb, h0, w_out, norm_w, router_w, router_b)


def _dispatch_kernel(pstart_ref, padded_ref, used_ref, slot_ref, hnp_ref, xs_ref, zbuf, sem, zsem,
                     *, pack_rows, front, blocks_per_batch, n_slots):
    tm = DISPATCH_ROWS
    ts = EXP_SUB
    i = pl.program_id(0)
    n_exp = pstart_ref.shape[0]

    @pl.when(i == 0)
    def _():
        zbuf[...] = jnp.zeros_like(zbuf)

        def zero_copy(row):
            return pltpu.make_async_copy(
                zbuf, xs_ref.at[pl.ds(row * pack_rows, ts * pack_rows), :], zsem)

        def pad_block(e, _, *, start):
            @pl.when(padded_ref[e] > 0)
            def _():
                cp = zero_copy(pstart_ref[e] + padded_ref[e] - ts)
                cp.start() if start else cp.wait()
            return 0

        used = used_ref[0]
        n_tail = (n_slots - used) // ts
        lax.fori_loop(0, n_exp, functools.partial(pad_block, start=True), 0)
        lax.fori_loop(0, n_tail, lambda j, c: (zero_copy(used + j * ts).start(), c)[1], 0)
        lax.fori_loop(0, n_exp, functools.partial(pad_block, start=False), 0)
        lax.fori_loop(0, n_tail, lambda j, c: (zero_copy(used + j * ts).wait(), c)[1], 0)

    def copy(tok, j):
        src = pl.multiple_of(tok * pack_rows, pack_rows)
        return pltpu.make_async_copy(
            hnp_ref.at[pl.ds(src, pack_rows), :],
            xs_ref.at[pl.ds(slot_ref[0, tok * TOP_K + j] * pack_rows, pack_rows), :], sem)

    def start(tok, _):
        for j in range(TOP_K):
            copy(tok, j).start()
        return 0

    def wait_all(n_tok):
        for _ in range(TOP_K):
            pltpu.make_async_copy(
                hnp_ref.at[pl.ds(0, n_tok * pack_rows), :],
                xs_ref.at[pl.ds(0, n_tok * pack_rows), :], sem).wait()

    @pl.when(i % blocks_per_batch == 0)
    def _():
        lax.fori_loop(front, tm, start, 0)
        wait_all(tm - front)

    @pl.when(i % blocks_per_batch != 0)
    def _():
        lax.fori_loop(0, tm, start, 0)
        wait_all(tm)


def _dispatch(slots, hnp, pstart, padded, used_rows, n_slots, pack_rows, front, lp):
    rows = slots.shape[0]
    tm = DISPATCH_ROWS
    slots3 = slots.reshape(rows // tm, 1, tm * TOP_K)
    grid_spec = pltpu.PrefetchScalarGridSpec(
        num_scalar_prefetch=3,
        grid=(rows // tm,),
        in_specs=[
            pl.BlockSpec((None, 1, tm * TOP_K), lambda i, *_: (i, 0, 0), memory_space=pltpu.SMEM),
            pl.BlockSpec((tm * pack_rows, LANES), lambda i, *_: (i, 0)),
        ],
        out_specs=pl.BlockSpec(memory_space=pl.ANY),
        scratch_shapes=[pltpu.VMEM((EXP_SUB * pack_rows, LANES), U32),
                        pltpu.SemaphoreType.DMA(()), pltpu.SemaphoreType.DMA(())],
    )
    return pl.pallas_call(
        functools.partial(_dispatch_kernel, pack_rows=pack_rows, front=front,
                          blocks_per_batch=lp // tm, n_slots=n_slots),
        grid_spec=grid_spec,
        out_shape=jax.ShapeDtypeStruct((n_slots * pack_rows, LANES), U32),
        compiler_params=_cparams(1),
        name="dispatch",
    )(pstart, padded, used_rows, slots3, hnp)


def _expert_kernel(ue_ref, ur_ref, un_ref, used_ref, xs_ref, wg_ref, wu_ref, bg_ref, bu_ref, wd_ref,
                   bd_ref, ys_ref, xbuf, xb, acc, wg_b, wu_b, wd_b, ystage, sem_in, sem_out,
                   *, pack_rows, n_slots):
    del ue_ref
    ts = EXP_SUB
    u = pl.program_id(0)
    f = pl.program_id(1)
    n_f = pl.num_programs(1)
    nsub = un_ref[u]
    row0 = ur_ref[u]
    d = xb.shape[1]
    half = d // 2
    tf = wg_b.shape[1]

    @pl.when((u == 0) & (f == 0))
    def _():
        ystage[...] = jnp.zeros_like(ystage)
        used = used_ref[0]

        def tail_copy(i):
            return pltpu.make_async_copy(
                ystage.at[0], ys_ref.at[pl.ds((used + i * ts) * pack_rows, ts * pack_rows), :],
                sem_out.at[0])

        n_tail = (n_slots - used) // ts
        lax.fori_loop(0, n_tail, lambda i, c: (tail_copy(i).start(), c)[1], 0)
        lax.fori_loop(0, n_tail, lambda i, c: (tail_copy(i).wait(), c)[1], 0)

    def in_copy(s):
        return pltpu.make_async_copy(
            xs_ref.at[pl.ds((row0 + s * ts) * pack_rows, ts * pack_rows), :],
            xbuf.at[s % 2], sem_in.at[s % 2])

    @pl.when((f == 0) & (nsub > 0))
    def _():
        in_copy(0).start()

        def load(s, _):
            in_copy(s).wait()

            @pl.when(s + 1 < nsub)
            def _():
                in_copy(s + 1).start()
            words = jnp.concatenate(
                [xbuf[s % 2, pl.ds(c, ts, stride=pack_rows), :] for c in range(pack_rows)], axis=1)
            r = pl.multiple_of(s * ts, ts)
            xb[pl.ds(r, ts), :half] = pltpu.bitcast(words << 16, F32).astype(BF16)
            xb[pl.ds(r, ts), half:] = pltpu.bitcast(words & jnp.uint32(HIGH16), F32).astype(BF16)
            acc[pl.ds(r, ts), :] = jnp.broadcast_to(bd_ref[...], (ts, d))
            return 0
        lax.fori_loop(0, nsub, load, 0)

    @pl.when(nsub > 0)
    def _():
        wg_b[...] = wg_ref[...].astype(BF16)
        wu_b[...] = wu_ref[...].astype(BF16)
        wd_b[...] = wd_ref[...].astype(BF16)

        def ffn_rows(r, m):
            x = xb[pl.ds(r, m), :]
            g = jnp.minimum(_dot(x, wg_b[...]) + bg_ref[...], SWIGLU_LIMIT)
            up = jnp.clip(_dot(x, wu_b[...]) + bu_ref[...], -SWIGLU_LIMIT, SWIGLU_LIMIT)
            act = ((up + 1.0) * g * _sigmoid(SWIGLU_ALPHA * g)).astype(BF16)
            acc[pl.ds(r, m), :] += _dot(act, wd_b[...])

        group = EXP_GROUP

        def trip(p, _):
            ffn_rows(pl.multiple_of(p * (group * ts), ts), group * ts)
            return 0
        lax.fori_loop(0, nsub // group, trip, 0)
        for k in range(1, group):
            @pl.when(nsub % group == k)
            def _():
                ffn_rows(pl.multiple_of(nsub // group * (group * ts), ts), k * ts)

    def out_copy(s):
        return pltpu.make_async_copy(
            ystage.at[s % 2], ys_ref.at[pl.ds((row0 + s * ts) * pack_rows, ts * pack_rows), :],
            sem_out.at[s % 2])

    @pl.when((f == n_f - 1) & (nsub > 0))
    def _():
        def store(s, _):
            @pl.when(s >= 2)
            def _():
                out_copy(s - 2).wait()
            y = acc[pl.ds(pl.multiple_of(s * ts, ts), ts), :]
            lo = pltpu.bitcast(y[:, :half].astype(BF16).astype(F32), U32) >> 16
            hi = pltpu.bitcast(y[:, half:].astype(BF16).astype(F32), U32) & jnp.uint32(HIGH16)
            word = lo | hi
            for c in range(pack_rows):
                ystage[s % 2, pl.ds(c, ts, stride=pack_rows), :] = word[:, c * LANES:(c + 1) * LANES]
            out_copy(s).start()
            return 0
        lax.fori_loop(0, nsub, store, 0)

        @pl.when(nsub >= 2)
        def _():
            out_copy(nsub - 2).wait()
        out_copy(nsub - 1).wait()


def _experts(unit_e, unit_row0, unit_nsub, used_rows, xs, w_gate_up, b_gate_up, w_down, b_down,
             n_slots, pack_rows):
    n_exp, d, two_de = w_gate_up.shape
    de = two_de // 2
    tf = EXP_FTILE
    n_f = de // tf
    n_units = unit_e.shape[0]
    last_f = n_f - 1

    def fidx(u, f, un):
        return jnp.where(un[u] > 0, f, last_f)

    grid_spec = pltpu.PrefetchScalarGridSpec(
        num_scalar_prefetch=4,
        grid=(n_units, n_f),
        in_specs=[
            pl.BlockSpec(memory_space=pl.ANY),
            pl.BlockSpec((None, d, tf), lambda u, f, ue, ur, un, us: (ue[u], 0, fidx(u, f, un))),
            pl.BlockSpec((None, d, tf),
                         lambda u, f, ue, ur, un, us: (ue[u], 0, n_f + fidx(u, f, un))),
            pl.BlockSpec((None, 1, tf), lambda u, f, ue, ur, un, us: (ue[u], 0, fidx(u, f, un))),
            pl.BlockSpec((None, 1, tf),
                         lambda u, f, ue, ur, un, us: (ue[u], 0, n_f + fidx(u, f, un))),
            pl.BlockSpec((None, tf, d), lambda u, f, ue, ur, un, us: (ue[u], fidx(u, f, un), 0)),
            pl.BlockSpec((None, 1, d), lambda u, f, ue, ur, un, us: (ue[u], 0, 0)),
        ],
        out_specs=pl.BlockSpec(memory_space=pl.ANY),
        scratch_shapes=[
            pltpu.VMEM((2, EXP_SUB * pack_rows, LANES), U32),
            pltpu.VMEM((EXP_UNIT, d), BF16),
            pltpu.VMEM((EXP_UNIT, d), F32),
            pltpu.VMEM((d, tf), BF16),
            pltpu.VMEM((d, tf), BF16),
            pltpu.VMEM((tf, d), BF16),
            pltpu.VMEM((2, EXP_SUB * pack_rows, LANES), U32),
            pltpu.SemaphoreType.DMA((2,)),
            pltpu.SemaphoreType.DMA((2,)),
        ],
    )
    return pl.pallas_call(
        functools.partial(_expert_kernel, pack_rows=pack_rows, n_slots=n_slots),
        grid_spec=grid_spec,
        out_shape=jax.ShapeDtypeStruct((n_slots * pack_rows, LANES), U32),
        compiler_params=_cparams(2, vmem=EXPERT_VMEM_LIMIT),
        name="experts",
    )(unit_e, unit_row0, unit_nsub, used_rows, xs, w_gate_up, w_gate_up,
      b_gate_up.reshape(n_exp, 1, two_de), b_gate_up.reshape(n_exp, 1, two_de),
      w_down, b_down.reshape(n_exp, 1, d))


def _combine_kernel(slot_ref, gate_ref, h1_ref, ys_ref, o_ref, ybuf, sem, *, pack_rows):
    tc = COMB_ROWS
    half = h1_ref.shape[1] // 2

    def copy(tok, j):
        s = slot_ref[0, tok * TOP_K + j]
        dst = pl.multiple_of(tok * pack_rows, pack_rows)
        return pltpu.make_async_copy(
            ys_ref.at[pl.ds(s * pack_rows, pack_rows), :],
            ybuf.at[j, pl.ds(dst, pack_rows), :], sem)

    def start(tok, _):
        for j in range(TOP_K):
            copy(tok, j).start()
        return 0

    lax.fori_loop(0, tc, start, 0)
    for j in range(TOP_K):
        pltpu.make_async_copy(ys_ref.at[pl.ds(0, tc * pack_rows), :], ybuf.at[j], sem).wait()

    gates = gate_ref[...]
    for c in range(pack_rows):
        lo = h1_ref[:, c * LANES:(c + 1) * LANES]
        hi = h1_ref[:, half + c * LANES:half + (c + 1) * LANES]
        for j in range(TOP_K):
            word = ybuf[j, pl.ds(c, tc, stride=pack_rows), :]
            gate = gates[:, j:j + 1]
            lo = lo + gate * pltpu.bitcast(word << 16, F32)
            hi = hi + gate * pltpu.bitcast(word & jnp.uint32(HIGH16), F32)
        o_ref[:, c * LANES:(c + 1) * LANES] = lo
        o_ref[:, half + c * LANES:half + (c + 1) * LANES] = hi


def _combine(slots, gates, h1, ys, bsz, seq, lp, d):
    tc = COMB_ROWS
    rows = slots.shape[0]
    pack_rows = d // 2 // LANES
    slots3 = slots.reshape(rows // tc, 1, tc * TOP_K)
    nb = lp // tc
    first = (lp - seq) // tc
    return pl.pallas_call(
        functools.partial(_combine_kernel, pack_rows=pack_rows),
        grid=(bsz, seq // tc),
        in_specs=[
            pl.BlockSpec((None, 1, tc * TOP_K), lambda b, i: (b * nb + first + i, 0, 0),
                         memory_space=pltpu.SMEM),
            pl.BlockSpec((tc, LANES), lambda b, i: (b * nb + first + i, 0)),
            pl.BlockSpec((tc, d), lambda b, i: (b * nb + first + i, 0)),
            pl.BlockSpec(memory_space=pl.ANY),
        ],
        out_specs=pl.BlockSpec((None, tc, d), lambda b, i: (b, i, 0)),
        out_shape=jax.ShapeDtypeStruct((bsz, seq, d), F32),
        scratch_shapes=[pltpu.VMEM((TOP_K, tc * pack_rows, LANES), U32), pltpu.SemaphoreType.DMA(())],
        compiler_params=_cparams(2),
        name="combine",
    )(slots3, gates, h1, ys)


def _pick(n, candidates):
    for c in candidates:
        if n % c == 0:
            return c
    raise ValueError(f"no block size in {candidates} divides {n}")


def _plan(info, cnt, n_exp, rows, lp, front, n_assign):
    counts = cnt[0, :n_exp].astype(I32)
    padded = (counts + EXP_SUB - 1) // EXP_SUB * EXP_SUB
    pstart = jnp.cumsum(padded) - padded
    eid = info[:, :TOP_K]
    rank = info[:, TOP_K:2 * TOP_K]
    onehot = eid[:, :, None] == jnp.arange(n_exp, dtype=I32)[None, None, :]
    slot = rank + jnp.sum(jnp.where(onehot, pstart[None, None, :], 0), axis=-1)
    row_valid = (jnp.arange(rows, dtype=I32) % lp) >= front
    slots = jnp.where(row_valid[:, None], slot, -1).astype(I32)

    n_slots = (n_assign + n_exp * (EXP_SUB - 1)) // EXP_SUB * EXP_SUB
    units_per_e = (padded + EXP_UNIT - 1) // EXP_UNIT
    cum_units = jnp.cumsum(units_per_e)
    n_units = n_slots // EXP_UNIT + n_exp
    uidx = jnp.arange(n_units, dtype=I32)
    ue = jnp.sum(cum_units[None, :] <= uidx[:, None], axis=1).astype(I32)
    live = ue < n_exp
    last_e = jnp.max(jnp.where(counts > 0, jnp.arange(n_exp, dtype=I32), 0))
    ue = jnp.where(live, ue, last_e)
    k_in_e = uidx - (cum_units - units_per_e)[ue]
    nsub_e = (padded // EXP_SUB)[ue]
    units_e = jnp.maximum(units_per_e[ue], 1)
    base = nsub_e // units_e
    extra = nsub_e - base * units_e
    first_sub = k_in_e * base + jnp.minimum(k_in_e, extra)
    unit_row0 = jnp.where(live, pstart[ue] + first_sub * EXP_SUB, 0).astype(I32)
    unit_nsub = jnp.where(live, base + (k_in_e < extra), 0).astype(I32)
    used_rows = jnp.sum(padded).astype(I32)[None]
    return slots, (pstart.astype(I32), padded.astype(I32)), (ue, unit_row0, unit_nsub, used_rows), n_slots


def kernel(x, meta_tokens, mix_norm_w, w_in, conv_w, a_log, dt_bias, gdn_norm_w, sb_q_norm_w,
           sb_k_norm_w, sb_out_norm_w, w_out, ffn_norm_w, router_w, router_b, w_gate_up, b_gate_up,
           w_down, b_down):
    bsz, seq, d = x.shape
    n_meta = meta_tokens.shape[0]
    depth = mix_norm_w.shape[0]
    heads = a_log.shape[1]
    hd = gdn_norm_w.shape[1]
    gd = heads * hd
    sbd = (w_in.shape[2] - 4 * gd - 2 * heads) // 3
    sb_heads = sbd // hd
    n_exp = router_w.shape[2]
    assert seq % ROW_ALIGN == 0 and d % (2 * 8 * LANES) == 0 and hd == LANES
    assert 2 * heads <= LANES and n_exp <= LANES and sb_heads == heads
    assert depth == 1, "a second layer would need the meta rows carried through the combine stage"
    front = (-n_meta) % ROW_ALIGN
    lp = front + n_meta + seq
    rows = bsz * lp
    pack_rows = d // 2 // LANES

    h = jnp.concatenate([
        jnp.zeros((bsz, front, d), x.dtype),
        jnp.broadcast_to(meta_tokens.astype(x.dtype)[None], (bsz, n_meta, d)),
        x], axis=1).reshape(rows, d)

    wl = w_in[0]
    n_ba = 4 * gd
    w_main = jnp.concatenate([wl[:, :n_ba], wl[:, n_ba + 2 * heads:]], axis=1).astype(BF16)
    w_ba = jnp.pad(wl[:, n_ba:n_ba + 2 * heads], ((0, 0), (0, LANES - 2 * heads))).astype(BF16)
    tm = _pick(rows, (1024, 512, 256))
    tn = _pick(w_main.shape[1], (1024, 512, 256, 128))
    proj, ba = _in_proj(h, mix_norm_w[0][None], w_main, w_ba, tm, tn)
    proj3 = proj.reshape(bsz, lp, -1)
    ba3 = ba.reshape(bsz, lp, LANES)

    gparams = jnp.zeros((2, LANES), F32)
    gparams = gparams.at[0, heads:2 * heads].set(a_log[0].astype(F32))
    gparams = gparams.at[1, heads:2 * heads].set(dt_bias[0].astype(F32))
    qkv3, bg3 = _gdn_prep(proj3, ba3, conv_w[0].astype(F32), gparams, heads, hd)
    o_gdn = _gdn(qkv3, proj3, bg3, gdn_norm_w[0][None], heads, hd)
    o_sb = _sb(proj3, sb_q_norm_w[0][None], sb_k_norm_w[0][None], sb_out_norm_w[0][None],
               sb_heads, hd, 4 * gd, front)

    rw = jnp.pad(router_w[0].astype(F32), ((0, 0), (0, LANES - n_exp)))
    rb = jnp.pad(router_b[0].astype(F32), (0, LANES - n_exp), constant_values=-1e30)[None]
    h1, hnp, info, gates, cnt = _mix(
        o_gdn.reshape(rows, gd), o_sb.reshape(rows, sbd), h, w_out[0].astype(BF16),
        ffn_norm_w[0][None], rw, rb, front, lp)

    slots, (pstart, padded), units, n_slots = _plan(
        info, cnt, n_exp, rows, lp, front, bsz * (n_meta + seq) * TOP_K)
    xs = _dispatch(slots, hnp, pstart, padded, units[3], n_slots, pack_rows, front, lp)
    ys = _experts(*units, xs, w_gate_up[0], b_gate_up[0], w_down[0], b_down[0], n_slots, pack_rows)
    return _combine(slots, gates, h1, ys, bsz, seq, lp, d)
```

```python
import functools

import jax
import jax.numpy as jnp
from jax import lax
from jax.experimental import pallas as pl
from jax.experimental.pallas import tpu as pltpu

F32 = jnp.float32
BF16 = jnp.bfloat16
U32 = jnp.uint32
I32 = jnp.int32

NORM_EPS = 1e-6
TOP_K = 4
SWIGLU_LIMIT = 7.0
SWIGLU_ALPHA = 1.702
LANES = 128
ROW_ALIGN = 256
GDN_PREP_ROWS = 256
GDN_CHUNK = 64
SB_BLOCK = 256
SB_GROUP = 4
MIX_ROWS = 512
DISPATCH_ROWS = 256
EXP_SUB = 256
EXP_GROUP = 3
EXP_UNIT = 9 * EXP_SUB
EXP_FTILE = 256
EXP_OUT_STAGES = 4
COMB_ROWS = 256
VMEM_LIMIT = 56 * 1024 * 1024
EXPERT_VMEM_LIMIT = 60 * 1024 * 1024
HIGH16 = 0xFFFF0000
SIGN_BIT = 0x80000000
LOG2E = 1.4426950408889634


def _cparams(n_grid, vmem=VMEM_LIMIT):
    return pltpu.CompilerParams(dimension_semantics=("arbitrary",) * n_grid, vmem_limit_bytes=vmem)


def _dot(a, b):
    return jnp.dot(a, b, preferred_element_type=F32)


def _dot_nt(a, b):
    return lax.dot_general(a, b, (((1,), (1,)), ((), ())), preferred_element_type=F32)


def _dot_tn(a, b):
    return lax.dot_general(a, b, (((0,), (0,)), ((), ())), preferred_element_type=F32)


def _split3(x):
    hi = x.astype(BF16)
    r1 = x - hi.astype(F32)
    mid = r1.astype(BF16)
    lo = (r1 - mid.astype(F32)).astype(BF16)
    return hi, mid, lo


def _dot3(a, b):
    ah = a.astype(BF16)
    al = (a - ah.astype(F32)).astype(BF16)
    bh = b.astype(BF16)
    bl = (b - bh.astype(F32)).astype(BF16)
    return _dot(ah, bh) + (_dot(ah, bl) + _dot(al, bh))


def _sigmoid(x):
    return 1.0 / (1.0 + jnp.exp(-x))


def _softplus(x):
    return jnp.maximum(x, 0.0) + jnp.log1p(jnp.exp(-jnp.abs(x)))


def _inproj_kernel(h_ref, nw_ref, w_ref, wba_ref, o_ref, ba_ref, xn_ref):
    @pl.when(pl.program_id(1) == 0)
    def _():
        x = h_ref[...]
        xn = x * lax.rsqrt(jnp.mean(x * x, axis=-1, keepdims=True) + NORM_EPS) * nw_ref[...]
        xn = xn.astype(BF16)
        xn_ref[...] = xn
        ba_ref[...] = _dot(xn, wba_ref[...])

    o_ref[...] = _dot(xn_ref[...], w_ref[...]).astype(o_ref.dtype)


def _in_proj(h0, norm_w, w_main, w_ba, tm, tn):
    rows, d = h0.shape
    n_main = w_main.shape[1]
    return pl.pallas_call(
        _inproj_kernel,
        grid=(rows // tm, n_main // tn),
        in_specs=[
            pl.BlockSpec((tm, d), lambda i, n: (i, 0)),
            pl.BlockSpec((1, d), lambda i, n: (0, 0)),
            pl.BlockSpec((d, tn), lambda i, n: (0, n)),
            pl.BlockSpec((d, LANES), lambda i, n: (0, 0)),
        ],
        out_specs=[
            pl.BlockSpec((tm, tn), lambda i, n: (i, n)),
            pl.BlockSpec((tm, LANES), lambda i, n: (i, 0)),
        ],
        out_shape=[
            jax.ShapeDtypeStruct((rows, n_main), BF16),
            jax.ShapeDtypeStruct((rows, LANES), F32),
        ],
        scratch_shapes=[pltpu.VMEM((tm, d), BF16)],
        compiler_params=_cparams(2),
        name="in_proj",
    )(h0, norm_w, w_main, w_ba)


def _gdn_prep_kernel(x_ref, prev_ref, ba_ref, cw_ref, gp_ref, o_ref, bg_ref, *, heads, hd):
    tp = x_ref.shape[0]
    gd = heads * hd
    taps = cw_ref.shape[0]
    x = x_ref[...].astype(F32)
    prev = jnp.where(pl.program_id(1) > 0, prev_ref[8:16, :].astype(F32), 0.0)
    xs = jnp.concatenate([prev, x], axis=0)
    y = cw_ref[taps - 1:taps, :] * x
    for s in range(1, taps):
        y = y + cw_ref[taps - 1 - s:taps - s, :] * pltpu.roll(xs, s, axis=0)[8:8 + tp, :]
    y = y * _sigmoid(y)
    for h in range(heads):
        q = y[:, h * hd:(h + 1) * hd]
        k = y[:, gd + h * hd:gd + (h + 1) * hd]
        q = q * (lax.rsqrt(jnp.sum(q * q, axis=-1, keepdims=True) + NORM_EPS) * (hd ** -0.5))
        k = k * lax.rsqrt(jnp.sum(k * k, axis=-1, keepdims=True) + NORM_EPS)
        o_ref[:, h * hd:(h + 1) * hd] = q.astype(o_ref.dtype)
        o_ref[:, gd + h * hd:gd + (h + 1) * hd] = k.astype(o_ref.dtype)
    o_ref[:, 2 * gd:] = y[:, 2 * gd:].astype(o_ref.dtype)
    ba = ba_ref[...]
    lane = lax.broadcasted_iota(I32, ba.shape, 1)
    decay = -jnp.exp(gp_ref[0:1, :]) * _softplus(ba + gp_ref[1:2, :])
    bg_ref[...] = jnp.where(lane < heads, _sigmoid(ba), decay)


def _gdn_prep(proj3, ba3, conv_w, gparams, heads, hd):
    bsz, lp, _ = proj3.shape
    gd = heads * hd
    tp = GDN_PREP_ROWS
    return pl.pallas_call(
        functools.partial(_gdn_prep_kernel, heads=heads, hd=hd),
        grid=(bsz, lp // tp),
        in_specs=[
            pl.BlockSpec((None, tp, 3 * gd), lambda b, i: (b, i, 0)),
            pl.BlockSpec((None, 16, 3 * gd), lambda b, i: (b, jnp.maximum(i * (tp // 16) - 1, 0), 0)),
            pl.BlockSpec((None, tp, LANES), lambda b, i: (b, i, 0)),
            pl.BlockSpec(conv_w.shape, lambda b, i: (0, 0)),
            pl.BlockSpec((2, LANES), lambda b, i: (0, 0)),
        ],
        out_specs=[
            pl.BlockSpec((None, tp, 3 * gd), lambda b, i: (b, i, 0)),
            pl.BlockSpec((None, tp, LANES), lambda b, i: (b, i, 0)),
        ],
        out_shape=[
            jax.ShapeDtypeStruct((bsz, lp, 3 * gd), BF16),
            jax.ShapeDtypeStruct((bsz, lp, LANES), F32),
        ],
        compiler_params=_cparams(2),
        name="gdn_prep",
    )(proj3, proj3, ba3, conv_w, gparams)


def _gdn_kernel(qkv_ref, z_ref, bg_ref, nw_ref, o_ref, state, *, heads, hd):
    c_rows = GDN_CHUNK
    gd = heads * hd

    @pl.when(pl.program_id(1) == 0)
    def _():
        state[...] = jnp.zeros_like(state)

    bg = bg_ref[...]
    r_i = lax.broadcasted_iota(I32, (c_rows, c_rows), 0)
    c_i = lax.broadcasted_iota(I32, (c_rows, c_rows), 1)
    incl = r_i >= c_i
    strict = r_i > c_i
    tri = incl.astype(BF16)
    g_parts = _split3(bg)
    gcum = _dot(tri, g_parts[0]) + _dot(tri, g_parts[1]) + _dot(tri, g_parts[2])
    n_sel = -(-heads // 8) * 8
    pick = (lax.broadcasted_iota(I32, (n_sel, LANES), 1)
            == lax.broadcasted_iota(I32, (n_sel, LANES), 0) + heads).astype(BF16)
    c_parts = _split3(gcum)
    gcum_rows = _dot_nt(pick, c_parts[0]) + _dot_nt(pick, c_parts[1]) + _dot_nt(pick, c_parts[2])

    hs = range(heads)
    q16 = [qkv_ref[:, h * hd:(h + 1) * hd] for h in hs]
    k16 = [qkv_ref[:, gd + h * hd:gd + (h + 1) * hd] for h in hs]
    k = [k16[h].astype(F32) for h in hs]
    gc = [gcum[:, heads + h:heads + h + 1] for h in hs]
    decay = [jnp.exp(jnp.minimum(gc[h] - gcum_rows[h:h + 1, :], 0.0)) for h in hs]
    kb = [k[h] * bg[:, h:h + 1] for h in hs]
    p = [jnp.where(strict, _dot_nt(kb[h].astype(BF16), k16[h]) * decay[h], 0.0) for h in hs]
    p = [(-p[h]).astype(BF16) for h in hs]
    sol = [jnp.concatenate(
        [qkv_ref[:, 2 * gd + h * hd:2 * gd + (h + 1) * hd].astype(F32) * bg[:, h:h + 1],
         kb[h] * jnp.exp(gc[h])], axis=1) for h in hs]
    n_fac = c_rows.bit_length() - 1
    for i in range(n_fac):
        sol = [sol[h] + _dot(p[h], sol[h].astype(BF16)) for h in hs]
        if i + 1 < n_fac:
            p = [_dot(p[h], p[h]).astype(BF16) for h in hs]
    attn = [jnp.where(incl, _dot_nt(q16[h], k16[h]) * decay[h], 0.0).astype(BF16) for h in hs]
    s_old = [state[h] for h in hs]
    s_b = [s_old[h].astype(BF16) for h in hs]
    v_new = [(sol[h][:, :hd] - _dot(sol[h][:, hd:].astype(BF16), s_b[h])).astype(BF16) for h in hs]
    o = [_dot((q16[h].astype(F32) * jnp.exp(gc[h])).astype(BF16), s_b[h]) + _dot(attn[h], v_new[h])
         for h in hs]
    for h in hs:
        g_last = gc[h][c_rows - 1:c_rows, :]
        k_dec = (k[h] * jnp.exp(g_last - gc[h])).astype(BF16)
        state[h] = s_old[h] * jnp.exp(g_last) + _dot_tn(k_dec, v_new[h])
    for h in hs:
        zf = z_ref[:, h * hd:(h + 1) * hd].astype(F32)
        o_n = o[h] * lax.rsqrt(jnp.mean(o[h] * o[h], axis=-1, keepdims=True) + NORM_EPS) * nw_ref[...]
        o_ref[:, h * hd:(h + 1) * hd] = (o_n * (zf * _sigmoid(zf))).astype(o_ref.dtype)


def _gdn(qkv3, proj3, bg3, norm_w, heads, hd):
    bsz, lp, _ = qkv3.shape
    gd = heads * hd
    c = GDN_CHUNK
    return pl.pallas_call(
        functools.partial(_gdn_kernel, heads=heads, hd=hd),
        grid=(bsz, lp // c),
        in_specs=[
            pl.BlockSpec((None, c, 3 * gd), lambda b, i: (b, i, 0)),
            pl.BlockSpec((None, c, gd), lambda b, i: (b, i, 3)),
            pl.BlockSpec((None, c, LANES), lambda b, i: (b, i, 0)),
            pl.BlockSpec((1, hd), lambda b, i: (0, 0)),
        ],
        out_specs=pl.BlockSpec((None, c, gd), lambda b, i: (b, i, 0)),
        out_shape=jax.ShapeDtypeStruct((bsz, lp, gd), BF16),
        scratch_shapes=[pltpu.VMEM((heads, hd, hd), F32)],
        compiler_params=_cparams(2),
        name="gdn",
    )(qkv3, proj3, bg3, norm_w)


def _sb_kernel(q_ref, k_ref, v_ref, qw_ref, kw_ref, ow_ref, tri_ref, o_ref, kn_ref, acc_ref,
               drop_ref, *, front, hd):
    t = SB_BLOCK
    qi = pl.program_id(2)
    n_blocks = k_ref.shape[0] // t

    @pl.when(qi == 0)
    def _():
        def norm_keys(i, _):
            kb = k_ref[pl.ds(i * t, t), :].astype(F32)
            kn = kb * lax.rsqrt(jnp.mean(kb * kb, axis=-1, keepdims=True) + NORM_EPS) * kw_ref[...]
            kn_ref[pl.ds(i * t, t), :] = kn.astype(BF16)
            return 0
        lax.fori_loop(0, n_blocks, norm_keys, 0)

    q = q_ref[...].astype(F32)
    qn = q * lax.rsqrt(jnp.mean(q * q, axis=-1, keepdims=True) + NORM_EPS) * qw_ref[...]
    qn = (qn * (hd ** -0.5 * LOG2E)).astype(BF16)

    acc_ref[...] = jnp.zeros_like(acc_ref)
    drop_ref[...] = jnp.zeros_like(drop_ref)

    def mask_of(kind):
        r_i = lax.broadcasted_iota(I32, (t, t), 0)
        c_i = lax.broadcasted_iota(I32, (t, t), 1)
        if kind == "causal":
            return c_i < r_i
        if kind == "front":
            return c_i >= front
        return (c_i < r_i) & (c_i >= front)

    def scores(kj, kind):
        start = pl.multiple_of(kj * t, t)
        s = _dot_nt(qn, kn_ref[pl.ds(start, t), :])
        neg_abs = pltpu.bitcast(pltpu.bitcast(s, U32) | jnp.uint32(SIGN_BIT), F32)
        drop = jnp.maximum(s, 0.0) + jnp.log2(1.0 + jnp.exp2(neg_abs))
        visible = None
        if kind is not None:
            visible = mask_of(kind)
            drop = jnp.where(visible, drop, 0.0)
        hi = pltpu.bitcast(pltpu.bitcast(drop, U32) & jnp.uint32(HIGH16), F32)
        lo = drop - hi
        tail = _dot(jnp.concatenate([hi.astype(BF16), lo.astype(BF16)], axis=1), tri_ref[...])
        return s, tail, visible

    def run(tiles):
        carry = (acc_ref[...], drop_ref[...])

        def finish(tile, part, carry):
            (kj, kind), (s, tail, visible), (acc, dropped) = tile, part, carry
            weight = jnp.exp2(s - tail - jnp.concatenate([dropped] * (t // LANES), axis=1))
            if kind is not None:
                weight = jnp.where(visible, weight, 0.0)
            start = pl.multiple_of(kj * t, t)
            acc = acc + _dot(weight.astype(BF16), v_ref[pl.ds(start, t), :])
            return acc, dropped + jnp.broadcast_to(tail[:, 0:1], dropped.shape)

        pending = None
        for tile in tiles:
            part = scores(*tile)
            if pending is not None:
                carry = finish(*pending, carry)
            pending = (tile, part)
        acc_ref[...], drop_ref[...] = finish(*pending, carry)

    @pl.when(qi == 0)
    def _():
        run([(0, "causal_front")])

    @pl.when(qi == 1)
    def _():
        run([(1, "causal"), (0, "front")])

    @pl.when(qi >= 2)
    def _():
        run([(qi, "causal"), (qi - 1, None)])
        rest = qi - 2
        group = SB_GROUP

        def many(p, _):
            top = qi - 2 - group * p
            run([(top - k, None) for k in range(group)])
            return 0
        lax.fori_loop(0, rest // group, many, 0)
        for left in range(group):
            @pl.when(rest % group == left)
            def _():
                run([(k, None) for k in range(left, 0, -1)] + [(0, "front")])

    o = acc_ref[...]
    o = o * lax.rsqrt(jnp.mean(o * o, axis=-1, keepdims=True) + NORM_EPS) * ow_ref[...]
    o_ref[...] = o.astype(o_ref.dtype)


def _sb(proj3, qw, kw, ow, heads, hd, col0, front):
    bsz, lp, _ = proj3.shape
    t = SB_BLOCK
    cb = col0 // hd
    tri = (jnp.arange(t)[:, None] >= jnp.arange(t)[None, :])
    tri = jnp.concatenate([tri, tri], axis=0).astype(BF16)
    return pl.pallas_call(
        functools.partial(_sb_kernel, front=front, hd=hd),
        grid=(bsz, heads, lp // t),
        in_specs=[
            pl.BlockSpec((None, t, hd), lambda b, h, i: (b, i, cb + h)),
            pl.BlockSpec((None, lp, hd), lambda b, h, i: (b, 0, cb + heads + h)),
            pl.BlockSpec((None, lp, hd), lambda b, h, i: (b, 0, cb + 2 * heads + h)),
            pl.BlockSpec((1, hd), lambda b, h, i: (0, 0)),
            pl.BlockSpec((1, hd), lambda b, h, i: (0, 0)),
            pl.BlockSpec((1, hd), lambda b, h, i: (0, 0)),
            pl.BlockSpec(tri.shape, lambda b, h, i: (0, 0)),
        ],
        out_specs=pl.BlockSpec((None, t, hd), lambda b, h, i: (b, i, h)),
        out_shape=jax.ShapeDtypeStruct((bsz, lp, heads * hd), BF16),
        scratch_shapes=[pltpu.VMEM((lp, hd), BF16), pltpu.VMEM((t, hd), F32),
                        pltpu.VMEM((t, LANES), F32)],
        compiler_params=_cparams(3),
        name="sb",
    )(proj3, proj3, proj3, qw, kw, ow, tri)


def _mix_kernel(og_ref, os_ref, h_ref, wo_ref, nw_ref, rw_ref, rb_ref,
                h1_ref, hnp_ref, info_ref, gate_ref, cnt_ref, cnt_acc,
                *, front, lp, bsz, pack_rows):
    tm = MIX_ROWS
    i = pl.program_id(0)

    @pl.when(i == 0)
    def _():
        cnt_acc[...] = jnp.zeros_like(cnt_acc)

    gd = og_ref.shape[1]
    h1 = h_ref[...] + _dot(og_ref[...], wo_ref[0:gd, :]) + _dot(os_ref[...], wo_ref[gd:, :])
    h1_ref[...] = h1
    hn = h1 * lax.rsqrt(jnp.mean(h1 * h1, axis=-1, keepdims=True) + NORM_EPS) * nw_ref[...]

    half = hn.shape[1] // 2
    lo = pltpu.bitcast(hn[:, :half].astype(BF16).astype(F32), U32) >> 16
    hi = pltpu.bitcast(hn[:, half:].astype(BF16).astype(F32), U32) & jnp.uint32(HIGH16)
    word = lo | hi
    for s in range(pack_rows):
        hnp_ref[pl.ds(s, tm, stride=pack_rows), :] = word[:, s * LANES:(s + 1) * LANES]

    logits = _dot3(hn, rw_ref[...]) + rb_ref[...]
    lane = lax.broadcasted_iota(I32, (tm, LANES), 1)
    row = lax.broadcasted_iota(I32, (tm, 1), 0)
    pos = i * tm + row
    valid = pos < 0
    for b in range(bsz):
        valid = valid | ((pos >= b * lp + front) & (pos < (b + 1) * lp))
    work = logits
    tops, idxs, hots = [], [], []
    for _ in range(TOP_K):
        m = jnp.max(work, axis=1, keepdims=True)
        idx = jnp.min(jnp.where(work == m, lane, LANES), axis=1, keepdims=True)
        hot = lane == idx
        tops.append(m)
        idxs.append(idx)
        hots.append(hot)
        work = jnp.where(hot, -jnp.inf, work)
    exps = [jnp.exp(m - tops[0]) for m in tops]
    denom = exps[0] + exps[1] + exps[2] + exps[3]
    sel = jnp.zeros((tm, LANES), F32)
    for hot in hots:
        sel = sel + hot.astype(F32)
    sel = jnp.where(valid, sel, 0.0)

    r_i = lax.broadcasted_iota(I32, (tm, tm), 0)
    c_i = lax.broadcasted_iota(I32, (tm, tm), 1)
    before = (r_i > c_i).astype(BF16)
    rank = cnt_acc[0:1, :] + _dot(before, sel.astype(BF16))
    info = jnp.zeros((tm, LANES), I32)
    gates = jnp.zeros((tm, LANES), F32)
    for j in range(TOP_K):
        rank_j = jnp.sum(jnp.where(hots[j], rank, 0.0), axis=1, keepdims=True).astype(I32)
        info = jnp.where(lane == j, idxs[j], info)
        info = jnp.where(lane == TOP_K + j, rank_j, info)
        gates = jnp.where(lane == j, exps[j] / denom, gates)
    info_ref[...] = info
    gate_ref[...] = gates
    cnt_acc[...] = cnt_acc[...] + jnp.sum(sel, axis=0, keepdims=True)
    cnt_ref[...] = cnt_acc[...]


def _mix(o_gdn, o_sb, h0, w_out, norm_w, router_w, router_b, front, lp):
    rows, d = h0.shape
    gd = o_gdn.shape[1]
    tm = MIX_ROWS
    pack_rows = d // 2 // LANES
    kern = functools.partial(_mix_kernel, front=front, lp=lp, bsz=rows // lp, pack_rows=pack_rows)
    return pl.pallas_call(
        kern,
        grid=(rows // tm,),
        in_specs=[
            pl.BlockSpec((tm, gd), lambda i: (i, 0)),
            pl.BlockSpec((tm, gd), lambda i: (i, 0)),
            pl.BlockSpec((tm, d), lambda i: (i, 0)),
            pl.BlockSpec(w_out.shape, lambda i: (0, 0)),
            pl.BlockSpec((1, d), lambda i: (0, 0)),
            pl.BlockSpec((d, LANES), lambda i: (0, 0)),
            pl.BlockSpec((1, LANES), lambda i: (0, 0)),
        ],
        out_specs=[
            pl.BlockSpec((tm, d), lambda i: (i, 0)),
            pl.BlockSpec((tm * pack_rows, LANES), lambda i: (i, 0)),
            pl.BlockSpec((tm, LANES), lambda i: (i, 0)),
            pl.BlockSpec((tm, LANES), lambda i: (i, 0)),
            pl.BlockSpec((8, LANES), lambda i: (0, 0)),
        ],
        out_shape=[
            jax.ShapeDtypeStruct((rows, d), F32),
            jax.ShapeDtypeStruct((rows * pack_rows, LANES), U32),
            jax.ShapeDtypeStruct((rows, LANES), I32),
            jax.ShapeDtypeStruct((rows, LANES), F32),
            jax.ShapeDtypeStruct((8, LANES), F32),
        ],
        scratch_shapes=[pltpu.VMEM((8, LANES), F32)],
        compiler_params=_cparams(1),
        name="mix_router",
    )(o_gdn, o_sb, h0, w_out, norm_w, router_w, router_b)


def _dispatch_kernel(pstart_ref, padded_ref, used_ref, slot_ref, hnp_ref, xs_ref, zbuf, sem, zsem,
                     *, pack_rows, front, blocks_per_batch, n_slots):
    tm = DISPATCH_ROWS
    ts = EXP_SUB
    i = pl.program_id(0)
    n_exp = pstart_ref.shape[0]

    @pl.when(i == 0)
    def _():
        zbuf[...] = jnp.zeros_like(zbuf)

        def zero_copy(row):
            return pltpu.make_async_copy(
                zbuf, xs_ref.at[pl.ds(row * pack_rows, ts * pack_rows), :], zsem)

        def pad_block(e, _, *, start):
            @pl.when(padded_ref[e] > 0)
            def _():
                cp = zero_copy(pstart_ref[e] + padded_ref[e] - ts)
                cp.start() if start else cp.wait()
            return 0

        used = used_ref[0]
        n_tail = (n_slots - used) // ts
        lax.fori_loop(0, n_exp, functools.partial(pad_block, start=True), 0)
        lax.fori_loop(0, n_tail, lambda j, c: (zero_copy(used + j * ts).start(), c)[1], 0)
        lax.fori_loop(0, n_exp, functools.partial(pad_block, start=False), 0)
        lax.fori_loop(0, n_tail, lambda j, c: (zero_copy(used + j * ts).wait(), c)[1], 0)

    def copy(tok, j):
        src = pl.multiple_of(tok * pack_rows, pack_rows)
        return pltpu.make_async_copy(
            hnp_ref.at[pl.ds(src, pack_rows), :],
            xs_ref.at[pl.ds(slot_ref[0, tok * TOP_K + j] * pack_rows, pack_rows), :], sem)

    def start(tok, _):
        for j in range(TOP_K):
            copy(tok, j).start()
        return 0

    def wait_all(n_tok):
        for _ in range(TOP_K):
            pltpu.make_async_copy(
                hnp_ref.at[pl.ds(0, n_tok * pack_rows), :],
                xs_ref.at[pl.ds(0, n_tok * pack_rows), :], sem).wait()

    @pl.when(i % blocks_per_batch == 0)
    def _():
        lax.fori_loop(front, tm, start, 0)
        wait_all(tm - front)

    @pl.when(i % blocks_per_batch != 0)
    def _():
        lax.fori_loop(0, tm, start, 0)
        wait_all(tm)


def _dispatch(slots, hnp, pstart, padded, used_rows, n_slots, pack_rows, front, lp):
    rows = slots.shape[0]
    tm = DISPATCH_ROWS
    slots3 = slots.reshape(rows // tm, 1, tm * TOP_K)
    grid_spec = pltpu.PrefetchScalarGridSpec(
        num_scalar_prefetch=3,
        grid=(rows // tm,),
        in_specs=[
            pl.BlockSpec((None, 1, tm * TOP_K), lambda i, *_: (i, 0, 0), memory_space=pltpu.SMEM),
            pl.BlockSpec((tm * pack_rows, LANES), lambda i, *_: (i, 0)),
        ],
        out_specs=pl.BlockSpec(memory_space=pl.ANY),
        scratch_shapes=[pltpu.VMEM((EXP_SUB * pack_rows, LANES), U32),
                        pltpu.SemaphoreType.DMA(()), pltpu.SemaphoreType.DMA(())],
    )
    return pl.pallas_call(
        functools.partial(_dispatch_kernel, pack_rows=pack_rows, front=front,
                          blocks_per_batch=lp // tm, n_slots=n_slots),
        grid_spec=grid_spec,
        out_shape=jax.ShapeDtypeStruct((n_slots * pack_rows, LANES), U32),
        compiler_params=_cparams(1),
        name="dispatch",
    )(pstart, padded, used_rows, slots3, hnp)


def _expert_kernel(ue_ref, ur_ref, un_ref, used_ref, xs_ref, wg_ref, wu_ref, bg_ref, bu_ref, wd_ref,
                   bd_ref, ys_ref, xbuf, xb, acc, wg_b, wu_b, wd_b, ystage, sem_in, sem_out,
                   *, pack_rows, n_slots):
    del ue_ref
    ts = EXP_SUB
    u = pl.program_id(0)
    f = pl.program_id(1)
    n_f = pl.num_programs(1)
    nsub = un_ref[u]
    row0 = ur_ref[u]
    d = xb.shape[1]
    half = d // 2
    tf = wg_b.shape[1]

    @pl.when((u == 0) & (f == 0))
    def _():
        ystage[...] = jnp.zeros_like(ystage)
        used = used_ref[0]

        def tail_copy(i):
            return pltpu.make_async_copy(
                ystage.at[0], ys_ref.at[pl.ds((used + i * ts) * pack_rows, ts * pack_rows), :],
                sem_out.at[0])

        n_tail = (n_slots - used) // ts
        lax.fori_loop(0, n_tail, lambda i, c: (tail_copy(i).start(), c)[1], 0)
        lax.fori_loop(0, n_tail, lambda i, c: (tail_copy(i).wait(), c)[1], 0)

    def in_copy(unit_row0, s):
        return pltpu.make_async_copy(
            xs_ref.at[pl.ds((unit_row0 + s * ts) * pack_rows, ts * pack_rows), :],
            xbuf.at[s], sem_in.at[s])

    def start_loads(unit):
        unit_row0 = ur_ref[unit]
        lax.fori_loop(0, un_ref[unit], lambda s, c: (in_copy(unit_row0, s).start(), c)[1], 0)

    @pl.when((u == 0) & (f == 0))
    def _():
        start_loads(0)

    @pl.when((f == 0) & (nsub > 0))
    def _():
        def load(s, _):
            in_copy(row0, s).wait()
            words = jnp.concatenate(
                [xbuf[s, pl.ds(c, ts, stride=pack_rows), :] for c in range(pack_rows)], axis=1)
            r = pl.multiple_of(s * ts, ts)
            xb[pl.ds(r, ts), :half] = pltpu.bitcast(words << 16, F32).astype(BF16)
            xb[pl.ds(r, ts), half:] = pltpu.bitcast(words & jnp.uint32(HIGH16), F32).astype(BF16)
            acc[pl.ds(r, ts), :] = jnp.broadcast_to(bd_ref[...], (ts, d))
            return 0
        lax.fori_loop(0, nsub, load, 0)

        @pl.when(u + 1 < pl.num_programs(0))
        def _():
            start_loads(jnp.minimum(u + 1, pl.num_programs(0) - 1))

    @pl.when(nsub > 0)
    def _():
        wg_b[...] = wg_ref[...].astype(BF16)
        wu_b[...] = wu_ref[...].astype(BF16)
        wd_b[...] = wd_ref[...].astype(BF16)

        def ffn_rows(r, m):
            x = xb[pl.ds(r, m), :]
            g = jnp.minimum(_dot(x, wg_b[...]) + bg_ref[...], SWIGLU_LIMIT)
            up = jnp.clip(_dot(x, wu_b[...]) + bu_ref[...], -SWIGLU_LIMIT, SWIGLU_LIMIT)
            act = ((up + 1.0) * g * _sigmoid(SWIGLU_ALPHA * g)).astype(BF16)
            acc[pl.ds(r, m), :] += _dot(act, wd_b[...])

        group = EXP_GROUP

        def trip(p, _):
            ffn_rows(pl.multiple_of(p * (group * ts), ts), group * ts)
            return 0
        lax.fori_loop(0, nsub // group, trip, 0)
        for k in range(1, group):
            @pl.when(nsub % group == k)
            def _():
                ffn_rows(pl.multiple_of(nsub // group * (group * ts), ts), k * ts)

    n_stage = ystage.shape[0]

    def out_copy(s):
        return pltpu.make_async_copy(
            ystage.at[s % n_stage],
            ys_ref.at[pl.ds((row0 + s * ts) * pack_rows, ts * pack_rows), :],
            sem_out.at[s % n_stage])

    @pl.when((f == n_f - 1) & (nsub > 0))
    def _():
        def store(s, _):
            @pl.when(s >= n_stage)
            def _():
                out_copy(s - n_stage).wait()
            y = acc[pl.ds(pl.multiple_of(s * ts, ts), ts), :]
            lo = pltpu.bitcast(y[:, :half].astype(BF16).astype(F32), U32) >> 16
            hi = pltpu.bitcast(y[:, half:].astype(BF16).astype(F32), U32) & jnp.uint32(HIGH16)
            word = lo | hi
            for c in range(pack_rows):
                ystage[s % n_stage, pl.ds(c, ts, stride=pack_rows), :] = (
                    word[:, c * LANES:(c + 1) * LANES])
            out_copy(s).start()
            return 0
        lax.fori_loop(0, nsub, store, 0)
        for k in range(n_stage, 0, -1):
            @pl.when(nsub >= k)
            def _():
                out_copy(nsub - k).wait()


def _experts(unit_e, unit_row0, unit_nsub, used_rows, xs, w_gate_up, b_gate_up, w_down, b_down,
             n_slots, pack_rows):
    n_exp, d, two_de = w_gate_up.shape
    de = two_de // 2
    tf = EXP_FTILE
    n_f = de // tf
    n_units = unit_e.shape[0]
    last_f = n_f - 1

    def fidx(u, f, un):
        return jnp.where(un[u] > 0, f, last_f)

    grid_spec = pltpu.PrefetchScalarGridSpec(
        num_scalar_prefetch=4,
        grid=(n_units, n_f),
        in_specs=[
            pl.BlockSpec(memory_space=pl.ANY),
            pl.BlockSpec((None, d, tf), lambda u, f, ue, ur, un, us: (ue[u], 0, fidx(u, f, un))),
            pl.BlockSpec((None, d, tf),
                         lambda u, f, ue, ur, un, us: (ue[u], 0, n_f + fidx(u, f, un))),
            pl.BlockSpec((None, 1, tf), lambda u, f, ue, ur, un, us: (ue[u], 0, fidx(u, f, un))),
            pl.BlockSpec((None, 1, tf),
                         lambda u, f, ue, ur, un, us: (ue[u], 0, n_f + fidx(u, f, un))),
            pl.BlockSpec((None, tf, d), lambda u, f, ue, ur, un, us: (ue[u], fidx(u, f, un), 0)),
            pl.BlockSpec((None, 1, d), lambda u, f, ue, ur, un, us: (ue[u], 0, 0)),
        ],
        out_specs=pl.BlockSpec(memory_space=pl.ANY),
        scratch_shapes=[
            pltpu.VMEM((EXP_UNIT // EXP_SUB, EXP_SUB * pack_rows, LANES), U32),
            pltpu.VMEM((EXP_UNIT, d), BF16),
            pltpu.VMEM((EXP_UNIT, d), F32),
            pltpu.VMEM((d, tf), BF16),
            pltpu.VMEM((d, tf), BF16),
            pltpu.VMEM((tf, d), BF16),
            pltpu.VMEM((EXP_OUT_STAGES, EXP_SUB * pack_rows, LANES), U32),
            pltpu.SemaphoreType.DMA((EXP_UNIT // EXP_SUB,)),
            pltpu.SemaphoreType.DMA((EXP_OUT_STAGES,)),
        ],
    )
    return pl.pallas_call(
        functools.partial(_expert_kernel, pack_rows=pack_rows, n_slots=n_slots),
        grid_spec=grid_spec,
        out_shape=jax.ShapeDtypeStruct((n_slots * pack_rows, LANES), U32),
        compiler_params=_cparams(2, vmem=EXPERT_VMEM_LIMIT),
        name="experts",
    )(unit_e, unit_row0, unit_nsub, used_rows, xs, w_gate_up, w_gate_up,
      b_gate_up.reshape(n_exp, 1, two_de), b_gate_up.reshape(n_exp, 1, two_de),
      w_down, b_down.reshape(n_exp, 1, d))


def _combine_kernel(slot_ref, gate_ref, h1_ref, ys_ref, o_ref, ybuf, sem, *, pack_rows):
    tc = COMB_ROWS
    half = h1_ref.shape[1] // 2

    def copy(tok, j):
        s = slot_ref[0, tok * TOP_K + j]
        dst = pl.multiple_of(tok * pack_rows, pack_rows)
        return pltpu.make_async_copy(
            ys_ref.at[pl.ds(s * pack_rows, pack_rows), :],
            ybuf.at[j, pl.ds(dst, pack_rows), :], sem)

    def start(tok, _):
        for j in range(TOP_K):
            copy(tok, j).start()
        return 0

    lax.fori_loop(0, tc, start, 0)
    for j in range(TOP_K):
        pltpu.make_async_copy(ys_ref.at[pl.ds(0, tc * pack_rows), :], ybuf.at[j], sem).wait()

    gates = gate_ref[...]
    for c in range(pack_rows):
        lo = h1_ref[:, c * LANES:(c + 1) * LANES]
        hi = h1_ref[:, half + c * LANES:half + (c + 1) * LANES]
        for j in range(TOP_K):
            word = ybuf[j, pl.ds(c, tc, stride=pack_rows), :]
            gate = gates[:, j:j + 1]
            lo = lo + gate * pltpu.bitcast(word << 16, F32)
            hi = hi + gate * pltpu.bitcast(word & jnp.uint32(HIGH16), F32)
        o_ref[:, c * LANES:(c + 1) * LANES] = lo
        o_ref[:, half + c * LANES:half + (c + 1) * LANES] = hi


def _combine(slots, gates, h1, ys, bsz, seq, lp, d):
    tc = COMB_ROWS
    rows = slots.shape[0]
    pack_rows = d // 2 // LANES
    slots3 = slots.reshape(rows // tc, 1, tc * TOP_K)
    nb = lp // tc
    first = (lp - seq) // tc
    return pl.pallas_call(
        functools.partial(_combine_kernel, pack_rows=pack_rows),
        grid=(bsz, seq // tc),
        in_specs=[
            pl.BlockSpec((None, 1, tc * TOP_K), lambda b, i: (b * nb + first + i, 0, 0),
                         memory_space=pltpu.SMEM),
            pl.BlockSpec((tc, LANES), lambda b, i: (b * nb + first + i, 0)),
            pl.BlockSpec((tc, d), lambda b, i: (b * nb + first + i, 0)),
            pl.BlockSpec(memory_space=pl.ANY),
        ],
        out_specs=pl.BlockSpec((None, tc, d), lambda b, i: (b, i, 0)),
        out_shape=jax.ShapeDtypeStruct((bsz, seq, d), F32),
        scratch_shapes=[pltpu.VMEM((TOP_K, tc * pack_rows, LANES), U32), pltpu.SemaphoreType.DMA(())],
        compiler_params=_cparams(2),
        name="combine",
    )(slots3, gates, h1, ys)


def _pick(n, candidates):
    for c in candidates:
        if n % c == 0:
            return c
    raise ValueError(f"no block size in {candidates} divides {n}")


def _plan(info, cnt, n_exp, rows, lp, front, n_assign):
    counts = cnt[0, :n_exp].astype(I32)
    padded = (counts + EXP_SUB - 1) // EXP_SUB * EXP_SUB
    pstart = jnp.cumsum(padded) - padded
    eid = info[:, :TOP_K]
    rank = info[:, TOP_K:2 * TOP_K]
    onehot = eid[:, :, None] == jnp.arange(n_exp, dtype=I32)[None, None, :]
    slot = rank + jnp.sum(jnp.where(onehot, pstart[None, None, :], 0), axis=-1)
    row_valid = (jnp.arange(rows, dtype=I32) % lp) >= front
    slots = jnp.where(row_valid[:, None], slot, -1).astype(I32)

    n_slots = (n_assign + n_exp * (EXP_SUB - 1)) // EXP_SUB * EXP_SUB
    units_per_e = (padded + EXP_UNIT - 1) // EXP_UNIT
    cum_units = jnp.cumsum(units_per_e)
    n_units = n_slots // EXP_UNIT + n_exp
    uidx = jnp.arange(n_units, dtype=I32)
    ue = jnp.sum(cum_units[None, :] <= uidx[:, None], axis=1).astype(I32)
    live = ue < n_exp
    last_e = jnp.max(jnp.where(counts > 0, jnp.arange(n_exp, dtype=I32), 0))
    ue = jnp.where(live, ue, last_e)
    k_in_e = uidx - (cum_units - units_per_e)[ue]
    nsub_e = (padded // EXP_SUB)[ue]
    units_e = jnp.maximum(units_per_e[ue], 1)
    base = nsub_e // units_e
    extra = nsub_e - base * units_e
    first_sub = k_in_e * base + jnp.minimum(k_in_e, extra)
    unit_row0 = jnp.where(live, pstart[ue] + first_sub * EXP_SUB, 0).astype(I32)
    unit_nsub = jnp.where(live, base + (k_in_e < extra), 0).astype(I32)
    used_rows = jnp.sum(padded).astype(I32)[None]
    return slots, (pstart.astype(I32), padded.astype(I32)), (ue, unit_row0, unit_nsub, used_rows), n_slots


def kernel(x, meta_tokens, mix_norm_w, w_in, conv_w, a_log, dt_bias, gdn_norm_w, sb_q_norm_w,
           sb_k_norm_w, sb_out_norm_w, w_out, ffn_norm_w, router_w, router_b, w_gate_up, b_gate_up,
           w_down, b_down):
    bsz, seq, d = x.shape
    n_meta = meta_tokens.shape[0]
    depth = mix_norm_w.shape[0]
    heads = a_log.shape[1]
    hd = gdn_norm_w.shape[1]
    gd = heads * hd
    sbd = (w_in.shape[2] - 4 * gd - 2 * heads) // 3
    sb_heads = sbd // hd
    n_exp = router_w.shape[2]
    assert seq % ROW_ALIGN == 0 and d % (2 * 8 * LANES) == 0 and hd == LANES
    assert 2 * heads <= LANES and n_exp <= LANES and sb_heads == heads
    assert depth == 1, "a second layer would need the meta rows carried through the combine stage"
    front = (-n_meta) % ROW_ALIGN
    lp = front + n_meta + seq
    rows = bsz * lp
    pack_rows = d // 2 // LANES

    h = jnp.concatenate([
        jnp.zeros((bsz, front, d), x.dtype),
        jnp.broadcast_to(meta_tokens.astype(x.dtype)[None], (bsz, n_meta, d)),
        x], axis=1).reshape(rows, d)

    wl = w_in[0]
    n_ba = 4 * gd
    w_main = jnp.concatenate([wl[:, :n_ba], wl[:, n_ba + 2 * heads:]], axis=1).astype(BF16)
    w_ba = jnp.pad(wl[:, n_ba:n_ba + 2 * heads], ((0, 0), (0, LANES - 2 * heads))).astype(BF16)
    tm = _pick(rows, (1024, 512, 256))
    tn = _pick(w_main.shape[1], (1024, 512, 256, 128))
    proj, ba = _in_proj(h, mix_norm_w[0][None], w_main, w_ba, tm, tn)
    proj3 = proj.reshape(bsz, lp, -1)
    ba3 = ba.reshape(bsz, lp, LANES)

    gparams = jnp.zeros((2, LANES), F32)
    gparams = gparams.at[0, heads:2 * heads].set(a_log[0].astype(F32))
    gparams = gparams.at[1, heads:2 * heads].set(dt_bias[0].astype(F32))
    qkv3, bg3 = _gdn_prep(proj3, ba3, conv_w[0].astype(F32), gparams, heads, hd)
    o_gdn = _gdn(qkv3, proj3, bg3, gdn_norm_w[0][None], heads, hd)
    o_sb = _sb(proj3, sb_q_norm_w[0][None], sb_k_norm_w[0][None], sb_out_norm_w[0][None],
               sb_heads, hd, 4 * gd, front)

    rw = jnp.pad(router_w[0].astype(F32), ((0, 0), (0, LANES - n_exp)))
    rb = jnp.pad(router_b[0].astype(F32), (0, LANES - n_exp), constant_values=-1e30)[None]
    h1, hnp, info, gates, cnt = _mix(
        o_gdn.reshape(rows, gd), o_sb.reshape(rows, sbd), h, w_out[0].astype(BF16),
        ffn_norm_w[0][None], rw, rb, front, lp)

    slots, (pstart, padded), units, n_slots = _plan(
        info, cnt, n_exp, rows, lp, front, bsz * (n_meta + seq) * TOP_K)
    xs = _dispatch(slots, hnp, pstart, padded, units[3], n_slots, pack_rows, front, lp)
    ys = _experts(*units, xs, w_gate_up[0], b_gate_up[0], w_down[0], b_down[0], n_slots, pack_rows)
    return _combine(slots, gates, h1, ys, bsz, seq, lp, d)
```

```python
import functools

import jax
import jax.numpy as jnp
from jax import lax
from jax.experimental import pallas as pl
from jax.experimental.pallas import tpu as pltpu

F32 = jnp.float32
BF16 = jnp.bfloat16
U32 = jnp.uint32
I32 = jnp.int32

NORM_EPS = 1e-6
TOP_K = 4
SWIGLU_LIMIT = 7.0
SWIGLU_ALPHA = 1.702
LANES = 128
ROW_ALIGN = 256
GDN_PREP_ROWS = 256
GDN_CHUNK = 64
GDN_STEP_CHUNKS = 2
SB_BLOCK = 256
SB_GROUP = 4
MIX_ROWS = 512
DISPATCH_ROWS = 256
EXP_SUB = 256
EXP_GROUP = 3
EXP_UNIT = 9 * EXP_SUB
EXP_FTILE = 256
EXP_OUT_STAGES = 4
COMB_ROWS = 256
VMEM_LIMIT = 56 * 1024 * 1024
EXPERT_VMEM_LIMIT = 60 * 1024 * 1024
HIGH16 = 0xFFFF0000
SIGN_BIT = 0x80000000
LOG2E = 1.4426950408889634


def _cparams(n_grid, vmem=VMEM_LIMIT):
    return pltpu.CompilerParams(dimension_semantics=("arbitrary",) * n_grid, vmem_limit_bytes=vmem)


def _dot(a, b):
    return jnp.dot(a, b, preferred_element_type=F32)


def _dot_nt(a, b):
    return lax.dot_general(a, b, (((1,), (1,)), ((), ())), preferred_element_type=F32)


def _dot_tn(a, b):
    return lax.dot_general(a, b, (((0,), (0,)), ((), ())), preferred_element_type=F32)


def _split3(x):
    hi = x.astype(BF16)
    r1 = x - hi.astype(F32)
    mid = r1.astype(BF16)
    lo = (r1 - mid.astype(F32)).astype(BF16)
    return hi, mid, lo


def _dot3(a, b):
    ah = a.astype(BF16)
    al = (a - ah.astype(F32)).astype(BF16)
    bh = b.astype(BF16)
    bl = (b - bh.astype(F32)).astype(BF16)
    return _dot(ah, bh) + (_dot(ah, bl) + _dot(al, bh))


def _sigmoid(x):
    return 1.0 / (1.0 + jnp.exp(-x))


def _softplus(x):
    return jnp.maximum(x, 0.0) + jnp.log1p(jnp.exp(-jnp.abs(x)))


def _inproj_kernel(h_ref, nw_ref, w_ref, wba_ref, o_ref, ba_ref, xn_ref):
    @pl.when(pl.program_id(1) == 0)
    def _():
        x = h_ref[...]
        xn = x * lax.rsqrt(jnp.mean(x * x, axis=-1, keepdims=True) + NORM_EPS) * nw_ref[...]
        xn = xn.astype(BF16)
        xn_ref[...] = xn
        ba_ref[...] = _dot(xn, wba_ref[...])

    o_ref[...] = _dot(xn_ref[...], w_ref[...]).astype(o_ref.dtype)


def _in_proj(h0, norm_w, w_main, w_ba, tm, tn):
    rows, d = h0.shape
    n_main = w_main.shape[1]
    return pl.pallas_call(
        _inproj_kernel,
        grid=(rows // tm, n_main // tn),
        in_specs=[
            pl.BlockSpec((tm, d), lambda i, n: (i, 0)),
            pl.BlockSpec((1, d), lambda i, n: (0, 0)),
            pl.BlockSpec((d, tn), lambda i, n: (0, n)),
            pl.BlockSpec((d, LANES), lambda i, n: (0, 0)),
        ],
        out_specs=[
            pl.BlockSpec((tm, tn), lambda i, n: (i, n)),
            pl.BlockSpec((tm, LANES), lambda i, n: (i, 0)),
        ],
        out_shape=[
            jax.ShapeDtypeStruct((rows, n_main), BF16),
            jax.ShapeDtypeStruct((rows, LANES), F32),
        ],
        scratch_shapes=[pltpu.VMEM((tm, d), BF16)],
        compiler_params=_cparams(2),
        name="in_proj",
    )(h0, norm_w, w_main, w_ba)


def _gdn_prep_kernel(x_ref, prev_ref, ba_ref, cw_ref, gp_ref, o_ref, bg_ref, *, heads, hd):
    tp = x_ref.shape[0]
    gd = heads * hd
    taps = cw_ref.shape[0]
    x = x_ref[...].astype(F32)
    prev = jnp.where(pl.program_id(1) > 0, prev_ref[8:16, :].astype(F32), 0.0)
    xs = jnp.concatenate([prev, x], axis=0)
    y = cw_ref[taps - 1:taps, :] * x
    for s in range(1, taps):
        y = y + cw_ref[taps - 1 - s:taps - s, :] * pltpu.roll(xs, s, axis=0)[8:8 + tp, :]
    y = y * _sigmoid(y)
    for h in range(heads):
        q = y[:, h * hd:(h + 1) * hd]
        k = y[:, gd + h * hd:gd + (h + 1) * hd]
        q = q * (lax.rsqrt(jnp.sum(q * q, axis=-1, keepdims=True) + NORM_EPS) * (hd ** -0.5))
        k = k * lax.rsqrt(jnp.sum(k * k, axis=-1, keepdims=True) + NORM_EPS)
        o_ref[:, h * hd:(h + 1) * hd] = q.astype(o_ref.dtype)
        o_ref[:, gd + h * hd:gd + (h + 1) * hd] = k.astype(o_ref.dtype)
    o_ref[:, 2 * gd:] = y[:, 2 * gd:].astype(o_ref.dtype)
    ba = ba_ref[...]
    lane = lax.broadcasted_iota(I32, ba.shape, 1)
    decay = -jnp.exp(gp_ref[0:1, :]) * _softplus(ba + gp_ref[1:2, :])
    bg_ref[...] = jnp.where(lane < heads, _sigmoid(ba), decay)


def _gdn_prep(proj3, ba3, conv_w, gparams, heads, hd):
    bsz, lp, _ = proj3.shape
    gd = heads * hd
    tp = GDN_PREP_ROWS
    return pl.pallas_call(
        functools.partial(_gdn_prep_kernel, heads=heads, hd=hd),
        grid=(bsz, lp // tp),
        in_specs=[
            pl.BlockSpec((None, tp, 3 * gd), lambda b, i: (b, i, 0)),
            pl.BlockSpec((None, 16, 3 * gd), lambda b, i: (b, jnp.maximum(i * (tp // 16) - 1, 0), 0)),
            pl.BlockSpec((None, tp, LANES), lambda b, i: (b, i, 0)),
            pl.BlockSpec(conv_w.shape, lambda b, i: (0, 0)),
            pl.BlockSpec((2, LANES), lambda b, i: (0, 0)),
        ],
        out_specs=[
            pl.BlockSpec((None, tp, 3 * gd), lambda b, i: (b, i, 0)),
            pl.BlockSpec((None, tp, LANES), lambda b, i: (b, i, 0)),
        ],
        out_shape=[
            jax.ShapeDtypeStruct((bsz, lp, 3 * gd), BF16),
            jax.ShapeDtypeStruct((bsz, lp, LANES), F32),
        ],
        compiler_params=_cparams(2),
        name="gdn_prep",
    )(proj3, proj3, ba3, conv_w, gparams)


def _gdn_kernel(qkv_ref, z_ref, bg_ref, nw_ref, o_ref, state, *, heads, hd):
    c_rows = GDN_CHUNK
    gd = heads * hd

    @pl.when(pl.program_id(1) == 0)
    def _():
        state[...] = jnp.zeros_like(state)

    n_chunks = qkv_ref.shape[0] // c_rows
    r_i = lax.broadcasted_iota(I32, (c_rows, c_rows), 0)
    c_i = lax.broadcasted_iota(I32, (c_rows, c_rows), 1)
    incl = r_i >= c_i
    strict = r_i > c_i
    tri = incl.astype(BF16)
    n_sel = -(-heads // 8) * 8
    pick = (lax.broadcasted_iota(I32, (n_sel, LANES), 1)
            == lax.broadcasted_iota(I32, (n_sel, LANES), 0) + heads).astype(BF16)

    rows = [slice(j * c_rows, (j + 1) * c_rows) for j in range(n_chunks)]
    bg, gcum, gcum_rows = [], [], []
    for j in range(n_chunks):
        bg.append(bg_ref[rows[j], :])
        g_parts = _split3(bg[j])
        gcum.append(_dot(tri, g_parts[0]) + _dot(tri, g_parts[1]) + _dot(tri, g_parts[2]))
        c_parts = _split3(gcum[j])
        gcum_rows.append(_dot_nt(pick, c_parts[0]) + _dot_nt(pick, c_parts[1])
                         + _dot_nt(pick, c_parts[2]))

    units = [(j, h) for j in range(n_chunks) for h in range(heads)]
    us = range(len(units))
    q16 = [qkv_ref[rows[j], h * hd:(h + 1) * hd] for j, h in units]
    k16 = [qkv_ref[rows[j], gd + h * hd:gd + (h + 1) * hd] for j, h in units]
    k = [k16[u].astype(F32) for u in us]
    gc = [gcum[j][:, heads + h:heads + h + 1] for j, h in units]
    decay = [jnp.exp(jnp.minimum(gc[u] - gcum_rows[j][h:h + 1, :], 0.0))
             for u, (j, h) in enumerate(units)]
    kb = [k[u] * bg[j][:, h:h + 1] for u, (j, h) in enumerate(units)]
    p = [jnp.where(strict, _dot_nt(kb[u].astype(BF16), k16[u]) * decay[u], 0.0) for u in us]
    p = [(-p[u]).astype(BF16) for u in us]
    sol = [jnp.concatenate(
        [qkv_ref[rows[j], 2 * gd + h * hd:2 * gd + (h + 1) * hd].astype(F32) * bg[j][:, h:h + 1],
         kb[u] * jnp.exp(gc[u])], axis=1) for u, (j, h) in enumerate(units)]
    n_fac = c_rows.bit_length() - 1
    for i in range(n_fac):
        sol = [sol[u] + _dot(p[u], sol[u].astype(BF16)) for u in us]
        if i + 1 < n_fac:
            p = [_dot(p[u], p[u]).astype(BF16) for u in us]
    attn = [jnp.where(incl, _dot_nt(q16[u], k16[u]) * decay[u], 0.0).astype(BF16) for u in us]
    q_dec = [(q16[u].astype(F32) * jnp.exp(gc[u])).astype(BF16) for u in us]
    g_last = [gc[u][c_rows - 1:c_rows, :] for u in us]
    k_dec = [(k[u] * jnp.exp(g_last[u] - gc[u])).astype(BF16) for u in us]

    s_cur = [state[h] for h in range(heads)]
    for j in range(n_chunks):
        ids = [j * heads + h for h in range(heads)]
        s_b = [s.astype(BF16) for s in s_cur]
        v_new = [(sol[u][:, :hd] - _dot(sol[u][:, hd:].astype(BF16), s_b[h])).astype(BF16)
                 for h, u in enumerate(ids)]
        o = [_dot(q_dec[u], s_b[h]) + _dot(attn[u], v_new[h]) for h, u in enumerate(ids)]
        s_cur = [s_cur[h] * jnp.exp(g_last[u]) + _dot_tn(k_dec[u], v_new[h])
                 for h, u in enumerate(ids)]
        for h in range(heads):
            zf = z_ref[rows[j], h * hd:(h + 1) * hd].astype(F32)
            o_n = (o[h] * lax.rsqrt(jnp.mean(o[h] * o[h], axis=-1, keepdims=True) + NORM_EPS)
                   * nw_ref[...])
            o_ref[rows[j], h * hd:(h + 1) * hd] = (o_n * (zf * _sigmoid(zf))).astype(o_ref.dtype)
    for h in range(heads):
        state[h] = s_cur[h]


def _gdn(qkv3, proj3, bg3, norm_w, heads, hd):
    bsz, lp, _ = qkv3.shape
    gd = heads * hd
    c = GDN_CHUNK * GDN_STEP_CHUNKS
    return pl.pallas_call(
        functools.partial(_gdn_kernel, heads=heads, hd=hd),
        grid=(bsz, lp // c),
        in_specs=[
            pl.BlockSpec((None, c, 3 * gd), lambda b, i: (b, i, 0)),
            pl.BlockSpec((None, c, gd), lambda b, i: (b, i, 3)),
            pl.BlockSpec((None, c, LANES), lambda b, i: (b, i, 0)),
            pl.BlockSpec((1, hd), lambda b, i: (0, 0)),
        ],
        out_specs=pl.BlockSpec((None, c, gd), lambda b, i: (b, i, 0)),
        out_shape=jax.ShapeDtypeStruct((bsz, lp, gd), BF16),
        scratch_shapes=[pltpu.VMEM((heads, hd, hd), F32)],
        compiler_params=_cparams(2),
        name="gdn",
    )(qkv3, proj3, bg3, norm_w)


def _sb_kernel(q_ref, k_ref, v_ref, qw_ref, kw_ref, ow_ref, tri_ref, o_ref, kn_ref, acc_ref,
               drop_ref, *, front, hd):
    t = SB_BLOCK
    qi = pl.program_id(2)
    n_blocks = k_ref.shape[0] // t

    @pl.when(qi == 0)
    def _():
        def norm_keys(i, _):
            kb = k_ref[pl.ds(i * t, t), :].astype(F32)
            kn = kb * lax.rsqrt(jnp.mean(kb * kb, axis=-1, keepdims=True) + NORM_EPS) * kw_ref[...]
            kn_ref[pl.ds(i * t, t), :] = kn.astype(BF16)
            return 0
        lax.fori_loop(0, n_blocks, norm_keys, 0)

    q = q_ref[...].astype(F32)
    qn = q * lax.rsqrt(jnp.mean(q * q, axis=-1, keepdims=True) + NORM_EPS) * qw_ref[...]
    qn = (qn * (hd ** -0.5 * LOG2E)).astype(BF16)

    acc_ref[...] = jnp.zeros_like(acc_ref)
    drop_ref[...] = jnp.zeros_like(drop_ref)

    def mask_of(kind):
        r_i = lax.broadcasted_iota(I32, (t, t), 0)
        c_i = lax.broadcasted_iota(I32, (t, t), 1)
        if kind == "causal":
            return c_i < r_i
        if kind == "front":
            return c_i >= front
        return (c_i < r_i) & (c_i >= front)

    def scores(kj, kind):
        start = pl.multiple_of(kj * t, t)
        s = _dot_nt(qn, kn_ref[pl.ds(start, t), :])
        neg_abs = pltpu.bitcast(pltpu.bitcast(s, U32) | jnp.uint32(SIGN_BIT), F32)
        drop = jnp.maximum(s, 0.0) + jnp.log2(1.0 + jnp.exp2(neg_abs))
        visible = None
        if kind is not None:
            visible = mask_of(kind)
            drop = jnp.where(visible, drop, 0.0)
        hi = pltpu.bitcast(pltpu.bitcast(drop, U32) & jnp.uint32(HIGH16), F32)
        lo = drop - hi
        tail = _dot(jnp.concatenate([hi.astype(BF16), lo.astype(BF16)], axis=1), tri_ref[...])
        return s, tail, visible

    def run(tiles):
        carry = (acc_ref[...], drop_ref[...])

        def finish(tile, part, carry):
            (kj, kind), (s, tail, visible), (acc, dropped) = tile, part, carry
            weight = jnp.exp2(s - tail - jnp.concatenate([dropped] * (t // LANES), axis=1))
            if kind is not None:
                weight = jnp.where(visible, weight, 0.0)
            start = pl.multiple_of(kj * t, t)
            acc = acc + _dot(weight.astype(BF16), v_ref[pl.ds(start, t), :])
            return acc, dropped + jnp.broadcast_to(tail[:, 0:1], dropped.shape)

        pending = None
        for tile in tiles:
            part = scores(*tile)
            if pending is not None:
                carry = finish(*pending, carry)
            pending = (tile, part)
        acc_ref[...], drop_ref[...] = finish(*pending, carry)

    @pl.when(qi == 0)
    def _():
        run([(0, "causal_front")])

    @pl.when(qi == 1)
    def _():
        run([(1, "causal"), (0, "front")])

    @pl.when(qi >= 2)
    def _():
        run([(qi, "causal"), (qi - 1, None)])
        rest = qi - 2
        group = SB_GROUP

        def many(p, _):
            top = qi - 2 - group * p
            run([(top - k, None) for k in range(group)])
            return 0
        lax.fori_loop(0, rest // group, many, 0)
        for left in range(group):
            @pl.when(rest % group == left)
            def _():
                run([(k, None) for k in range(left, 0, -1)] + [(0, "front")])

    o = acc_ref[...]
    o = o * lax.rsqrt(jnp.mean(o * o, axis=-1, keepdims=True) + NORM_EPS) * ow_ref[...]
    o_ref[...] = o.astype(o_ref.dtype)


def _sb(proj3, qw, kw, ow, heads, hd, col0, front):
    bsz, lp, _ = proj3.shape
    t = SB_BLOCK
    cb = col0 // hd
    tri = (jnp.arange(t)[:, None] >= jnp.arange(t)[None, :])
    tri = jnp.concatenate([tri, tri], axis=0).astype(BF16)
    return pl.pallas_call(
        functools.partial(_sb_kernel, front=front, hd=hd),
        grid=(bsz, heads, lp // t),
        in_specs=[
            pl.BlockSpec((None, t, hd), lambda b, h, i: (b, i, cb + h)),
            pl.BlockSpec((None, lp, hd), lambda b, h, i: (b, 0, cb + heads + h)),
            pl.BlockSpec((None, lp, hd), lambda b, h, i: (b, 0, cb + 2 * heads + h)),
            pl.BlockSpec((1, hd), lambda b, h, i: (0, 0)),
            pl.BlockSpec((1, hd), lambda b, h, i: (0, 0)),
            pl.BlockSpec((1, hd), lambda b, h, i: (0, 0)),
            pl.BlockSpec(tri.shape, lambda b, h, i: (0, 0)),
        ],
        out_specs=pl.BlockSpec((None, t, hd), lambda b, h, i: (b, i, h)),
        out_shape=jax.ShapeDtypeStruct((bsz, lp, heads * hd), BF16),
        scratch_shapes=[pltpu.VMEM((lp, hd), BF16), pltpu.VMEM((t, hd), F32),
                        pltpu.VMEM((t, LANES), F32)],
        compiler_params=_cparams(3),
        name="sb",
    )(proj3, proj3, proj3, qw, kw, ow, tri)


def _sbk_kernel(q_ref, k_ref, v_ref, qw_ref, kw_ref, ow_ref, tri_ref, o_ref, qn_ref, kn_ref,
                acc_ref, drop_ref, *, front, hd):
    t = SB_BLOCK
    nb = q_ref.shape[0] // t

    def prep(i, _):
        rows = pl.ds(pl.multiple_of(i * t, t), t)
        qb = q_ref[rows, :].astype(F32)
        kb = k_ref[rows, :].astype(F32)
        qn = qb * lax.rsqrt(jnp.mean(qb * qb, axis=-1, keepdims=True) + NORM_EPS) * qw_ref[...]
        qn_ref[rows, :] = (qn * (hd ** -0.5 * LOG2E)).astype(BF16)
        kn = kb * lax.rsqrt(jnp.mean(kb * kb, axis=-1, keepdims=True) + NORM_EPS) * kw_ref[...]
        kn_ref[rows, :] = kn.astype(BF16)
        acc_ref[rows, :] = jnp.zeros((t, hd), F32)
        drop_ref[rows, :] = jnp.zeros((t, LANES), F32)
        return 0
    lax.fori_loop(0, nb, prep, 0)

    def mask_of(kind):
        r_i = lax.broadcasted_iota(I32, (t, t), 0)
        c_i = lax.broadcasted_iota(I32, (t, t), 1)
        if kind == "causal":
            return c_i < r_i
        if kind == "front":
            return c_i >= front
        return (c_i < r_i) & (c_i >= front)

    def scores(qi, kj, kind):
        qrows = pl.ds(pl.multiple_of(qi * t, t), t)
        krows = pl.ds(pl.multiple_of(kj * t, t), t)
        s = _dot_nt(qn_ref[qrows, :], kn_ref[krows, :])
        neg_abs = pltpu.bitcast(pltpu.bitcast(s, U32) | jnp.uint32(SIGN_BIT), F32)
        drop = jnp.maximum(s, 0.0) + jnp.log2(1.0 + jnp.exp2(neg_abs))
        visible = None
        if kind is not None:
            visible = mask_of(kind)
            drop = jnp.where(visible, drop, 0.0)
        hi = pltpu.bitcast(pltpu.bitcast(drop, U32) & jnp.uint32(HIGH16), F32)
        lo = drop - hi
        tail = _dot(jnp.concatenate([hi.astype(BF16), lo.astype(BF16)], axis=1), tri_ref[...])
        return s, tail, visible

    def finish(tile, part):
        (qi, kj, kind), (s, tail, visible) = tile, part
        qrows = pl.ds(pl.multiple_of(qi * t, t), t)
        krows = pl.ds(pl.multiple_of(kj * t, t), t)
        dropped = drop_ref[qrows, :]
        weight = jnp.exp2(s - tail - jnp.concatenate([dropped] * (t // LANES), axis=1))
        if kind is not None:
            weight = jnp.where(visible, weight, 0.0)
        acc_ref[qrows, :] += _dot(weight.astype(BF16), v_ref[krows, :])
        drop_ref[qrows, :] = dropped + jnp.broadcast_to(tail[:, 0:1], dropped.shape)

    def run(tiles):
        pending = None
        for tile in tiles:
            part = scores(*tile)
            if pending is not None:
                finish(*pending)
            pending = (tile, part)
        finish(*pending)

    group = SB_GROUP

    def key_block(kj, diag_kind, plain_kind):
        n_tiles = nb - kj
        n_head = (n_tiles - 1) % group + 1
        for h in range(1, group + 1):
            @pl.when(n_head == h)
            def _():
                run([(kj, kj, diag_kind)] + [(kj + m, kj, plain_kind) for m in range(1, h)])

        def many(p, _):
            base = kj + n_head + group * p
            run([(base + m, kj, plain_kind) for m in range(group)])
            return 0
        lax.fori_loop(0, (n_tiles - n_head) // group, many, 0)

    def later_blocks(i, _):
        key_block(nb - 1 - i, "causal", None)
        return 0
    lax.fori_loop(0, nb - 1, later_blocks, 0)
    key_block(jnp.int32(0), "causal_front", "front")

    def write(i, _):
        rows = pl.ds(pl.multiple_of(i * t, t), t)
        o = acc_ref[rows, :]
        o = o * lax.rsqrt(jnp.mean(o * o, axis=-1, keepdims=True) + NORM_EPS) * ow_ref[...]
        o_ref[rows, :] = o.astype(o_ref.dtype)
        return 0
    lax.fori_loop(0, nb, write, 0)


def _sbk(proj3, qw, kw, ow, heads, hd, col0, front):
    bsz, lp, _ = proj3.shape
    t = SB_BLOCK
    cb = col0 // hd
    tri = (jnp.arange(t)[:, None] >= jnp.arange(t)[None, :])
    tri = jnp.concatenate([tri, tri], axis=0).astype(BF16)
    return pl.pallas_call(
        functools.partial(_sbk_kernel, front=front, hd=hd),
        grid=(bsz, heads),
        in_specs=[
            pl.BlockSpec((None, lp, hd), lambda b, h: (b, 0, cb + h)),
            pl.BlockSpec((None, lp, hd), lambda b, h: (b, 0, cb + heads + h)),
            pl.BlockSpec((None, lp, hd), lambda b, h: (b, 0, cb + 2 * heads + h)),
            pl.BlockSpec((1, hd), lambda b, h: (0, 0)),
            pl.BlockSpec((1, hd), lambda b, h: (0, 0)),
            pl.BlockSpec((1, hd), lambda b, h: (0, 0)),
            pl.BlockSpec(tri.shape, lambda b, h: (0, 0)),
        ],
        out_specs=pl.BlockSpec((None, lp, hd), lambda b, h: (b, 0, h)),
        out_shape=jax.ShapeDtypeStruct((bsz, lp, heads * hd), BF16),
        scratch_shapes=[pltpu.VMEM((lp, hd), BF16), pltpu.VMEM((lp, hd), BF16),
                        pltpu.VMEM((lp, hd), F32), pltpu.VMEM((lp, LANES), F32)],
        compiler_params=_cparams(2),
        name="sb",
    )(proj3, proj3, proj3, qw, kw, ow, tri)


def _mix_kernel(og_ref, os_ref, h_ref, wo_ref, nw_ref, rw_ref, rb_ref,
                h1_ref, hnp_ref, info_ref, gate_ref, cnt_ref, cnt_acc,
                *, front, lp, bsz, pack_rows):
    tm = MIX_ROWS
    i = pl.program_id(0)

    @pl.when(i == 0)
    def _():
        cnt_acc[...] = jnp.zeros_like(cnt_acc)

    gd = og_ref.shape[1]
    h1 = h_ref[...] + _dot(og_ref[...], wo_ref[0:gd, :]) + _dot(os_ref[...], wo_ref[gd:, :])
    h1_ref[...] = h1
    hn = h1 * lax.rsqrt(jnp.mean(h1 * h1, axis=-1, keepdims=True) + NORM_EPS) * nw_ref[...]

    half = hn.shape[1] // 2
    lo = pltpu.bitcast(hn[:, :half].astype(BF16).astype(F32), U32) >> 16
    hi = pltpu.bitcast(hn[:, half:].astype(BF16).astype(F32), U32) & jnp.uint32(HIGH16)
    word = lo | hi
    for s in range(pack_rows):
        hnp_ref[pl.ds(s, tm, stride=pack_rows), :] = word[:, s * LANES:(s + 1) * LANES]

    logits = _dot3(hn, rw_ref[...]) + rb_ref[...]
    lane = lax.broadcasted_iota(I32, (tm, LANES), 1)
    row = lax.broadcasted_iota(I32, (tm, 1), 0)
    pos = i * tm + row
    valid = pos < 0
    for b in range(bsz):
        valid = valid | ((pos >= b * lp + front) & (pos < (b + 1) * lp))
    work = logits
    tops, idxs, hots = [], [], []
    for _ in range(TOP_K):
        m = jnp.max(work, axis=1, keepdims=True)
        idx = jnp.min(jnp.where(work == m, lane, LANES), axis=1, keepdims=True)
        hot = lane == idx
        tops.append(m)
        idxs.append(idx)
        hots.append(hot)
        work = jnp.where(hot, -jnp.inf, work)
    exps = [jnp.exp(m - tops[0]) for m in tops]
    denom = exps[0] + exps[1] + exps[2] + exps[3]
    sel = jnp.zeros((tm, LANES), F32)
    for hot in hots:
        sel = sel + hot.astype(F32)
    sel = jnp.where(valid, sel, 0.0)

    r_i = lax.broadcasted_iota(I32, (tm, tm), 0)
    c_i = lax.broadcasted_iota(I32, (tm, tm), 1)
    before = (r_i > c_i).astype(BF16)
    rank = cnt_acc[0:1, :] + _dot(before, sel.astype(BF16))
    info = jnp.zeros((tm, LANES), I32)
    gates = jnp.zeros((tm, LANES), F32)
    for j in range(TOP_K):
        rank_j = jnp.sum(jnp.where(hots[j], rank, 0.0), axis=1, keepdims=True).astype(I32)
        info = jnp.where(lane == j, idxs[j], info)
        info = jnp.where(lane == TOP_K + j, rank_j, info)
        gates = jnp.where(lane == j, exps[j] / denom, gates)
    info_ref[...] = info
    gate_ref[...] = gates
    cnt_acc[...] = cnt_acc[...] + jnp.sum(sel, axis=0, keepdims=True)
    cnt_ref[...] = cnt_acc[...]


def _mix(o_gdn, o_sb, h0, w_out, norm_w, router_w, router_b, front, lp):
    rows, d = h0.shape
    gd = o_gdn.shape[1]
    tm = MIX_ROWS
    pack_rows = d // 2 // LANES
    kern = functools.partial(_mix_kernel, front=front, lp=lp, bsz=rows // lp, pack_rows=pack_rows)
    return pl.pallas_call(
        kern,
        grid=(rows // tm,),
        in_specs=[
            pl.BlockSpec((tm, gd), lambda i: (i, 0)),
            pl.BlockSpec((tm, gd), lambda i: (i, 0)),
            pl.BlockSpec((tm, d), lambda i: (i, 0)),
            pl.BlockSpec(w_out.shape, lambda i: (0, 0)),
            pl.BlockSpec((1, d), lambda i: (0, 0)),
            pl.BlockSpec((d, LANES), lambda i: (0, 0)),
            pl.BlockSpec((1, LANES), lambda i: (0, 0)),
        ],
        out_specs=[
            pl.BlockSpec((tm, d), lambda i: (i, 0)),
            pl.BlockSpec((tm * pack_rows, LANES), lambda i: (i, 0)),
            pl.BlockSpec((tm, LANES), lambda i: (i, 0)),
            pl.BlockSpec((tm, LANES), lambda i: (i, 0)),
            pl.BlockSpec((8, LANES), lambda i: (0, 0)),
        ],
        out_shape=[
            jax.ShapeDtypeStruct((rows, d), F32),
            jax.ShapeDtypeStruct((rows * pack_rows, LANES), U32),
            jax.ShapeDtypeStruct((rows, LANES), I32),
            jax.ShapeDtypeStruct((rows, LANES), F32),
            jax.ShapeDtypeStruct((8, LANES), F32),
        ],
        scratch_shapes=[pltpu.VMEM((8, LANES), F32)],
        compiler_params=_cparams(1),
        name="mix_router",
    )(o_gdn, o_sb, h0, w_out, norm_w, router_w, router_b)


def _dispatch_kernel(pstart_ref, padded_ref, used_ref, slot_ref, hnp_ref, xs_ref, zbuf, sem, zsem,
                     *, pack_rows, front, blocks_per_batch, n_slots):
    tm = DISPATCH_ROWS
    ts = EXP_SUB
    i = pl.program_id(0)
    n_exp = pstart_ref.shape[0]

    @pl.when(i == 0)
    def _():
        zbuf[...] = jnp.zeros_like(zbuf)

        def zero_copy(row):
            return pltpu.make_async_copy(
                zbuf, xs_ref.at[pl.ds(row * pack_rows, ts * pack_rows), :], zsem)

        def pad_block(e, _, *, start):
            @pl.when(padded_ref[e] > 0)
            def _():
                cp = zero_copy(pstart_ref[e] + padded_ref[e] - ts)
                cp.start() if start else cp.wait()
            return 0

        used = used_ref[0]
        n_tail = (n_slots - used) // ts
        lax.fori_loop(0, n_exp, functools.partial(pad_block, start=True), 0)
        lax.fori_loop(0, n_tail, lambda j, c: (zero_copy(used + j * ts).start(), c)[1], 0)
        lax.fori_loop(0, n_exp, functools.partial(pad_block, start=False), 0)
        lax.fori_loop(0, n_tail, lambda j, c: (zero_copy(used + j * ts).wait(), c)[1], 0)

    def copy(tok, j):
        src = pl.multiple_of(tok * pack_rows, pack_rows)
        return pltpu.make_async_copy(
            hnp_ref.at[pl.ds(src, pack_rows), :],
            xs_ref.at[pl.ds(slot_ref[0, tok * TOP_K + j] * pack_rows, pack_rows), :], sem)

    def start(tok, _):
        for j in range(TOP_K):
            copy(tok, j).start(priority=j % 2)
        return 0

    def wait_all(n_tok):
        for _ in range(TOP_K):
            pltpu.make_async_copy(
                hnp_ref.at[pl.ds(0, n_tok * pack_rows), :],
                xs_ref.at[pl.ds(0, n_tok * pack_rows), :], sem).wait()

    @pl.when(i % blocks_per_batch == 0)
    def _():
        lax.fori_loop(front, tm, start, 0)
        wait_all(tm - front)

    @pl.when(i % blocks_per_batch != 0)
    def _():
        lax.fori_loop(0, tm, start, 0)
        wait_all(tm)


def _dispatch(slots, hnp, pstart, padded, used_rows, n_slots, pack_rows, front, lp):
    rows = slots.shape[0]
    tm = DISPATCH_ROWS
    slots3 = slots.reshape(rows // tm, 1, tm * TOP_K)
    grid_spec = pltpu.PrefetchScalarGridSpec(
        num_scalar_prefetch=3,
        grid=(rows // tm,),
        in_specs=[
            pl.BlockSpec((None, 1, tm * TOP_K), lambda i, *_: (i, 0, 0), memory_space=pltpu.SMEM),
            pl.BlockSpec((tm * pack_rows, LANES), lambda i, *_: (i, 0)),
        ],
        out_specs=pl.BlockSpec(memory_space=pl.ANY),
        scratch_shapes=[pltpu.VMEM((EXP_SUB * pack_rows, LANES), U32),
                        pltpu.SemaphoreType.DMA(()), pltpu.SemaphoreType.DMA(())],
    )
    return pl.pallas_call(
        functools.partial(_dispatch_kernel, pack_rows=pack_rows, front=front,
                          blocks_per_batch=lp // tm, n_slots=n_slots),
        grid_spec=grid_spec,
        out_shape=jax.ShapeDtypeStruct((n_slots * pack_rows, LANES), U32),
        compiler_params=_cparams(1),
        name="dispatch",
    )(pstart, padded, used_rows, slots3, hnp)


def _expert_kernel(ue_ref, ur_ref, un_ref, used_ref, xs_ref, wg_ref, wu_ref, bg_ref, bu_ref, wd_ref,
                   bd_ref, ys_ref, xbuf, xb, acc, wg_b, wu_b, wd_b, ystage, sem_in, sem_out,
                   *, pack_rows, n_slots):
    del ue_ref
    ts = EXP_SUB
    u = pl.program_id(0)
    f = pl.program_id(1)
    n_f = pl.num_programs(1)
    nsub = un_ref[u]
    row0 = ur_ref[u]
    d = xb.shape[1]
    half = d // 2
    tf = wg_b.shape[1]

    @pl.when((u == 0) & (f == 0))
    def _():
        ystage[...] = jnp.zeros_like(ystage)
        used = used_ref[0]

        def tail_copy(i):
            return pltpu.make_async_copy(
                ystage.at[0], ys_ref.at[pl.ds((used + i * ts) * pack_rows, ts * pack_rows), :],
                sem_out.at[0])

        n_tail = (n_slots - used) // ts
        lax.fori_loop(0, n_tail, lambda i, c: (tail_copy(i).start(), c)[1], 0)
        lax.fori_loop(0, n_tail, lambda i, c: (tail_copy(i).wait(), c)[1], 0)

    def in_copy(unit_row0, s):
        return pltpu.make_async_copy(
            xs_ref.at[pl.ds((unit_row0 + s * ts) * pack_rows, ts * pack_rows), :],
            xbuf.at[s], sem_in.at[s])

    def start_loads(unit):
        unit_row0 = ur_ref[unit]
        lax.fori_loop(0, un_ref[unit], lambda s, c: (in_copy(unit_row0, s).start(), c)[1], 0)

    @pl.when((u == 0) & (f == 0))
    def _():
        start_loads(0)

    @pl.when((f == 0) & (nsub > 0))
    def _():
        def load(s, _):
            in_copy(row0, s).wait()
            words = jnp.concatenate(
                [xbuf[s, pl.ds(c, ts, stride=pack_rows), :] for c in range(pack_rows)], axis=1)
            r = pl.multiple_of(s * ts, ts)
            xb[pl.ds(r, ts), :half] = pltpu.bitcast(words << 16, F32).astype(BF16)
            xb[pl.ds(r, ts), half:] = pltpu.bitcast(words & jnp.uint32(HIGH16), F32).astype(BF16)
            acc[pl.ds(r, ts), :] = jnp.broadcast_to(bd_ref[...], (ts, d))
            return 0
        lax.fori_loop(0, nsub, load, 0)

        @pl.when(u + 1 < pl.num_programs(0))
        def _():
            start_loads(jnp.minimum(u + 1, pl.num_programs(0) - 1))

    @pl.when(nsub > 0)
    def _():
        wg_b[...] = wg_ref[...].astype(BF16)
        wu_b[...] = wu_ref[...].astype(BF16)
        wd_b[...] = wd_ref[...].astype(BF16)

        def ffn_rows(r, m):
            x = xb[pl.ds(r, m), :]
            g = jnp.minimum(_dot(x, wg_b[...]) + bg_ref[...], SWIGLU_LIMIT)
            up = jnp.clip(_dot(x, wu_b[...]) + bu_ref[...], -SWIGLU_LIMIT, SWIGLU_LIMIT)
            act = ((up + 1.0) * g * _sigmoid(SWIGLU_ALPHA * g)).astype(BF16)
            acc[pl.ds(r, m), :] += _dot(act, wd_b[...])

        group = EXP_GROUP

        def trip(p, _):
            ffn_rows(pl.multiple_of(p * (group * ts), ts), group * ts)
            return 0
        lax.fori_loop(0, nsub // group, trip, 0)
        for k in range(1, group):
            @pl.when(nsub % group == k)
            def _():
                ffn_rows(pl.multiple_of(nsub // group * (group * ts), ts), k * ts)

    n_stage = ystage.shape[0]

    def out_copy(s):
        return pltpu.make_async_copy(
            ystage.at[s % n_stage],
            ys_ref.at[pl.ds((row0 + s * ts) * pack_rows, ts * pack_rows), :],
            sem_out.at[s % n_stage])

    @pl.when((f == n_f - 1) & (nsub > 0))
    def _():
        def store(s, _):
            @pl.when(s >= n_stage)
            def _():
                out_copy(s - n_stage).wait()
            y = acc[pl.ds(pl.multiple_of(s * ts, ts), ts), :]
            lo = pltpu.bitcast(y[:, :half].astype(BF16).astype(F32), U32) >> 16
            hi = pltpu.bitcast(y[:, half:].astype(BF16).astype(F32), U32) & jnp.uint32(HIGH16)
            word = lo | hi
            for c in range(pack_rows):
                ystage[s % n_stage, pl.ds(c, ts, stride=pack_rows), :] = (
                    word[:, c * LANES:(c + 1) * LANES])
            out_copy(s).start()
            return 0
        lax.fori_loop(0, nsub, store, 0)
        for k in range(n_stage, 0, -1):
            @pl.when(nsub >= k)
            def _():
                out_copy(nsub - k).wait()


def _experts(unit_e, unit_row0, unit_nsub, used_rows, xs, w_gate_up, b_gate_up, w_down, b_down,
             n_slots, pack_rows):
    n_exp, d, two_de = w_gate_up.shape
    de = two_de // 2
    tf = EXP_FTILE
    n_f = de // tf
    n_units = unit_e.shape[0]
    last_f = n_f - 1

    def fidx(u, f, un):
        return jnp.where(un[u] > 0, f, last_f)

    grid_spec = pltpu.PrefetchScalarGridSpec(
        num_scalar_prefetch=4,
        grid=(n_units, n_f),
        in_specs=[
            pl.BlockSpec(memory_space=pl.ANY),
            pl.BlockSpec((None, d, tf), lambda u, f, ue, ur, un, us: (ue[u], 0, fidx(u, f, un))),
            pl.BlockSpec((None, d, tf),
                         lambda u, f, ue, ur, un, us: (ue[u], 0, n_f + fidx(u, f, un))),
            pl.BlockSpec((None, 1, tf), lambda u, f, ue, ur, un, us: (ue[u], 0, fidx(u, f, un))),
            pl.BlockSpec((None, 1, tf),
                         lambda u, f, ue, ur, un, us: (ue[u], 0, n_f + fidx(u, f, un))),
            pl.BlockSpec((None, tf, d), lambda u, f, ue, ur, un, us: (ue[u], fidx(u, f, un), 0)),
            pl.BlockSpec((None, 1, d), lambda u, f, ue, ur, un, us: (ue[u], 0, 0)),
        ],
        out_specs=pl.BlockSpec(memory_space=pl.ANY),
        scratch_shapes=[
            pltpu.VMEM((EXP_UNIT // EXP_SUB, EXP_SUB * pack_rows, LANES), U32),
            pltpu.VMEM((EXP_UNIT, d), BF16),
            pltpu.VMEM((EXP_UNIT, d), F32),
            pltpu.VMEM((d, tf), BF16),
            pltpu.VMEM((d, tf), BF16),
            pltpu.VMEM((tf, d), BF16),
            pltpu.VMEM((EXP_OUT_STAGES, EXP_SUB * pack_rows, LANES), U32),
            pltpu.SemaphoreType.DMA((EXP_UNIT // EXP_SUB,)),
            pltpu.SemaphoreType.DMA((EXP_OUT_STAGES,)),
        ],
    )
    return pl.pallas_call(
        functools.partial(_expert_kernel, pack_rows=pack_rows, n_slots=n_slots),
        grid_spec=grid_spec,
        out_shape=jax.ShapeDtypeStruct((n_slots * pack_rows, LANES), U32),
        compiler_params=_cparams(2, vmem=EXPERT_VMEM_LIMIT),
        name="experts",
    )(unit_e, unit_row0, unit_nsub, used_rows, xs, w_gate_up, w_gate_up,
      b_gate_up.reshape(n_exp, 1, two_de), b_gate_up.reshape(n_exp, 1, two_de),
      w_down, b_down.reshape(n_exp, 1, d))


def _combine_kernel(slot_ref, gate_ref, h1_ref, ys_ref, o_ref, ybuf, sem, *, pack_rows):
    tc = COMB_ROWS
    half = h1_ref.shape[1] // 2

    def copy(tok, j):
        s = slot_ref[0, tok * TOP_K + j]
        dst = pl.multiple_of(tok * pack_rows, pack_rows)
        return pltpu.make_async_copy(
            ys_ref.at[pl.ds(s * pack_rows, pack_rows), :],
            ybuf.at[j, pl.ds(dst, pack_rows), :], sem)

    def start(tok, _):
        for j in range(TOP_K):
            copy(tok, j).start(priority=j % 2)
        return 0

    lax.fori_loop(0, tc, start, 0)
    for j in range(TOP_K):
        pltpu.make_async_copy(ys_ref.at[pl.ds(0, tc * pack_rows), :], ybuf.at[j], sem).wait()

    gates = gate_ref[...]
    for c in range(pack_rows):
        lo = h1_ref[:, c * LANES:(c + 1) * LANES]
        hi = h1_ref[:, half + c * LANES:half + (c + 1) * LANES]
        for j in range(TOP_K):
            word = ybuf[j, pl.ds(c, tc, stride=pack_rows), :]
            gate = gates[:, j:j + 1]
            lo = lo + gate * pltpu.bitcast(word << 16, F32)
            hi = hi + gate * pltpu.bitcast(word & jnp.uint32(HIGH16), F32)
        o_ref[:, c * LANES:(c + 1) * LANES] = lo
        o_ref[:, half + c * LANES:half + (c + 1) * LANES] = hi


def _combine(slots, gates, h1, ys, bsz, seq, lp, d):
    tc = COMB_ROWS
    rows = slots.shape[0]
    pack_rows = d // 2 // LANES
    slots3 = slots.reshape(rows // tc, 1, tc * TOP_K)
    nb = lp // tc
    first = (lp - seq) // tc
    return pl.pallas_call(
        functools.partial(_combine_kernel, pack_rows=pack_rows),
        grid=(bsz, seq // tc),
        in_specs=[
            pl.BlockSpec((None, 1, tc * TOP_K), lambda b, i: (b * nb + first + i, 0, 0),
                         memory_space=pltpu.SMEM),
            pl.BlockSpec((tc, LANES), lambda b, i: (b * nb + first + i, 0)),
            pl.BlockSpec((tc, d), lambda b, i: (b * nb + first + i, 0)),
            pl.BlockSpec(memory_space=pl.ANY),
        ],
        out_specs=pl.BlockSpec((None, tc, d), lambda b, i: (b, i, 0)),
        out_shape=jax.ShapeDtypeStruct((bsz, seq, d), F32),
        scratch_shapes=[pltpu.VMEM((TOP_K, tc * pack_rows, LANES), U32), pltpu.SemaphoreType.DMA(())],
        compiler_params=_cparams(2),
        name="combine",
    )(slots3, gates, h1, ys)


def _pick(n, candidates):
    for c in candidates:
        if n % c == 0:
            return c
    raise ValueError(f"no block size in {candidates} divides {n}")


def _plan(info, cnt, n_exp, rows, lp, front, n_assign):
    counts = cnt[0, :n_exp].astype(I32)
    padded = (counts + EXP_SUB - 1) // EXP_SUB * EXP_SUB
    pstart = jnp.cumsum(padded) - padded
    eid = info[:, :TOP_K]
    rank = info[:, TOP_K:2 * TOP_K]
    onehot = eid[:, :, None] == jnp.arange(n_exp, dtype=I32)[None, None, :]
    slot = rank + jnp.sum(jnp.where(onehot, pstart[None, None, :], 0), axis=-1)
    row_valid = (jnp.arange(rows, dtype=I32) % lp) >= front
    slots = jnp.where(row_valid[:, None], slot, -1).astype(I32)

    n_slots = (n_assign + n_exp * (EXP_SUB - 1)) // EXP_SUB * EXP_SUB
    units_per_e = (padded + EXP_UNIT - 1) // EXP_UNIT
    cum_units = jnp.cumsum(units_per_e)
    n_units = n_slots // EXP_UNIT + n_exp
    uidx = jnp.arange(n_units, dtype=I32)
    ue = jnp.sum(cum_units[None, :] <= uidx[:, None], axis=1).astype(I32)
    live = ue < n_exp
    last_e = jnp.max(jnp.where(counts > 0, jnp.arange(n_exp, dtype=I32), 0))
    ue = jnp.where(live, ue, last_e)
    k_in_e = uidx - (cum_units - units_per_e)[ue]
    nsub_e = (padded // EXP_SUB)[ue]
    units_e = jnp.maximum(units_per_e[ue], 1)
    base = nsub_e // units_e
    extra = nsub_e - base * units_e
    first_sub = k_in_e * base + jnp.minimum(k_in_e, extra)
    unit_row0 = jnp.where(live, pstart[ue] + first_sub * EXP_SUB, 0).astype(I32)
    unit_nsub = jnp.where(live, base + (k_in_e < extra), 0).astype(I32)
    used_rows = jnp.sum(padded).astype(I32)[None]
    return slots, (pstart.astype(I32), padded.astype(I32)), (ue, unit_row0, unit_nsub, used_rows), n_slots


def kernel(x, meta_tokens, mix_norm_w, w_in, conv_w, a_log, dt_bias, gdn_norm_w, sb_q_norm_w,
           sb_k_norm_w, sb_out_norm_w, w_out, ffn_norm_w, router_w, router_b, w_gate_up, b_gate_up,
           w_down, b_down):
    bsz, seq, d = x.shape
    n_meta = meta_tokens.shape[0]
    depth = mix_norm_w.shape[0]
    heads = a_log.shape[1]
    hd = gdn_norm_w.shape[1]
    gd = heads * hd
    sbd = (w_in.shape[2] - 4 * gd - 2 * heads) // 3
    sb_heads = sbd // hd
    n_exp = router_w.shape[2]
    assert seq % ROW_ALIGN == 0 and d % (2 * 8 * LANES) == 0 and hd == LANES
    assert 2 * heads <= LANES and n_exp <= LANES and sb_heads == heads
    assert depth == 1, "a second layer would need the meta rows carried through the combine stage"
    front = (-n_meta) % ROW_ALIGN
    lp = front + n_meta + seq
    rows = bsz * lp
    pack_rows = d // 2 // LANES

    h = jnp.concatenate([
        jnp.zeros((bsz, front, d), x.dtype),
        jnp.broadcast_to(meta_tokens.astype(x.dtype)[None], (bsz, n_meta, d)),
        x], axis=1).reshape(rows, d)

    wl = w_in[0]
    n_ba = 4 * gd
    w_main = jnp.concatenate([wl[:, :n_ba], wl[:, n_ba + 2 * heads:]], axis=1).astype(BF16)
    w_ba = jnp.pad(wl[:, n_ba:n_ba + 2 * heads], ((0, 0), (0, LANES - 2 * heads))).astype(BF16)
    tm = _pick(rows, (1024, 512, 256))
    tn = _pick(w_main.shape[1], (1024, 512, 256, 128))
    proj, ba = _in_proj(h, mix_norm_w[0][None], w_main, w_ba, tm, tn)
    proj3 = proj.reshape(bsz, lp, -1)
    ba3 = ba.reshape(bsz, lp, LANES)

    gparams = jnp.zeros((2, LANES), F32)
    gparams = gparams.at[0, heads:2 * heads].set(a_log[0].astype(F32))
    gparams = gparams.at[1, heads:2 * heads].set(dt_bias[0].astype(F32))
    qkv3, bg3 = _gdn_prep(proj3, ba3, conv_w[0].astype(F32), gparams, heads, hd)
    o_gdn = _gdn(qkv3, proj3, bg3, gdn_norm_w[0][None], heads, hd)
    o_sb = _sbk(proj3, sb_q_norm_w[0][None], sb_k_norm_w[0][None], sb_out_norm_w[0][None],
               sb_heads, hd, 4 * gd, front)

    rw = jnp.pad(router_w[0].astype(F32), ((0, 0), (0, LANES - n_exp)))
    rb = jnp.pad(router_b[0].astype(F32), (0, LANES - n_exp), constant_values=-1e30)[None]
    h1, hnp, info, gates, cnt = _mix(
        o_gdn.reshape(rows, gd), o_sb.reshape(rows, sbd), h, w_out[0].astype(BF16),
        ffn_norm_w[0][None], rw, rb, front, lp)

    slots, (pstart, padded), units, n_slots = _plan(
        info, cnt, n_exp, rows, lp, front, bsz * (n_meta + seq) * TOP_K)
    xs = _dispatch(slots, hnp, pstart, padded, units[3], n_slots, pack_rows, front, lp)
    ys = _experts(*units, xs, w_gate_up[0], b_gate_up[0], w_down[0], b_down[0], n_slots, pack_rows)
    return _combine(slots, gates, h1, ys, bsz, seq, lp, d)
```

```python
import functools

import jax
import jax.numpy as jnp
from jax import lax
from jax.experimental import pallas as pl
from jax.experimental.pallas import tpu as pltpu

F32 = jnp.float32
BF16 = jnp.bfloat16
U32 = jnp.uint32
I32 = jnp.int32

NORM_EPS = 1e-6
TOP_K = 4
SWIGLU_LIMIT = 7.0
SWIGLU_ALPHA = 1.702
LANES = 128
ROW_ALIGN = 256
GDN_PREP_ROWS = 256
GDN_CHUNK = 64
GDN_STEP_CHUNKS = 4
SB_BLOCK = 256
SB_GROUP = 4
MIX_ROWS = 512
DISPATCH_ROWS = 256
EXP_SUB = 256
EXP_GROUP = 3
EXP_UNIT = 9 * EXP_SUB
EXP_FTILE = 256
EXP_OUT_STAGES = 4
COMB_ROWS = 256
VMEM_LIMIT = 56 * 1024 * 1024
EXPERT_VMEM_LIMIT = 60 * 1024 * 1024
HIGH16 = 0xFFFF0000
SIGN_BIT = 0x80000000
LOG2E = 1.4426950408889634


def _cparams(n_grid, vmem=VMEM_LIMIT):
    return pltpu.CompilerParams(dimension_semantics=("arbitrary",) * n_grid, vmem_limit_bytes=vmem)


def _dot(a, b):
    return jnp.dot(a, b, preferred_element_type=F32)


def _dot_nt(a, b):
    return lax.dot_general(a, b, (((1,), (1,)), ((), ())), preferred_element_type=F32)


def _dot_tn(a, b):
    return lax.dot_general(a, b, (((0,), (0,)), ((), ())), preferred_element_type=F32)


def _split3(x):
    hi = x.astype(BF16)
    r1 = x - hi.astype(F32)
    mid = r1.astype(BF16)
    lo = (r1 - mid.astype(F32)).astype(BF16)
    return hi, mid, lo


def _dot3(a, b):
    ah = a.astype(BF16)
    al = (a - ah.astype(F32)).astype(BF16)
    bh = b.astype(BF16)
    bl = (b - bh.astype(F32)).astype(BF16)
    return _dot(ah, bh) + (_dot(ah, bl) + _dot(al, bh))


def _sigmoid(x):
    return 1.0 / (1.0 + jnp.exp(-x))


def _softplus(x):
    return jnp.maximum(x, 0.0) + jnp.log1p(jnp.exp(-jnp.abs(x)))


def _inproj_kernel(h_ref, nw_ref, w_ref, wba_ref, o_ref, ba_ref, xn_ref):
    @pl.when(pl.program_id(1) == 0)
    def _():
        x = h_ref[...]
        xn = x * lax.rsqrt(jnp.mean(x * x, axis=-1, keepdims=True) + NORM_EPS) * nw_ref[...]
        xn = xn.astype(BF16)
        xn_ref[...] = xn
        ba_ref[...] = _dot(xn, wba_ref[...])

    o_ref[...] = _dot(xn_ref[...], w_ref[...]).astype(o_ref.dtype)


def _in_proj(h0, norm_w, w_main, w_ba, tm, tn):
    rows, d = h0.shape
    n_main = w_main.shape[1]
    return pl.pallas_call(
        _inproj_kernel,
        grid=(rows // tm, n_main // tn),
        in_specs=[
            pl.BlockSpec((tm, d), lambda i, n: (i, 0)),
            pl.BlockSpec((1, d), lambda i, n: (0, 0)),
            pl.BlockSpec((d, tn), lambda i, n: (0, n)),
            pl.BlockSpec((d, LANES), lambda i, n: (0, 0)),
        ],
        out_specs=[
            pl.BlockSpec((tm, tn), lambda i, n: (i, n)),
            pl.BlockSpec((tm, LANES), lambda i, n: (i, 0)),
        ],
        out_shape=[
            jax.ShapeDtypeStruct((rows, n_main), BF16),
            jax.ShapeDtypeStruct((rows, LANES), F32),
        ],
        scratch_shapes=[pltpu.VMEM((tm, d), BF16)],
        compiler_params=_cparams(2),
        name="in_proj",
    )(h0, norm_w, w_main, w_ba)


def _gdn_prep_kernel(x_ref, prev_ref, ba_ref, cw_ref, gp_ref, o_ref, bg_ref, *, heads, hd):
    tp = x_ref.shape[0]
    gd = heads * hd
    taps = cw_ref.shape[0]
    x = x_ref[...].astype(F32)
    prev = jnp.where(pl.program_id(1) > 0, prev_ref[8:16, :].astype(F32), 0.0)
    xs = jnp.concatenate([prev, x], axis=0)
    y = cw_ref[taps - 1:taps, :] * x
    for s in range(1, taps):
        y = y + cw_ref[taps - 1 - s:taps - s, :] * pltpu.roll(xs, s, axis=0)[8:8 + tp, :]
    y = y * _sigmoid(y)
    for h in range(heads):
        q = y[:, h * hd:(h + 1) * hd]
        k = y[:, gd + h * hd:gd + (h + 1) * hd]
        q = q * (lax.rsqrt(jnp.sum(q * q, axis=-1, keepdims=True) + NORM_EPS) * (hd ** -0.5))
        k = k * lax.rsqrt(jnp.sum(k * k, axis=-1, keepdims=True) + NORM_EPS)
        o_ref[:, h * hd:(h + 1) * hd] = q.astype(o_ref.dtype)
        o_ref[:, gd + h * hd:gd + (h + 1) * hd] = k.astype(o_ref.dtype)
    o_ref[:, 2 * gd:] = y[:, 2 * gd:].astype(o_ref.dtype)
    ba = ba_ref[...]
    lane = lax.broadcasted_iota(I32, ba.shape, 1)
    decay = -jnp.exp(gp_ref[0:1, :]) * _softplus(ba + gp_ref[1:2, :])
    bg_ref[...] = jnp.where(lane < heads, _sigmoid(ba), decay)


def _gdn_prep(proj3, ba3, conv_w, gparams, heads, hd):
    bsz, lp, _ = proj3.shape
    gd = heads * hd
    tp = GDN_PREP_ROWS
    return pl.pallas_call(
        functools.partial(_gdn_prep_kernel, heads=heads, hd=hd),
        grid=(bsz, lp // tp),
        in_specs=[
            pl.BlockSpec((None, tp, 3 * gd), lambda b, i: (b, i, 0)),
            pl.BlockSpec((None, 16, 3 * gd), lambda b, i: (b, jnp.maximum(i * (tp // 16) - 1, 0), 0)),
            pl.BlockSpec((None, tp, LANES), lambda b, i: (b, i, 0)),
            pl.BlockSpec(conv_w.shape, lambda b, i: (0, 0)),
            pl.BlockSpec((2, LANES), lambda b, i: (0, 0)),
        ],
        out_specs=[
            pl.BlockSpec((None, tp, 3 * gd), lambda b, i: (b, i, 0)),
            pl.BlockSpec((None, tp, LANES), lambda b, i: (b, i, 0)),
        ],
        out_shape=[
            jax.ShapeDtypeStruct((bsz, lp, 3 * gd), BF16),
            jax.ShapeDtypeStruct((bsz, lp, LANES), F32),
        ],
        compiler_params=_cparams(2),
        name="gdn_prep",
    )(proj3, proj3, ba3, conv_w, gparams)


def _gdn_kernel(qkv_ref, z_ref, bg_ref, nw_ref, o_ref, state, *, heads, hd):
    c_rows = GDN_CHUNK
    gd = heads * hd

    @pl.when(pl.program_id(1) == 0)
    def _():
        state[...] = jnp.zeros_like(state)

    n_chunks = qkv_ref.shape[0] // c_rows
    r_i = lax.broadcasted_iota(I32, (c_rows, c_rows), 0)
    c_i = lax.broadcasted_iota(I32, (c_rows, c_rows), 1)
    incl = r_i >= c_i
    strict = r_i > c_i
    tri = incl.astype(BF16)
    n_sel = -(-heads // 8) * 8
    pick = (lax.broadcasted_iota(I32, (n_sel, LANES), 1)
            == lax.broadcasted_iota(I32, (n_sel, LANES), 0) + heads).astype(BF16)

    rows = [slice(j * c_rows, (j + 1) * c_rows) for j in range(n_chunks)]
    bg, gcum, gcum_rows = [], [], []
    for j in range(n_chunks):
        bg.append(bg_ref[rows[j], :])
        g_parts = _split3(bg[j])
        gcum.append(_dot(tri, g_parts[0]) + _dot(tri, g_parts[1]) + _dot(tri, g_parts[2]))
        c_parts = _split3(gcum[j])
        gcum_rows.append(_dot_nt(pick, c_parts[0]) + _dot_nt(pick, c_parts[1])
                         + _dot_nt(pick, c_parts[2]))

    units = [(j, h) for j in range(n_chunks) for h in range(heads)]
    us = range(len(units))
    q16 = [qkv_ref[rows[j], h * hd:(h + 1) * hd] for j, h in units]
    k16 = [qkv_ref[rows[j], gd + h * hd:gd + (h + 1) * hd] for j, h in units]
    k = [k16[u].astype(F32) for u in us]
    gc = [gcum[j][:, heads + h:heads + h + 1] for j, h in units]
    decay = [jnp.exp(jnp.minimum(gc[u] - gcum_rows[j][h:h + 1, :], 0.0))
             for u, (j, h) in enumerate(units)]
    kb = [k[u] * bg[j][:, h:h + 1] for u, (j, h) in enumerate(units)]
    p = [jnp.where(strict, _dot_nt(kb[u].astype(BF16), k16[u]) * decay[u], 0.0) for u in us]
    p = [(-p[u]).astype(BF16) for u in us]
    sol = [jnp.concatenate(
        [qkv_ref[rows[j], 2 * gd + h * hd:2 * gd + (h + 1) * hd].astype(F32) * bg[j][:, h:h + 1],
         kb[u] * jnp.exp(gc[u])], axis=1) for u, (j, h) in enumerate(units)]
    n_fac = c_rows.bit_length() - 1
    for i in range(n_fac):
        sol = [sol[u] + _dot(p[u], sol[u].astype(BF16)) for u in us]
        if i + 1 < n_fac:
            p = [_dot(p[u], p[u]).astype(BF16) for u in us]
    attn = [jnp.where(incl, _dot_nt(q16[u], k16[u]) * decay[u], 0.0).astype(BF16) for u in us]
    q_dec = [(q16[u].astype(F32) * jnp.exp(gc[u])).astype(BF16) for u in us]
    g_last = [gc[u][c_rows - 1:c_rows, :] for u in us]
    k_dec = [(k[u] * jnp.exp(g_last[u] - gc[u])).astype(BF16) for u in us]

    s_cur = [state[h] for h in range(heads)]
    for j in range(n_chunks):
        ids = [j * heads + h for h in range(heads)]
        s_b = [s.astype(BF16) for s in s_cur]
        v_new = [(sol[u][:, :hd] - _dot(sol[u][:, hd:].astype(BF16), s_b[h])).astype(BF16)
                 for h, u in enumerate(ids)]
        o = [_dot(q_dec[u], s_b[h]) + _dot(attn[u], v_new[h]) for h, u in enumerate(ids)]
        s_cur = [s_cur[h] * jnp.exp(g_last[u]) + _dot_tn(k_dec[u], v_new[h])
                 for h, u in enumerate(ids)]
        for h in range(heads):
            zf = z_ref[rows[j], h * hd:(h + 1) * hd].astype(F32)
            o_n = (o[h] * lax.rsqrt(jnp.mean(o[h] * o[h], axis=-1, keepdims=True) + NORM_EPS)
                   * nw_ref[...])
            o_ref[rows[j], h * hd:(h + 1) * hd] = (o_n * (zf * _sigmoid(zf))).astype(o_ref.dtype)
    for h in range(heads):
        state[h] = s_cur[h]


def _gdn(qkv3, proj3, bg3, norm_w, heads, hd):
    bsz, lp, _ = qkv3.shape
    gd = heads * hd
    c = GDN_CHUNK * GDN_STEP_CHUNKS
    return pl.pallas_call(
        functools.partial(_gdn_kernel, heads=heads, hd=hd),
        grid=(bsz, lp // c),
        in_specs=[
            pl.BlockSpec((None, c, 3 * gd), lambda b, i: (b, i, 0)),
            pl.BlockSpec((None, c, gd), lambda b, i: (b, i, 3)),
            pl.BlockSpec((None, c, LANES), lambda b, i: (b, i, 0)),
            pl.BlockSpec((1, hd), lambda b, i: (0, 0)),
        ],
        out_specs=pl.BlockSpec((None, c, gd), lambda b, i: (b, i, 0)),
        out_shape=jax.ShapeDtypeStruct((bsz, lp, gd), BF16),
        scratch_shapes=[pltpu.VMEM((heads, hd, hd), F32)],
        compiler_params=_cparams(2),
        name="gdn",
    )(qkv3, proj3, bg3, norm_w)


def _sbk_kernel(q_ref, k_ref, v_ref, qw_ref, kw_ref, ow_ref, tri_ref, o_ref, qn_ref, kn_ref,
                acc_ref, drop_ref, *, front, hd):
    t = SB_BLOCK
    nb = q_ref.shape[0] // t

    def prep(i, _):
        rows = pl.ds(pl.multiple_of(i * t, t), t)
        qb = q_ref[rows, :].astype(F32)
        kb = k_ref[rows, :].astype(F32)
        qn = qb * lax.rsqrt(jnp.mean(qb * qb, axis=-1, keepdims=True) + NORM_EPS) * qw_ref[...]
        qn_ref[rows, :] = (qn * (hd ** -0.5 * LOG2E)).astype(BF16)
        kn = kb * lax.rsqrt(jnp.mean(kb * kb, axis=-1, keepdims=True) + NORM_EPS) * kw_ref[...]
        kn_ref[rows, :] = kn.astype(BF16)
        acc_ref[rows, :] = jnp.zeros((t, hd), F32)
        drop_ref[rows, :] = jnp.zeros((t, LANES), F32)
        return 0
    lax.fori_loop(0, nb, prep, 0)

    def mask_of(kind):
        r_i = lax.broadcasted_iota(I32, (t, t), 0)
        c_i = lax.broadcasted_iota(I32, (t, t), 1)
        if kind == "causal":
            return c_i < r_i
        if kind == "front":
            return c_i >= front
        return (c_i < r_i) & (c_i >= front)

    def scores(qi, kj, kind):
        qrows = pl.ds(pl.multiple_of(qi * t, t), t)
        krows = pl.ds(pl.multiple_of(kj * t, t), t)
        s = _dot_nt(qn_ref[qrows, :], kn_ref[krows, :])
        neg_abs = pltpu.bitcast(pltpu.bitcast(s, U32) | jnp.uint32(SIGN_BIT), F32)
        drop = jnp.maximum(s, 0.0) + jnp.log2(1.0 + jnp.exp2(neg_abs))
        visible = None
        if kind is not None:
            visible = mask_of(kind)
            drop = jnp.where(visible, drop, 0.0)
        tail = _dot(drop.astype(BF16), tri_ref[...])
        return s, tail, visible

    def finish(tile, part):
        (qi, kj, kind), (s, tail, visible) = tile, part
        qrows = pl.ds(pl.multiple_of(qi * t, t), t)
        krows = pl.ds(pl.multiple_of(kj * t, t), t)
        dropped = drop_ref[qrows, :]
        weight = jnp.exp2(s - tail - jnp.concatenate([dropped] * (t // LANES), axis=1))
        if kind is not None:
            weight = jnp.where(visible, weight, 0.0)
        acc_ref[qrows, :] += _dot(weight.astype(BF16), v_ref[krows, :])
        drop_ref[qrows, :] = dropped + jnp.broadcast_to(tail[:, 0:1], dropped.shape)

    def run(tiles):
        pending = None
        for tile in tiles:
            part = scores(*tile)
            if pending is not None:
                finish(*pending)
            pending = (tile, part)
        finish(*pending)

    group = SB_GROUP

    def key_block(kj, diag_kind, plain_kind):
        n_tiles = nb - kj
        n_head = (n_tiles - 1) % group + 1
        for h in range(1, group + 1):
            @pl.when(n_head == h)
            def _():
                run([(kj, kj, diag_kind)] + [(kj + m, kj, plain_kind) for m in range(1, h)])

        def many(p, _):
            base = kj + n_head + group * p
            run([(base + m, kj, plain_kind) for m in range(group)])
            return 0
        lax.fori_loop(0, (n_tiles - n_head) // group, many, 0)

    def later_blocks(i, _):
        key_block(nb - 1 - i, "causal", None)
        return 0
    lax.fori_loop(0, nb - 1, later_blocks, 0)
    key_block(jnp.int32(0), "causal_front", "front")

    def write(i, _):
        rows = pl.ds(pl.multiple_of(i * t, t), t)
        o = acc_ref[rows, :]
        o = o * lax.rsqrt(jnp.mean(o * o, axis=-1, keepdims=True) + NORM_EPS) * ow_ref[...]
        o_ref[rows, :] = o.astype(o_ref.dtype)
        return 0
    lax.fori_loop(0, nb, write, 0)


def _sbk(proj3, qw, kw, ow, heads, hd, col0, front):
    bsz, lp, _ = proj3.shape
    t = SB_BLOCK
    cb = col0 // hd
    tri = (jnp.arange(t)[:, None] >= jnp.arange(t)[None, :]).astype(BF16)
    return pl.pallas_call(
        functools.partial(_sbk_kernel, front=front, hd=hd),
        grid=(bsz, heads),
        in_specs=[
            pl.BlockSpec((None, lp, hd), lambda b, h: (b, 0, cb + h)),
            pl.BlockSpec((None, lp, hd), lambda b, h: (b, 0, cb + heads + h)),
            pl.BlockSpec((None, lp, hd), lambda b, h: (b, 0, cb + 2 * heads + h)),
            pl.BlockSpec((1, hd), lambda b, h: (0, 0)),
            pl.BlockSpec((1, hd), lambda b, h: (0, 0)),
            pl.BlockSpec((1, hd), lambda b, h: (0, 0)),
            pl.BlockSpec(tri.shape, lambda b, h: (0, 0)),
        ],
        out_specs=pl.BlockSpec((None, lp, hd), lambda b, h: (b, 0, h)),
        out_shape=jax.ShapeDtypeStruct((bsz, lp, heads * hd), BF16),
        scratch_shapes=[pltpu.VMEM((lp, hd), BF16), pltpu.VMEM((lp, hd), BF16),
                        pltpu.VMEM((lp, hd), F32), pltpu.VMEM((lp, LANES), F32)],
        compiler_params=_cparams(2),
        name="sb",
    )(proj3, proj3, proj3, qw, kw, ow, tri)


def _mix_kernel(og_ref, os_ref, h_ref, wo_ref, nw_ref, rw_ref, rb_ref,
                h1_ref, hnp_ref, info_ref, gate_ref, cnt_ref, cnt_acc,
                *, front, lp, bsz, pack_rows):
    tm = MIX_ROWS
    i = pl.program_id(0)

    @pl.when(i == 0)
    def _():
        cnt_acc[...] = jnp.zeros_like(cnt_acc)

    gd = og_ref.shape[1]
    h1 = h_ref[...] + _dot(og_ref[...], wo_ref[0:gd, :]) + _dot(os_ref[...], wo_ref[gd:, :])
    h1_ref[...] = h1
    hn = h1 * lax.rsqrt(jnp.mean(h1 * h1, axis=-1, keepdims=True) + NORM_EPS) * nw_ref[...]

    half = hn.shape[1] // 2
    lo = pltpu.bitcast(hn[:, :half].astype(BF16).astype(F32), U32) >> 16
    hi = pltpu.bitcast(hn[:, half:].astype(BF16).astype(F32), U32) & jnp.uint32(HIGH16)
    word = lo | hi
    for s in range(pack_rows):
        hnp_ref[pl.ds(s, tm, stride=pack_rows), :] = word[:, s * LANES:(s + 1) * LANES]

    logits = _dot3(hn, rw_ref[...]) + rb_ref[...]
    lane = lax.broadcasted_iota(I32, (tm, LANES), 1)
    row = lax.broadcasted_iota(I32, (tm, 1), 0)
    pos = i * tm + row
    valid = pos < 0
    for b in range(bsz):
        valid = valid | ((pos >= b * lp + front) & (pos < (b + 1) * lp))
    work = logits
    tops, idxs, hots = [], [], []
    for _ in range(TOP_K):
        m = jnp.max(work, axis=1, keepdims=True)
        idx = jnp.min(jnp.where(work == m, lane, LANES), axis=1, keepdims=True)
        hot = lane == idx
        tops.append(m)
        idxs.append(idx)
        hots.append(hot)
        work = jnp.where(hot, -jnp.inf, work)
    exps = [jnp.exp(m - tops[0]) for m in tops]
    denom = exps[0] + exps[1] + exps[2] + exps[3]
    sel = jnp.zeros((tm, LANES), F32)
    for hot in hots:
        sel = sel + hot.astype(F32)
    sel = jnp.where(valid, sel, 0.0)

    r_i = lax.broadcasted_iota(I32, (tm, tm), 0)
    c_i = lax.broadcasted_iota(I32, (tm, tm), 1)
    before = (r_i > c_i).astype(BF16)
    rank = cnt_acc[0:1, :] + _dot(before, sel.astype(BF16))
    info = jnp.zeros((tm, LANES), I32)
    gates = jnp.zeros((tm, LANES), F32)
    for j in range(TOP_K):
        rank_j = jnp.sum(jnp.where(hots[j], rank, 0.0), axis=1, keepdims=True).astype(I32)
        info = jnp.where(lane == j, idxs[j], info)
        info = jnp.where(lane == TOP_K + j, rank_j, info)
        gates = jnp.where(lane == j, exps[j] / denom, gates)
    info_ref[...] = info
    gate_ref[...] = gates
    cnt_acc[...] = cnt_acc[...] + jnp.sum(sel, axis=0, keepdims=True)
    cnt_ref[...] = cnt_acc[...]


def _mix(o_gdn, o_sb, h0, w_out, norm_w, router_w, router_b, front, lp):
    rows, d = h0.shape
    gd = o_gdn.shape[1]
    tm = MIX_ROWS
    pack_rows = d // 2 // LANES
    kern = functools.partial(_mix_kernel, front=front, lp=lp, bsz=rows // lp, pack_rows=pack_rows)
    return pl.pallas_call(
        kern,
        grid=(rows // tm,),
        in_specs=[
            pl.BlockSpec((tm, gd), lambda i: (i, 0)),
            pl.BlockSpec((tm, gd), lambda i: (i, 0)),
            pl.BlockSpec((tm, d), lambda i: (i, 0)),
            pl.BlockSpec(w_out.shape, lambda i: (0, 0)),
            pl.BlockSpec((1, d), lambda i: (0, 0)),
            pl.BlockSpec((d, LANES), lambda i: (0, 0)),
            pl.BlockSpec((1, LANES), lambda i: (0, 0)),
        ],
        out_specs=[
            pl.BlockSpec((tm, d), lambda i: (i, 0)),
            pl.BlockSpec((tm * pack_rows, LANES), lambda i: (i, 0)),
            pl.BlockSpec((tm, LANES), lambda i: (i, 0)),
            pl.BlockSpec((tm, LANES), lambda i: (i, 0)),
            pl.BlockSpec((8, LANES), lambda i: (0, 0)),
        ],
        out_shape=[
            jax.ShapeDtypeStruct((rows, d), F32),
            jax.ShapeDtypeStruct((rows * pack_rows, LANES), U32),
            jax.ShapeDtypeStruct((rows, LANES), I32),
            jax.ShapeDtypeStruct((rows, LANES), F32),
            jax.ShapeDtypeStruct((8, LANES), F32),
        ],
        scratch_shapes=[pltpu.VMEM((8, LANES), F32)],
        compiler_params=_cparams(1),
        name="mix_router",
    )(o_gdn, o_sb, h0, w_out, norm_w, router_w, router_b)


def _dispatch_kernel(pstart_ref, padded_ref, used_ref, slot_ref, hnp_ref, xs_ref, zbuf, sem, zsem,
                     *, pack_rows, front, blocks_per_batch, n_slots):
    tm = DISPATCH_ROWS
    ts = EXP_SUB
    i = pl.program_id(0)
    n_exp = pstart_ref.shape[0]

    @pl.when(i == 0)
    def _():
        zbuf[...] = jnp.zeros_like(zbuf)

        def zero_copy(row):
            return pltpu.make_async_copy(
                zbuf, xs_ref.at[pl.ds(row * pack_rows, ts * pack_rows), :], zsem)

        def pad_block(e, _, *, start):
            @pl.when(padded_ref[e] > 0)
            def _():
                cp = zero_copy(pstart_ref[e] + padded_ref[e] - ts)
                cp.start() if start else cp.wait()
            return 0

        used = used_ref[0]
        n_tail = (n_slots - used) // ts
        lax.fori_loop(0, n_exp, functools.partial(pad_block, start=True), 0)
        lax.fori_loop(0, n_tail, lambda j, c: (zero_copy(used + j * ts).start(), c)[1], 0)
        lax.fori_loop(0, n_exp, functools.partial(pad_block, start=False), 0)
        lax.fori_loop(0, n_tail, lambda j, c: (zero_copy(used + j * ts).wait(), c)[1], 0)

    def copy(tok, j):
        src = pl.multiple_of(tok * pack_rows, pack_rows)
        return pltpu.make_async_copy(
            hnp_ref.at[pl.ds(src, pack_rows), :],
            xs_ref.at[pl.ds(slot_ref[0, tok * TOP_K + j] * pack_rows, pack_rows), :], sem)

    def start(tok, _):
        for j in range(TOP_K):
            copy(tok, j).start(priority=j % 2)
        return 0

    def wait_all(n_tok):
        for _ in range(TOP_K):
            pltpu.make_async_copy(
                hnp_ref.at[pl.ds(0, n_tok * pack_rows), :],
                xs_ref.at[pl.ds(0, n_tok * pack_rows), :], sem).wait()

    @pl.when(i % blocks_per_batch == 0)
    def _():
        lax.fori_loop(front, tm, start, 0)
        wait_all(tm - front)

    @pl.when(i % blocks_per_batch != 0)
    def _():
        lax.fori_loop(0, tm, start, 0)
        wait_all(tm)


def _dispatch(slots, hnp, pstart, padded, used_rows, n_slots, pack_rows, front, lp):
    rows = slots.shape[0]
    tm = DISPATCH_ROWS
    slots3 = slots.reshape(rows // tm, 1, tm * TOP_K)
    grid_spec = pltpu.PrefetchScalarGridSpec(
        num_scalar_prefetch=3,
        grid=(rows // tm,),
        in_specs=[
            pl.BlockSpec((None, 1, tm * TOP_K), lambda i, *_: (i, 0, 0), memory_space=pltpu.SMEM),
            pl.BlockSpec((tm * pack_rows, LANES), lambda i, *_: (i, 0)),
        ],
        out_specs=pl.BlockSpec(memory_space=pl.ANY),
        scratch_shapes=[pltpu.VMEM((EXP_SUB * pack_rows, LANES), U32),
                        pltpu.SemaphoreType.DMA(()), pltpu.SemaphoreType.DMA(())],
    )
    return pl.pallas_call(
        functools.partial(_dispatch_kernel, pack_rows=pack_rows, front=front,
                          blocks_per_batch=lp // tm, n_slots=n_slots),
        grid_spec=grid_spec,
        out_shape=jax.ShapeDtypeStruct((n_slots * pack_rows, LANES), U32),
        compiler_params=_cparams(1),
        name="dispatch",
    )(pstart, padded, used_rows, slots3, hnp)


def _expert_kernel(ue_ref, ur_ref, un_ref, used_ref, xs_ref, wg_ref, wu_ref, bg_ref, bu_ref, wd_ref,
                   bd_ref, ys_ref, xbuf, xb, acc, wg_b, wu_b, wd_b, ystage, sem_in, sem_out,
                   *, pack_rows, n_slots):
    del ue_ref
    ts = EXP_SUB
    u = pl.program_id(0)
    f = pl.program_id(1)
    n_f = pl.num_programs(1)
    nsub = un_ref[u]
    row0 = ur_ref[u]
    d = xb.shape[1]
    half = d // 2
    tf = wg_b.shape[1]

    @pl.when((u == 0) & (f == 0))
    def _():
        ystage[...] = jnp.zeros_like(ystage)
        used = used_ref[0]

        def tail_copy(i):
            return pltpu.make_async_copy(
                ystage.at[0], ys_ref.at[pl.ds((used + i * ts) * pack_rows, ts * pack_rows), :],
                sem_out.at[0])

        n_tail = (n_slots - used) // ts
        lax.fori_loop(0, n_tail, lambda i, c: (tail_copy(i).start(), c)[1], 0)
        lax.fori_loop(0, n_tail, lambda i, c: (tail_copy(i).wait(), c)[1], 0)

    def in_copy(unit_row0, s):
        return pltpu.make_async_copy(
            xs_ref.at[pl.ds((unit_row0 + s * ts) * pack_rows, ts * pack_rows), :],
            xbuf.at[s], sem_in.at[s])

    def start_loads(unit):
        unit_row0 = ur_ref[unit]
        lax.fori_loop(0, un_ref[unit], lambda s, c: (in_copy(unit_row0, s).start(), c)[1], 0)

    @pl.when((u == 0) & (f == 0))
    def _():
        start_loads(0)

    @pl.when((f == 0) & (nsub > 0))
    def _():
        def load(s, _):
            in_copy(row0, s).wait()
            words = jnp.concatenate(
                [xbuf[s, pl.ds(c, ts, stride=pack_rows), :] for c in range(pack_rows)], axis=1)
            r = pl.multiple_of(s * ts, ts)
            xb[pl.ds(r, ts), :half] = pltpu.bitcast(words << 16, F32).astype(BF16)
            xb[pl.ds(r, ts), half:] = pltpu.bitcast(words & jnp.uint32(HIGH16), F32).astype(BF16)
            acc[pl.ds(r, ts), :] = jnp.broadcast_to(bd_ref[...], (ts, d))
            return 0
        lax.fori_loop(0, nsub, load, 0)

        @pl.when(u + 1 < pl.num_programs(0))
        def _():
            start_loads(jnp.minimum(u + 1, pl.num_programs(0) - 1))

    @pl.when(nsub > 0)
    def _():
        wg_b[...] = wg_ref[...].astype(BF16)
        wu_b[...] = wu_ref[...].astype(BF16)
        wd_b[...] = wd_ref[...].astype(BF16)

        def ffn_rows(r, m):
            x = xb[pl.ds(r, m), :]
            g = jnp.minimum(_dot(x, wg_b[...]) + bg_ref[...], SWIGLU_LIMIT)
            up = jnp.clip(_dot(x, wu_b[...]) + bu_ref[...], -SWIGLU_LIMIT, SWIGLU_LIMIT)
            act = ((up + 1.0) * g * _sigmoid(SWIGLU_ALPHA * g)).astype(BF16)
            acc[pl.ds(r, m), :] += _dot(act, wd_b[...])

        group = EXP_GROUP

        def trip(p, _):
            ffn_rows(pl.multiple_of(p * (group * ts), ts), group * ts)
            return 0
        lax.fori_loop(0, nsub // group, trip, 0)
        for k in range(1, group):
            @pl.when(nsub % group == k)
            def _():
                ffn_rows(pl.multiple_of(nsub // group * (group * ts), ts), k * ts)

    n_stage = ystage.shape[0]

    def out_copy(s):
        return pltpu.make_async_copy(
            ystage.at[s % n_stage],
            ys_ref.at[pl.ds((row0 + s * ts) * pack_rows, ts * pack_rows), :],
            sem_out.at[s % n_stage])

    @pl.when((f == n_f - 1) & (nsub > 0))
    def _():
        def store(s, _):
            @pl.when(s >= n_stage)
            def _():
                out_copy(s - n_stage).wait()
            y = acc[pl.ds(pl.multiple_of(s * ts, ts), ts), :]
            lo = pltpu.bitcast(y[:, :half].astype(BF16).astype(F32), U32) >> 16
            hi = pltpu.bitcast(y[:, half:].astype(BF16).astype(F32), U32) & jnp.uint32(HIGH16)
            word = lo | hi
            for c in range(pack_rows):
                ystage[s % n_stage, pl.ds(c, ts, stride=pack_rows), :] = (
                    word[:, c * LANES:(c + 1) * LANES])
            out_copy(s).start()
            return 0
        lax.fori_loop(0, nsub, store, 0)
        for k in range(n_stage, 0, -1):
            @pl.when(nsub >= k)
            def _():
                out_copy(nsub - k).wait()


def _experts(unit_e, unit_row0, unit_nsub, used_rows, xs, w_gate_up, b_gate_up, w_down, b_down,
             n_slots, pack_rows):
    n_exp, d, two_de = w_gate_up.shape
    de = two_de // 2
    tf = EXP_FTILE
    n_f = de // tf
    n_units = unit_e.shape[0]
    last_f = n_f - 1

    def fidx(u, f, un):
        return jnp.where(un[u] > 0, f, last_f)

    grid_spec = pltpu.PrefetchScalarGridSpec(
        num_scalar_prefetch=4,
        grid=(n_units, n_f),
        in_specs=[
            pl.BlockSpec(memory_space=pl.ANY),
            pl.BlockSpec((None, d, tf), lambda u, f, ue, ur, un, us: (ue[u], 0, fidx(u, f, un))),
            pl.BlockSpec((None, d, tf),
                         lambda u, f, ue, ur, un, us: (ue[u], 0, n_f + fidx(u, f, un))),
            pl.BlockSpec((None, 1, tf), lambda u, f, ue, ur, un, us: (ue[u], 0, fidx(u, f, un))),
            pl.BlockSpec((None, 1, tf),
                         lambda u, f, ue, ur, un, us: (ue[u], 0, n_f + fidx(u, f, un))),
            pl.BlockSpec((None, tf, d), lambda u, f, ue, ur, un, us: (ue[u], fidx(u, f, un), 0)),
            pl.BlockSpec((None, 1, d), lambda u, f, ue, ur, un, us: (ue[u], 0, 0)),
        ],
        out_specs=pl.BlockSpec(memory_space=pl.ANY),
        scratch_shapes=[
            pltpu.VMEM((EXP_UNIT // EXP_SUB, EXP_SUB * pack_rows, LANES), U32),
            pltpu.VMEM((EXP_UNIT, d), BF16),
            pltpu.VMEM((EXP_UNIT, d), F32),
            pltpu.VMEM((d, tf), BF16),
            pltpu.VMEM((d, tf), BF16),
            pltpu.VMEM((tf, d), BF16),
            pltpu.VMEM((EXP_OUT_STAGES, EXP_SUB * pack_rows, LANES), U32),
            pltpu.SemaphoreType.DMA((EXP_UNIT // EXP_SUB,)),
            pltpu.SemaphoreType.DMA((EXP_OUT_STAGES,)),
        ],
    )
    return pl.pallas_call(
        functools.partial(_expert_kernel, pack_rows=pack_rows, n_slots=n_slots),
        grid_spec=grid_spec,
        out_shape=jax.ShapeDtypeStruct((n_slots * pack_rows, LANES), U32),
        compiler_params=_cparams(2, vmem=EXPERT_VMEM_LIMIT),
        name="experts",
    )(unit_e, unit_row0, unit_nsub, used_rows, xs, w_gate_up, w_gate_up,
      b_gate_up.reshape(n_exp, 1, two_de), b_gate_up.reshape(n_exp, 1, two_de),
      w_down, b_down.reshape(n_exp, 1, d))


def _combine_kernel(slot_ref, gate_ref, h1_ref, ys_ref, o_ref, ybuf, sem, *, pack_rows):
    tc = COMB_ROWS
    half = h1_ref.shape[1] // 2

    def copy(tok, j):
        s = slot_ref[0, tok * TOP_K + j]
        dst = pl.multiple_of(tok * pack_rows, pack_rows)
        return pltpu.make_async_copy(
            ys_ref.at[pl.ds(s * pack_rows, pack_rows), :],
            ybuf.at[j, pl.ds(dst, pack_rows), :], sem)

    def start(tok, _):
        for j in range(TOP_K):
            copy(tok, j).start(priority=j % 2)
        return 0

    lax.fori_loop(0, tc, start, 0)
    for j in range(TOP_K):
        pltpu.make_async_copy(ys_ref.at[pl.ds(0, tc * pack_rows), :], ybuf.at[j], sem).wait()

    gates = gate_ref[...]
    for c in range(pack_rows):
        lo = h1_ref[:, c * LANES:(c + 1) * LANES]
        hi = h1_ref[:, half + c * LANES:half + (c + 1) * LANES]
        for j in range(TOP_K):
            word = ybuf[j, pl.ds(c, tc, stride=pack_rows), :]
            gate = gates[:, j:j + 1]
            lo = lo + gate * pltpu.bitcast(word << 16, F32)
            hi = hi + gate * pltpu.bitcast(word & jnp.uint32(HIGH16), F32)
        o_ref[:, c * LANES:(c + 1) * LANES] = lo
        o_ref[:, half + c * LANES:half + (c + 1) * LANES] = hi


def _combine(slots, gates, h1, ys, bsz, seq, lp, d):
    tc = COMB_ROWS
    rows = slots.shape[0]
    pack_rows = d // 2 // LANES
    slots3 = slots.reshape(rows // tc, 1, tc * TOP_K)
    nb = lp // tc
    first = (lp - seq) // tc
    return pl.pallas_call(
        functools.partial(_combine_kernel, pack_rows=pack_rows),
        grid=(bsz, seq // tc),
        in_specs=[
            pl.BlockSpec((None, 1, tc * TOP_K), lambda b, i: (b * nb + first + i, 0, 0),
                         memory_space=pltpu.SMEM),
            pl.BlockSpec((tc, LANES), lambda b, i: (b * nb + first + i, 0)),
            pl.BlockSpec((tc, d), lambda b, i: (b * nb + first + i, 0)),
            pl.BlockSpec(memory_space=pl.ANY),
        ],
        out_specs=pl.BlockSpec((None, tc, d), lambda b, i: (b, i, 0)),
        out_shape=jax.ShapeDtypeStruct((bsz, seq, d), F32),
        scratch_shapes=[pltpu.VMEM((TOP_K, tc * pack_rows, LANES), U32), pltpu.SemaphoreType.DMA(())],
        compiler_params=_cparams(2),
        name="combine",
    )(slots3, gates, h1, ys)


def _pick(n, candidates):
    for c in candidates:
        if n % c == 0:
            return c
    raise ValueError(f"no block size in {candidates} divides {n}")


def _plan(info, cnt, n_exp, rows, lp, front, n_assign):
    counts = cnt[0, :n_exp].astype(I32)
    padded = (counts + EXP_SUB - 1) // EXP_SUB * EXP_SUB
    pstart = jnp.cumsum(padded) - padded
    eid = info[:, :TOP_K]
    rank = info[:, TOP_K:2 * TOP_K]
    onehot = eid[:, :, None] == jnp.arange(n_exp, dtype=I32)[None, None, :]
    slot = rank + jnp.sum(jnp.where(onehot, pstart[None, None, :], 0), axis=-1)
    row_valid = (jnp.arange(rows, dtype=I32) % lp) >= front
    slots = jnp.where(row_valid[:, None], slot, -1).astype(I32)

    n_slots = (n_assign + n_exp * (EXP_SUB - 1)) // EXP_SUB * EXP_SUB
    units_per_e = (padded + EXP_UNIT - 1) // EXP_UNIT
    cum_units = jnp.cumsum(units_per_e)
    n_units = n_slots // EXP_UNIT + n_exp
    uidx = jnp.arange(n_units, dtype=I32)
    ue = jnp.sum(cum_units[None, :] <= uidx[:, None], axis=1).astype(I32)
    live = ue < n_exp
    last_e = jnp.max(jnp.where(counts > 0, jnp.arange(n_exp, dtype=I32), 0))
    ue = jnp.where(live, ue, last_e)
    k_in_e = uidx - (cum_units - units_per_e)[ue]
    nsub_e = (padded // EXP_SUB)[ue]
    units_e = jnp.maximum(units_per_e[ue], 1)
    base = nsub_e // units_e
    extra = nsub_e - base * units_e
    first_sub = k_in_e * base + jnp.minimum(k_in_e, extra)
    unit_row0 = jnp.where(live, pstart[ue] + first_sub * EXP_SUB, 0).astype(I32)
    unit_nsub = jnp.where(live, base + (k_in_e < extra), 0).astype(I32)
    used_rows = jnp.sum(padded).astype(I32)[None]
    return slots, (pstart.astype(I32), padded.astype(I32)), (ue, unit_row0, unit_nsub, used_rows), n_slots


def kernel(x, meta_tokens, mix_norm_w, w_in, conv_w, a_log, dt_bias, gdn_norm_w, sb_q_norm_w,
           sb_k_norm_w, sb_out_norm_w, w_out, ffn_norm_w, router_w, router_b, w_gate_up, b_gate_up,
           w_down, b_down):
    bsz, seq, d = x.shape
    n_meta = meta_tokens.shape[0]
    depth = mix_norm_w.shape[0]
    heads = a_log.shape[1]
    hd = gdn_norm_w.shape[1]
    gd = heads * hd
    sbd = (w_in.shape[2] - 4 * gd - 2 * heads) // 3
    sb_heads = sbd // hd
    n_exp = router_w.shape[2]
    assert seq % ROW_ALIGN == 0 and d % (2 * 8 * LANES) == 0 and hd == LANES
    assert 2 * heads <= LANES and n_exp <= LANES and sb_heads == heads
    assert depth == 1, "a second layer would need the meta rows carried through the combine stage"
    front = (-n_meta) % ROW_ALIGN
    lp = front + n_meta + seq
    rows = bsz * lp
    pack_rows = d // 2 // LANES

    h = jnp.concatenate([
        jnp.zeros((bsz, front, d), x.dtype),
        jnp.broadcast_to(meta_tokens.astype(x.dtype)[None], (bsz, n_meta, d)),
        x], axis=1).reshape(rows, d)

    wl = w_in[0]
    n_ba = 4 * gd
    w_main = jnp.concatenate([wl[:, :n_ba], wl[:, n_ba + 2 * heads:]], axis=1).astype(BF16)
    w_ba = jnp.pad(wl[:, n_ba:n_ba + 2 * heads], ((0, 0), (0, LANES - 2 * heads))).astype(BF16)
    tm = _pick(rows, (1024, 512, 256))
    tn = _pick(w_main.shape[1], (1024, 512, 256, 128))
    proj, ba = _in_proj(h, mix_norm_w[0][None], w_main, w_ba, tm, tn)
    proj3 = proj.reshape(bsz, lp, -1)
    ba3 = ba.reshape(bsz, lp, LANES)

    gparams = jnp.zeros((2, LANES), F32)
    gparams = gparams.at[0, heads:2 * heads].set(a_log[0].astype(F32))
    gparams = gparams.at[1, heads:2 * heads].set(dt_bias[0].astype(F32))
    qkv3, bg3 = _gdn_prep(proj3, ba3, conv_w[0].astype(F32), gparams, heads, hd)
    o_gdn = _gdn(qkv3, proj3, bg3, gdn_norm_w[0][None], heads, hd)
    o_sb = _sbk(proj3, sb_q_norm_w[0][None], sb_k_norm_w[0][None], sb_out_norm_w[0][None],
               sb_heads, hd, 4 * gd, front)

    rw = jnp.pad(router_w[0].astype(F32), ((0, 0), (0, LANES - n_exp)))
    rb = jnp.pad(router_b[0].astype(F32), (0, LANES - n_exp), constant_values=-1e30)[None]
    h1, hnp, info, gates, cnt = _mix(
        o_gdn.reshape(rows, gd), o_sb.reshape(rows, sbd), h, w_out[0].astype(BF16),
        ffn_norm_w[0][None], rw, rb, front, lp)

    slots, (pstart, padded), units, n_slots = _plan(
        info, cnt, n_exp, rows, lp, front, bsz * (n_meta + seq) * TOP_K)
    xs = _dispatch(slots, hnp, pstart, padded, units[3], n_slots, pack_rows, front, lp)
    ys = _experts(*units, xs, w_gate_up[0], b_gate_up[0], w_down[0], b_down[0], n_slots, pack_rows)
    return _combine(slots, gates, h1, ys, bsz, seq, lp, d)
```

```python
import functools

import jax
import jax.numpy as jnp
from jax import lax
from jax.experimental import pallas as pl
from jax.experimental.pallas import tpu as pltpu

F32 = jnp.float32
BF16 = jnp.bfloat16
U32 = jnp.uint32
I32 = jnp.int32

NORM_EPS = 1e-6
TOP_K = 4
SWIGLU_LIMIT = 7.0
SWIGLU_ALPHA = 1.702
LANES = 128
ROW_ALIGN = 256
GDN_PREP_ROWS = 256
GDN_CHUNK = 64
GDN_STEP_CHUNKS = 4
SB_BLOCK = 256
SB_GROUP = 4
MIX_ROWS = 512
DISPATCH_ROWS = 256
EXP_SUB = 256
EXP_GROUP = 3
EXP_UNIT = 9 * EXP_SUB
EXP_FTILE = 256
EXP_OUT_STAGES = 4
COMB_ROWS = 256
VMEM_LIMIT = 56 * 1024 * 1024
EXPERT_VMEM_LIMIT = 60 * 1024 * 1024
HIGH16 = 0xFFFF0000
SIGN_BIT = 0x80000000
LOG2E = 1.4426950408889634


def _cparams(n_grid, vmem=VMEM_LIMIT):
    return pltpu.CompilerParams(dimension_semantics=("arbitrary",) * n_grid, vmem_limit_bytes=vmem)


def _dot(a, b):
    return jnp.dot(a, b, preferred_element_type=F32)


def _dot_nt(a, b):
    return lax.dot_general(a, b, (((1,), (1,)), ((), ())), preferred_element_type=F32)


def _dot_tn(a, b):
    return lax.dot_general(a, b, (((0,), (0,)), ((), ())), preferred_element_type=F32)


def _split3(x):
    hi = x.astype(BF16)
    r1 = x - hi.astype(F32)
    mid = r1.astype(BF16)
    lo = (r1 - mid.astype(F32)).astype(BF16)
    return hi, mid, lo


def _dot3(a, b):
    ah = a.astype(BF16)
    al = (a - ah.astype(F32)).astype(BF16)
    bh = b.astype(BF16)
    bl = (b - bh.astype(F32)).astype(BF16)
    return _dot(ah, bh) + (_dot(ah, bl) + _dot(al, bh))


def _sigmoid(x):
    return 1.0 / (1.0 + jnp.exp(-x))


def _softplus(x):
    return jnp.maximum(x, 0.0) + jnp.log1p(jnp.exp(-jnp.abs(x)))


def _inproj_kernel(h_ref, nw_ref, w_ref, wba_ref, o_ref, ba_ref, xn_ref):
    @pl.when(pl.program_id(1) == 0)
    def _():
        x = h_ref[...]
        xn = x * lax.rsqrt(jnp.mean(x * x, axis=-1, keepdims=True) + NORM_EPS) * nw_ref[...]
        xn = xn.astype(BF16)
        xn_ref[...] = xn
        ba_ref[...] = _dot(xn, wba_ref[...])

    o_ref[...] = _dot(xn_ref[...], w_ref[...]).astype(o_ref.dtype)


def _in_proj(h0, norm_w, w_main, w_ba, tm, tn):
    rows, d = h0.shape
    n_main = w_main.shape[1]
    return pl.pallas_call(
        _inproj_kernel,
        grid=(rows // tm, n_main // tn),
        in_specs=[
            pl.BlockSpec((tm, d), lambda i, n: (i, 0)),
            pl.BlockSpec((1, d), lambda i, n: (0, 0)),
            pl.BlockSpec((d, tn), lambda i, n: (0, n)),
            pl.BlockSpec((d, LANES), lambda i, n: (0, 0)),
        ],
        out_specs=[
            pl.BlockSpec((tm, tn), lambda i, n: (i, n)),
            pl.BlockSpec((tm, LANES), lambda i, n: (i, 0)),
        ],
        out_shape=[
            jax.ShapeDtypeStruct((rows, n_main), BF16),
            jax.ShapeDtypeStruct((rows, LANES), F32),
        ],
        scratch_shapes=[pltpu.VMEM((tm, d), BF16)],
        compiler_params=_cparams(2),
        name="in_proj",
    )(h0, norm_w, w_main, w_ba)


def _gdn_prep_kernel(x_ref, prev_ref, ba_ref, cw_ref, gp_ref, o_ref, bg_ref, *, heads, hd):
    tp = x_ref.shape[0]
    gd = heads * hd
    taps = cw_ref.shape[0]
    x = x_ref[...].astype(F32)
    prev = jnp.where(pl.program_id(1) > 0, prev_ref[8:16, :].astype(F32), 0.0)
    xs = jnp.concatenate([prev, x], axis=0)
    y = cw_ref[taps - 1:taps, :] * x
    for s in range(1, taps):
        y = y + cw_ref[taps - 1 - s:taps - s, :] * pltpu.roll(xs, s, axis=0)[8:8 + tp, :]
    y = y * _sigmoid(y)
    for h in range(heads):
        q = y[:, h * hd:(h + 1) * hd]
        k = y[:, gd + h * hd:gd + (h + 1) * hd]
        q = q * (lax.rsqrt(jnp.sum(q * q, axis=-1, keepdims=True) + NORM_EPS) * (hd ** -0.5))
        k = k * lax.rsqrt(jnp.sum(k * k, axis=-1, keepdims=True) + NORM_EPS)
        o_ref[:, h * hd:(h + 1) * hd] = q.astype(o_ref.dtype)
        o_ref[:, gd + h * hd:gd + (h + 1) * hd] = k.astype(o_ref.dtype)
    o_ref[:, 2 * gd:] = y[:, 2 * gd:].astype(o_ref.dtype)
    ba = ba_ref[...]
    lane = lax.broadcasted_iota(I32, ba.shape, 1)
    decay = -jnp.exp(gp_ref[0:1, :]) * _softplus(ba + gp_ref[1:2, :])
    bg_ref[...] = jnp.where(lane < heads, _sigmoid(ba), decay)


def _gdn_prep(proj3, ba3, conv_w, gparams, heads, hd):
    bsz, lp, _ = proj3.shape
    gd = heads * hd
    tp = GDN_PREP_ROWS
    return pl.pallas_call(
        functools.partial(_gdn_prep_kernel, heads=heads, hd=hd),
        grid=(bsz, lp // tp),
        in_specs=[
            pl.BlockSpec((None, tp, 3 * gd), lambda b, i: (b, i, 0)),
            pl.BlockSpec((None, 16, 3 * gd), lambda b, i: (b, jnp.maximum(i * (tp // 16) - 1, 0), 0)),
            pl.BlockSpec((None, tp, LANES), lambda b, i: (b, i, 0)),
            pl.BlockSpec(conv_w.shape, lambda b, i: (0, 0)),
            pl.BlockSpec((2, LANES), lambda b, i: (0, 0)),
        ],
        out_specs=[
            pl.BlockSpec((None, tp, 3 * gd), lambda b, i: (b, i, 0)),
            pl.BlockSpec((None, tp, LANES), lambda b, i: (b, i, 0)),
        ],
        out_shape=[
            jax.ShapeDtypeStruct((bsz, lp, 3 * gd), BF16),
            jax.ShapeDtypeStruct((bsz, lp, LANES), F32),
        ],
        compiler_params=_cparams(2),
        name="gdn_prep",
    )(proj3, proj3, ba3, conv_w, gparams)


def _gdn_kernel(qkv_ref, z_ref, bg_ref, nw_ref, o_ref, state, *, heads, hd):
    c_rows = GDN_CHUNK
    gd = heads * hd

    @pl.when(pl.program_id(1) == 0)
    def _():
        state[...] = jnp.zeros_like(state)

    n_chunks = qkv_ref.shape[0] // c_rows
    r_i = lax.broadcasted_iota(I32, (c_rows, c_rows), 0)
    c_i = lax.broadcasted_iota(I32, (c_rows, c_rows), 1)
    incl = r_i >= c_i
    strict = r_i > c_i
    tri = incl.astype(BF16)
    n_sel = -(-heads // 8) * 8
    pick = (lax.broadcasted_iota(I32, (n_sel, LANES), 1)
            == lax.broadcasted_iota(I32, (n_sel, LANES), 0) + heads).astype(BF16)

    rows = [slice(j * c_rows, (j + 1) * c_rows) for j in range(n_chunks)]
    bg, gcum, gcum_rows = [], [], []
    for j in range(n_chunks):
        bg.append(bg_ref[rows[j], :])
        g_parts = _split3(bg[j])
        gcum.append(_dot(tri, g_parts[0]) + _dot(tri, g_parts[1]) + _dot(tri, g_parts[2]))
        c_parts = _split3(gcum[j])
        gcum_rows.append(_dot_nt(pick, c_parts[0]) + _dot_nt(pick, c_parts[1])
                         + _dot_nt(pick, c_parts[2]))

    units = [(j, h) for j in range(n_chunks) for h in range(heads)]
    us = range(len(units))
    q16 = [qkv_ref[rows[j], h * hd:(h + 1) * hd] for j, h in units]
    k16 = [qkv_ref[rows[j], gd + h * hd:gd + (h + 1) * hd] for j, h in units]
    k = [k16[u].astype(F32) for u in us]
    gc = [gcum[j][:, heads + h:heads + h + 1] for j, h in units]
    decay = [jnp.exp(jnp.minimum(gc[u] - gcum_rows[j][h:h + 1, :], 0.0))
             for u, (j, h) in enumerate(units)]
    kb = [k[u] * bg[j][:, h:h + 1] for u, (j, h) in enumerate(units)]
    p = [jnp.where(strict, _dot_nt(kb[u].astype(BF16), k16[u]) * decay[u], 0.0) for u in us]
    p = [(-p[u]).astype(BF16) for u in us]
    sol = [jnp.concatenate(
        [qkv_ref[rows[j], 2 * gd + h * hd:2 * gd + (h + 1) * hd].astype(F32) * bg[j][:, h:h + 1],
         kb[u] * jnp.exp(gc[u])], axis=1) for u, (j, h) in enumerate(units)]
    n_fac = c_rows.bit_length() - 1
    for i in range(n_fac):
        sol = [sol[u] + _dot(p[u], sol[u].astype(BF16)) for u in us]
        if i + 1 < n_fac:
            p = [_dot(p[u], p[u]).astype(BF16) for u in us]
    attn = [jnp.where(incl, _dot_nt(q16[u], k16[u]) * decay[u], 0.0).astype(BF16) for u in us]
    q_dec = [(q16[u].astype(F32) * jnp.exp(gc[u])).astype(BF16) for u in us]
    g_last = [gc[u][c_rows - 1:c_rows, :] for u in us]
    k_dec = [(k[u] * jnp.exp(g_last[u] - gc[u])).astype(BF16) for u in us]

    s_cur = [state[h] for h in range(heads)]
    for j in range(n_chunks):
        ids = [j * heads + h for h in range(heads)]
        s_b = [s.astype(BF16) for s in s_cur]
        v_new = [(sol[u][:, :hd] - _dot(sol[u][:, hd:].astype(BF16), s_b[h])).astype(BF16)
                 for h, u in enumerate(ids)]
        o = [_dot(q_dec[u], s_b[h]) + _dot(attn[u], v_new[h]) for h, u in enumerate(ids)]
        s_cur = [s_cur[h] * jnp.exp(g_last[u]) + _dot_tn(k_dec[u], v_new[h])
                 for h, u in enumerate(ids)]
        for h in range(heads):
            zf = z_ref[rows[j], h * hd:(h + 1) * hd].astype(F32)
            o_n = (o[h] * lax.rsqrt(jnp.mean(o[h] * o[h], axis=-1, keepdims=True) + NORM_EPS)
                   * nw_ref[...])
            o_ref[rows[j], h * hd:(h + 1) * hd] = (o_n * (zf * _sigmoid(zf))).astype(o_ref.dtype)
    for h in range(heads):
        state[h] = s_cur[h]


def _gdn(qkv3, proj3, bg3, norm_w, heads, hd):
    bsz, lp, _ = qkv3.shape
    gd = heads * hd
    c = GDN_CHUNK * GDN_STEP_CHUNKS
    return pl.pallas_call(
        functools.partial(_gdn_kernel, heads=heads, hd=hd),
        grid=(bsz, lp // c),
        in_specs=[
            pl.BlockSpec((None, c, 3 * gd), lambda b, i: (b, i, 0)),
            pl.BlockSpec((None, c, gd), lambda b, i: (b, i, 3)),
            pl.BlockSpec((None, c, LANES), lambda b, i: (b, i, 0)),
            pl.BlockSpec((1, hd), lambda b, i: (0, 0)),
        ],
        out_specs=pl.BlockSpec((None, c, gd), lambda b, i: (b, i, 0)),
        out_shape=jax.ShapeDtypeStruct((bsz, lp, gd), BF16),
        scratch_shapes=[pltpu.VMEM((heads, hd, hd), F32)],
        compiler_params=_cparams(2),
        name="gdn",
    )(qkv3, proj3, bg3, norm_w)


def _sbk_kernel(q_ref, k_ref, v_ref, qw_ref, kw_ref, ow_ref, tri_ref, o_ref, qn_ref, kn_ref,
                acc_ref, drop_ref, *, front, hd):
    t = SB_BLOCK
    nb = q_ref.shape[0] // t

    def prep(i, _):
        rows = pl.ds(pl.multiple_of(i * t, t), t)
        qb = q_ref[rows, :].astype(F32)
        kb = k_ref[rows, :].astype(F32)
        qn = qb * lax.rsqrt(jnp.mean(qb * qb, axis=-1, keepdims=True) + NORM_EPS) * qw_ref[...]
        qn_ref[rows, :] = (qn * (hd ** -0.5 * LOG2E)).astype(BF16)
        kn = kb * lax.rsqrt(jnp.mean(kb * kb, axis=-1, keepdims=True) + NORM_EPS) * kw_ref[...]
        kn_ref[rows, :] = kn.astype(BF16)
        acc_ref[rows, :] = jnp.zeros((t, hd), F32)
        drop_ref[rows, :] = jnp.zeros((t, LANES), F32)
        return 0
    lax.fori_loop(0, nb, prep, 0)

    def mask_of(kind):
        r_i = lax.broadcasted_iota(I32, (t, t), 0)
        c_i = lax.broadcasted_iota(I32, (t, t), 1)
        if kind == "causal":
            return c_i < r_i
        if kind == "front":
            return c_i >= front
        return (c_i < r_i) & (c_i >= front)

    def scores(qi, kj, kind):
        qrows = pl.ds(pl.multiple_of(qi * t, t), t)
        krows = pl.ds(pl.multiple_of(kj * t, t), t)
        s = _dot_nt(qn_ref[qrows, :], kn_ref[krows, :])
        neg_abs = pltpu.bitcast(pltpu.bitcast(s, U32) | jnp.uint32(SIGN_BIT), F32)
        drop = jnp.maximum(s, 0.0) + jnp.log2(1.0 + jnp.exp2(neg_abs))
        visible = None
        if kind is not None:
            visible = mask_of(kind)
            drop = jnp.where(visible, drop, 0.0)
        tail = _dot(drop.astype(BF16), tri_ref[...])
        return s, tail, visible

    def finish(tile, part):
        (qi, kj, kind), (s, tail, visible) = tile, part
        qrows = pl.ds(pl.multiple_of(qi * t, t), t)
        krows = pl.ds(pl.multiple_of(kj * t, t), t)
        dropped = drop_ref[qrows, :]
        weight = jnp.exp2(s - tail - jnp.concatenate([dropped] * (t // LANES), axis=1))
        if kind is not None:
            weight = jnp.where(visible, weight, 0.0)
        acc_ref[qrows, :] += _dot(weight.astype(BF16), v_ref[krows, :])
        drop_ref[qrows, :] = dropped + jnp.broadcast_to(tail[:, 0:1], dropped.shape)

    def run(tiles):
        pending = None
        for tile in tiles:
            part = scores(*tile)
            if pending is not None:
                finish(*pending)
            pending = (tile, part)
        finish(*pending)

    group = SB_GROUP

    def key_block(kj, diag_kind, plain_kind):
        n_tiles = nb - kj
        n_head = (n_tiles - 1) % group + 1
        for h in range(1, group + 1):
            @pl.when(n_head == h)
            def _():
                run([(kj, kj, diag_kind)] + [(kj + m, kj, plain_kind) for m in range(1, h)])

        def many(p, _):
            base = kj + n_head + group * p
            run([(base + m, kj, plain_kind) for m in range(group)])
            return 0
        lax.fori_loop(0, (n_tiles - n_head) // group, many, 0)

    def later_blocks(i, _):
        key_block(nb - 1 - i, "causal", None)
        return 0
    lax.fori_loop(0, nb - 1, later_blocks, 0)
    key_block(jnp.int32(0), "causal_front", "front")

    def write(i, _):
        rows = pl.ds(pl.multiple_of(i * t, t), t)
        o = acc_ref[rows, :]
        o = o * lax.rsqrt(jnp.mean(o * o, axis=-1, keepdims=True) + NORM_EPS) * ow_ref[...]
        o_ref[rows, :] = o.astype(o_ref.dtype)
        return 0
    lax.fori_loop(0, nb, write, 0)


def _sbk(proj3, qw, kw, ow, heads, hd, col0, front):
    bsz, lp, _ = proj3.shape
    t = SB_BLOCK
    cb = col0 // hd
    tri = (jnp.arange(t)[:, None] >= jnp.arange(t)[None, :]).astype(BF16)
    return pl.pallas_call(
        functools.partial(_sbk_kernel, front=front, hd=hd),
        grid=(bsz, heads),
        in_specs=[
            pl.BlockSpec((None, lp, hd), lambda b, h: (b, 0, cb + h)),
            pl.BlockSpec((None, lp, hd), lambda b, h: (b, 0, cb + heads + h)),
            pl.BlockSpec((None, lp, hd), lambda b, h: (b, 0, cb + 2 * heads + h)),
            pl.BlockSpec((1, hd), lambda b, h: (0, 0)),
            pl.BlockSpec((1, hd), lambda b, h: (0, 0)),
            pl.BlockSpec((1, hd), lambda b, h: (0, 0)),
            pl.BlockSpec(tri.shape, lambda b, h: (0, 0)),
        ],
        out_specs=pl.BlockSpec((None, lp, hd), lambda b, h: (b, 0, h)),
        out_shape=jax.ShapeDtypeStruct((bsz, lp, heads * hd), BF16),
        scratch_shapes=[pltpu.VMEM((lp, hd), BF16), pltpu.VMEM((lp, hd), BF16),
                        pltpu.VMEM((lp, hd), F32), pltpu.VMEM((lp, LANES), F32)],
        compiler_params=_cparams(2),
        name="sb",
    )(proj3, proj3, proj3, qw, kw, ow, tri)


def _mix_kernel(og_ref, os_ref, h_ref, wo_ref, nw_ref, rw_ref, rb_ref,
                h1_ref, hnp_ref, info_ref, gate_ref, cnt_ref, cnt_acc,
                *, front, lp, bsz, pack_rows):
    tm = MIX_ROWS
    i = pl.program_id(0)

    @pl.when(i == 0)
    def _():
        cnt_acc[...] = jnp.zeros_like(cnt_acc)

    gd = og_ref.shape[1]
    h1 = h_ref[...] + _dot(og_ref[...], wo_ref[0:gd, :]) + _dot(os_ref[...], wo_ref[gd:, :])
    h1_ref[...] = h1
    hn = h1 * lax.rsqrt(jnp.mean(h1 * h1, axis=-1, keepdims=True) + NORM_EPS) * nw_ref[...]

    half = hn.shape[1] // 2
    lo = pltpu.bitcast(hn[:, :half].astype(BF16).astype(F32), U32) >> 16
    hi = pltpu.bitcast(hn[:, half:].astype(BF16).astype(F32), U32) & jnp.uint32(HIGH16)
    word = lo | hi
    for s in range(pack_rows):
        hnp_ref[pl.ds(s, tm, stride=pack_rows), :] = word[:, s * LANES:(s + 1) * LANES]

    logits = _dot3(hn, rw_ref[...]) + rb_ref[...]
    lane = lax.broadcasted_iota(I32, (tm, LANES), 1)
    row = lax.broadcasted_iota(I32, (tm, 1), 0)
    pos = i * tm + row
    valid = pos < 0
    for b in range(bsz):
        valid = valid | ((pos >= b * lp + front) & (pos < (b + 1) * lp))
    work = logits
    tops, idxs, hots = [], [], []
    for _ in range(TOP_K):
        m = jnp.max(work, axis=1, keepdims=True)
        idx = jnp.min(jnp.where(work == m, lane, LANES), axis=1, keepdims=True)
        hot = lane == idx
        tops.append(m)
        idxs.append(idx)
        hots.append(hot)
        work = jnp.where(hot, -jnp.inf, work)
    exps = [jnp.exp(m - tops[0]) for m in tops]
    denom = exps[0] + exps[1] + exps[2] + exps[3]
    sel = jnp.zeros((tm, LANES), F32)
    for hot in hots:
        sel = sel + hot.astype(F32)
    sel = jnp.where(valid, sel, 0.0)

    r_i = lax.broadcasted_iota(I32, (tm, tm), 0)
    c_i = lax.broadcasted_iota(I32, (tm, tm), 1)
    before = (r_i > c_i).astype(BF16)
    rank = cnt_acc[0:1, :] + _dot(before, sel.astype(BF16))
    info = jnp.zeros((tm, LANES), I32)
    gates = jnp.zeros((tm, LANES), F32)
    for j in range(TOP_K):
        rank_j = jnp.sum(jnp.where(hots[j], rank, 0.0), axis=1, keepdims=True).astype(I32)
        info = jnp.where(lane == j, idxs[j], info)
        info = jnp.where(lane == TOP_K + j, rank_j, info)
        gates = jnp.where(lane == j, exps[j] / denom, gates)
    info_ref[...] = info
    gate_ref[...] = gates
    cnt_acc[...] = cnt_acc[...] + jnp.sum(sel, axis=0, keepdims=True)
    cnt_ref[...] = cnt_acc[...]


def _mix(o_gdn, o_sb, h0, w_out, norm_w, router_w, router_b, front, lp):
    rows, d = h0.shape
    gd = o_gdn.shape[1]
    tm = MIX_ROWS
    pack_rows = d // 2 // LANES
    kern = functools.partial(_mix_kernel, front=front, lp=lp, bsz=rows // lp, pack_rows=pack_rows)
    return pl.pallas_call(
        kern,
        grid=(rows // tm,),
        in_specs=[
            pl.BlockSpec((tm, gd), lambda i: (i, 0)),
            pl.BlockSpec((tm, gd), lambda i: (i, 0)),
            pl.BlockSpec((tm, d), lambda i: (i, 0)),
            pl.BlockSpec(w_out.shape, lambda i: (0, 0)),
            pl.BlockSpec((1, d), lambda i: (0, 0)),
            pl.BlockSpec((d, LANES), lambda i: (0, 0)),
            pl.BlockSpec((1, LANES), lambda i: (0, 0)),
        ],
        out_specs=[
            pl.BlockSpec((tm, d), lambda i: (i, 0)),
            pl.BlockSpec((tm * pack_rows, LANES), lambda i: (i, 0)),
            pl.BlockSpec((tm, LANES), lambda i: (i, 0)),
            pl.BlockSpec((tm, LANES), lambda i: (i, 0)),
            pl.BlockSpec((8, LANES), lambda i: (0, 0)),
        ],
        out_shape=[
            jax.ShapeDtypeStruct((rows, d), F32),
            jax.ShapeDtypeStruct((rows * pack_rows, LANES), U32),
            jax.ShapeDtypeStruct((rows, LANES), I32),
            jax.ShapeDtypeStruct((rows, LANES), F32),
            jax.ShapeDtypeStruct((8, LANES), F32),
        ],
        scratch_shapes=[pltpu.VMEM((8, LANES), F32)],
        compiler_params=_cparams(1),
        name="mix_router",
    )(o_gdn, o_sb, h0, w_out, norm_w, router_w, router_b)


def _dispatch_kernel(pstart_ref, padded_ref, used_ref, slot_ref, hnp_ref, xs_ref, zbuf, sem, zsem,
                     *, pack_rows, front, blocks_per_batch, n_slots):
    tm = DISPATCH_ROWS
    ts = EXP_SUB
    i = pl.program_id(0)
    n_exp = pstart_ref.shape[0]

    @pl.when(i == 0)
    def _():
        zbuf[...] = jnp.zeros_like(zbuf)

        def zero_copy(row):
            return pltpu.make_async_copy(
                zbuf, xs_ref.at[pl.ds(row * pack_rows, ts * pack_rows), :], zsem)

        def pad_block(e, _, *, start):
            @pl.when(padded_ref[e] > 0)
            def _():
                cp = zero_copy(pstart_ref[e] + padded_ref[e] - ts)
                cp.start() if start else cp.wait()
            return 0

        used = used_ref[0]
        n_tail = (n_slots - used) // ts
        lax.fori_loop(0, n_exp, functools.partial(pad_block, start=True), 0)
        lax.fori_loop(0, n_tail, lambda j, c: (zero_copy(used + j * ts).start(), c)[1], 0)
        lax.fori_loop(0, n_exp, functools.partial(pad_block, start=False), 0)
        lax.fori_loop(0, n_tail, lambda j, c: (zero_copy(used + j * ts).wait(), c)[1], 0)

    def copy(tok, j):
        src = pl.multiple_of(tok * pack_rows, pack_rows)
        return pltpu.make_async_copy(
            hnp_ref.at[pl.ds(src, pack_rows), :],
            xs_ref.at[pl.ds(slot_ref[0, tok * TOP_K + j] * pack_rows, pack_rows), :], sem)

    def start(tok, _):
        for j in range(TOP_K):
            copy(tok, j).start(priority=j % 2)
        return 0

    def wait_all(n_tok):
        for _ in range(TOP_K):
            pltpu.make_async_copy(
                hnp_ref.at[pl.ds(0, n_tok * pack_rows), :],
                xs_ref.at[pl.ds(0, n_tok * pack_rows), :], sem).wait()

    @pl.when(i % blocks_per_batch == 0)
    def _():
        lax.fori_loop(front, tm, start, 0)
        wait_all(tm - front)

    @pl.when(i % blocks_per_batch != 0)
    def _():
        lax.fori_loop(0, tm, start, 0)
        wait_all(tm)


def _dispatch(slots, hnp, pstart, padded, used_rows, n_slots, pack_rows, front, lp):
    rows = slots.shape[0]
    tm = DISPATCH_ROWS
    slots3 = slots.reshape(rows // tm, 1, tm * TOP_K)
    grid_spec = pltpu.PrefetchScalarGridSpec(
        num_scalar_prefetch=3,
        grid=(rows // tm,),
        in_specs=[
            pl.BlockSpec((None, 1, tm * TOP_K), lambda i, *_: (i, 0, 0), memory_space=pltpu.SMEM),
            pl.BlockSpec((tm * pack_rows, LANES), lambda i, *_: (i, 0)),
        ],
        out_specs=pl.BlockSpec(memory_space=pl.ANY),
        scratch_shapes=[pltpu.VMEM((EXP_SUB * pack_rows, LANES), U32),
                        pltpu.SemaphoreType.DMA(()), pltpu.SemaphoreType.DMA(())],
    )
    return pl.pallas_call(
        functools.partial(_dispatch_kernel, pack_rows=pack_rows, front=front,
                          blocks_per_batch=lp // tm, n_slots=n_slots),
        grid_spec=grid_spec,
        out_shape=jax.ShapeDtypeStruct((n_slots * pack_rows, LANES), U32),
        compiler_params=_cparams(1),
        name="dispatch",
    )(pstart, padded, used_rows, slots3, hnp)


def _expert_kernel(ue_ref, ur_ref, un_ref, used_ref, xs_ref, wg_ref, wu_ref, bg_ref, bu_ref, wd_ref,
                   bd_ref, ys_ref, xbuf, xb, acc, ystage, sem_in, sem_out,
                   *, pack_rows, n_slots):
    del ue_ref
    ts = EXP_SUB
    u = pl.program_id(0)
    f = pl.program_id(1)
    n_f = pl.num_programs(1)
    nsub = un_ref[u]
    row0 = ur_ref[u]
    d = xb.shape[1]
    half = d // 2

    @pl.when((u == 0) & (f == 0))
    def _():
        ystage[...] = jnp.zeros_like(ystage)
        used = used_ref[0]

        def tail_copy(i):
            return pltpu.make_async_copy(
                ystage.at[0], ys_ref.at[pl.ds((used + i * ts) * pack_rows, ts * pack_rows), :],
                sem_out.at[0])

        n_tail = (n_slots - used) // ts
        lax.fori_loop(0, n_tail, lambda i, c: (tail_copy(i).start(), c)[1], 0)
        lax.fori_loop(0, n_tail, lambda i, c: (tail_copy(i).wait(), c)[1], 0)

    def in_copy(unit_row0, s):
        return pltpu.make_async_copy(
            xs_ref.at[pl.ds((unit_row0 + s * ts) * pack_rows, ts * pack_rows), :],
            xbuf.at[s], sem_in.at[s])

    def start_loads(unit):
        unit_row0 = ur_ref[unit]
        lax.fori_loop(0, un_ref[unit], lambda s, c: (in_copy(unit_row0, s).start(), c)[1], 0)

    @pl.when((u == 0) & (f == 0))
    def _():
        start_loads(0)

    @pl.when((f == 0) & (nsub > 0))
    def _():
        def load(s, _):
            in_copy(row0, s).wait()
            words = jnp.concatenate(
                [xbuf[s, pl.ds(c, ts, stride=pack_rows), :] for c in range(pack_rows)], axis=1)
            r = pl.multiple_of(s * ts, ts)
            xb[pl.ds(r, ts), :half] = pltpu.bitcast(words << 16, F32).astype(BF16)
            xb[pl.ds(r, ts), half:] = pltpu.bitcast(words & jnp.uint32(HIGH16), F32).astype(BF16)
            acc[pl.ds(r, ts), :] = jnp.broadcast_to(bd_ref[...], (ts, d))
            return 0
        lax.fori_loop(0, nsub, load, 0)

        @pl.when(u + 1 < pl.num_programs(0))
        def _():
            start_loads(jnp.minimum(u + 1, pl.num_programs(0) - 1))

    @pl.when(nsub > 0)
    def _():
        def ffn_rows(r, m):
            x = xb[pl.ds(r, m), :]
            g = jnp.minimum(_dot(x, wg_ref[...].astype(BF16)) + bg_ref[...], SWIGLU_LIMIT)
            up = jnp.clip(_dot(x, wu_ref[...].astype(BF16)) + bu_ref[...],
                          -SWIGLU_LIMIT, SWIGLU_LIMIT)
            act = ((up + 1.0) * g * _sigmoid(SWIGLU_ALPHA * g)).astype(BF16)
            acc[pl.ds(r, m), :] += _dot(act, wd_ref[...].astype(BF16))

        group = EXP_GROUP

        def trip(p, _):
            ffn_rows(pl.multiple_of(p * (group * ts), ts), group * ts)
            return 0
        lax.fori_loop(0, nsub // group, trip, 0)
        for k in range(1, group):
            @pl.when(nsub % group == k)
            def _():
                ffn_rows(pl.multiple_of(nsub // group * (group * ts), ts), k * ts)

    n_stage = ystage.shape[0]

    def out_copy(s):
        return pltpu.make_async_copy(
            ystage.at[s % n_stage],
            ys_ref.at[pl.ds((row0 + s * ts) * pack_rows, ts * pack_rows), :],
            sem_out.at[s % n_stage])

    @pl.when((f == n_f - 1) & (nsub > 0))
    def _():
        def store(s, _):
            @pl.when(s >= n_stage)
            def _():
                out_copy(s - n_stage).wait()
            y = acc[pl.ds(pl.multiple_of(s * ts, ts), ts), :]
            lo = pltpu.bitcast(y[:, :half].astype(BF16).astype(F32), U32) >> 16
            hi = pltpu.bitcast(y[:, half:].astype(BF16).astype(F32), U32) & jnp.uint32(HIGH16)
            word = lo | hi
            for c in range(pack_rows):
                ystage[s % n_stage, pl.ds(c, ts, stride=pack_rows), :] = (
                    word[:, c * LANES:(c + 1) * LANES])
            out_copy(s).start()
            return 0
        lax.fori_loop(0, nsub, store, 0)
        for k in range(n_stage, 0, -1):
            @pl.when(nsub >= k)
            def _():
                out_copy(nsub - k).wait()


def _experts(unit_e, unit_row0, unit_nsub, used_rows, xs, w_gate_up, b_gate_up, w_down, b_down,
             n_slots, pack_rows):
    n_exp, d, two_de = w_gate_up.shape
    de = two_de // 2
    tf = EXP_FTILE
    n_f = de // tf
    n_units = unit_e.shape[0]
    last_f = n_f - 1

    def fidx(u, f, un):
        return jnp.where(un[u] > 0, f, last_f)

    grid_spec = pltpu.PrefetchScalarGridSpec(
        num_scalar_prefetch=4,
        grid=(n_units, n_f),
        in_specs=[
            pl.BlockSpec(memory_space=pl.ANY),
            pl.BlockSpec((None, d, tf), lambda u, f, ue, ur, un, us: (ue[u], 0, fidx(u, f, un))),
            pl.BlockSpec((None, d, tf),
                         lambda u, f, ue, ur, un, us: (ue[u], 0, n_f + fidx(u, f, un))),
            pl.BlockSpec((None, 1, tf), lambda u, f, ue, ur, un, us: (ue[u], 0, fidx(u, f, un))),
            pl.BlockSpec((None, 1, tf),
                         lambda u, f, ue, ur, un, us: (ue[u], 0, n_f + fidx(u, f, un))),
            pl.BlockSpec((None, tf, d), lambda u, f, ue, ur, un, us: (ue[u], fidx(u, f, un), 0)),
            pl.BlockSpec((None, 1, d), lambda u, f, ue, ur, un, us: (ue[u], 0, 0)),
        ],
        out_specs=pl.BlockSpec(memory_space=pl.ANY),
        scratch_shapes=[
            pltpu.VMEM((EXP_UNIT // EXP_SUB, EXP_SUB * pack_rows, LANES), U32),
            pltpu.VMEM((EXP_UNIT, d), BF16),
            pltpu.VMEM((EXP_UNIT, d), F32),
            pltpu.VMEM((EXP_OUT_STAGES, EXP_SUB * pack_rows, LANES), U32),
            pltpu.SemaphoreType.DMA((EXP_UNIT // EXP_SUB,)),
            pltpu.SemaphoreType.DMA((EXP_OUT_STAGES,)),
        ],
    )
    return pl.pallas_call(
        functools.partial(_expert_kernel, pack_rows=pack_rows, n_slots=n_slots),
        grid_spec=grid_spec,
        out_shape=jax.ShapeDtypeStruct((n_slots * pack_rows, LANES), U32),
        compiler_params=_cparams(2, vmem=EXPERT_VMEM_LIMIT),
        name="experts",
    )(unit_e, unit_row0, unit_nsub, used_rows, xs, w_gate_up, w_gate_up,
      b_gate_up.reshape(n_exp, 1, two_de), b_gate_up.reshape(n_exp, 1, two_de),
      w_down, b_down.reshape(n_exp, 1, d))


def _combine_kernel(slot_ref, gate_ref, h1_ref, ys_ref, o_ref, ybuf, sem, *, pack_rows):
    tc = COMB_ROWS
    half = h1_ref.shape[1] // 2

    def copy(tok, j):
        s = slot_ref[0, tok * TOP_K + j]
        dst = pl.multiple_of(tok * pack_rows, pack_rows)
        return pltpu.make_async_copy(
            ys_ref.at[pl.ds(s * pack_rows, pack_rows), :],
            ybuf.at[j, pl.ds(dst, pack_rows), :], sem)

    def start(tok, _):
        for j in range(TOP_K):
            copy(tok, j).start(priority=j % 2)
        return 0

    lax.fori_loop(0, tc, start, 0)
    for j in range(TOP_K):
        pltpu.make_async_copy(ys_ref.at[pl.ds(0, tc * pack_rows), :], ybuf.at[j], sem).wait()

    gates = gate_ref[...]
    for c in range(pack_rows):
        lo = h1_ref[:, c * LANES:(c + 1) * LANES]
        hi = h1_ref[:, half + c * LANES:half + (c + 1) * LANES]
        for j in range(TOP_K):
            word = ybuf[j, pl.ds(c, tc, stride=pack_rows), :]
            gate = gates[:, j:j + 1]
            lo = lo + gate * pltpu.bitcast(word << 16, F32)
            hi = hi + gate * pltpu.bitcast(word & jnp.uint32(HIGH16), F32)
        o_ref[:, c * LANES:(c + 1) * LANES] = lo
        o_ref[:, half + c * LANES:half + (c + 1) * LANES] = hi


def _combine(slots, gates, h1, ys, bsz, seq, lp, d):
    tc = COMB_ROWS
    rows = slots.shape[0]
    pack_rows = d // 2 // LANES
    slots3 = slots.reshape(rows // tc, 1, tc * TOP_K)
    nb = lp // tc
    first = (lp - seq) // tc
    return pl.pallas_call(
        functools.partial(_combine_kernel, pack_rows=pack_rows),
        grid=(bsz, seq // tc),
        in_specs=[
            pl.BlockSpec((None, 1, tc * TOP_K), lambda b, i: (b * nb + first + i, 0, 0),
                         memory_space=pltpu.SMEM),
            pl.BlockSpec((tc, LANES), lambda b, i: (b * nb + first + i, 0)),
            pl.BlockSpec((tc, d), lambda b, i: (b * nb + first + i, 0)),
            pl.BlockSpec(memory_space=pl.ANY),
        ],
        out_specs=pl.BlockSpec((None, tc, d), lambda b, i: (b, i, 0)),
        out_shape=jax.ShapeDtypeStruct((bsz, seq, d), F32),
        scratch_shapes=[pltpu.VMEM((TOP_K, tc * pack_rows, LANES), U32), pltpu.SemaphoreType.DMA(())],
        compiler_params=_cparams(2),
        name="combine",
    )(slots3, gates, h1, ys)


def _pick(n, candidates):
    for c in candidates:
        if n % c == 0:
            return c
    raise ValueError(f"no block size in {candidates} divides {n}")


def _plan(info, cnt, n_exp, rows, lp, front, n_assign):
    counts = cnt[0, :n_exp].astype(I32)
    padded = (counts + EXP_SUB - 1) // EXP_SUB * EXP_SUB
    pstart = jnp.cumsum(padded) - padded
    eid = info[:, :TOP_K]
    rank = info[:, TOP_K:2 * TOP_K]
    onehot = eid[:, :, None] == jnp.arange(n_exp, dtype=I32)[None, None, :]
    slot = rank + jnp.sum(jnp.where(onehot, pstart[None, None, :], 0), axis=-1)
    row_valid = (jnp.arange(rows, dtype=I32) % lp) >= front
    slots = jnp.where(row_valid[:, None], slot, -1).astype(I32)

    n_slots = (n_assign + n_exp * (EXP_SUB - 1)) // EXP_SUB * EXP_SUB
    units_per_e = (padded + EXP_UNIT - 1) // EXP_UNIT
    cum_units = jnp.cumsum(units_per_e)
    n_units = n_slots // EXP_UNIT + n_exp
    uidx = jnp.arange(n_units, dtype=I32)
    ue = jnp.sum(cum_units[None, :] <= uidx[:, None], axis=1).astype(I32)
    live = ue < n_exp
    last_e = jnp.max(jnp.where(counts > 0, jnp.arange(n_exp, dtype=I32), 0))
    ue = jnp.where(live, ue, last_e)
    k_in_e = uidx - (cum_units - units_per_e)[ue]
    nsub_e = (padded // EXP_SUB)[ue]
    units_e = jnp.maximum(units_per_e[ue], 1)
    base = nsub_e // units_e
    extra = nsub_e - base * units_e
    first_sub = k_in_e * base + jnp.minimum(k_in_e, extra)
    unit_row0 = jnp.where(live, pstart[ue] + first_sub * EXP_SUB, 0).astype(I32)
    unit_nsub = jnp.where(live, base + (k_in_e < extra), 0).astype(I32)
    used_rows = jnp.sum(padded).astype(I32)[None]
    return slots, (pstart.astype(I32), padded.astype(I32)), (ue, unit_row0, unit_nsub, used_rows), n_slots


def kernel(x, meta_tokens, mix_norm_w, w_in, conv_w, a_log, dt_bias, gdn_norm_w, sb_q_norm_w,
           sb_k_norm_w, sb_out_norm_w, w_out, ffn_norm_w, router_w, router_b, w_gate_up, b_gate_up,
           w_down, b_down):
    bsz, seq, d = x.shape
    n_meta = meta_tokens.shape[0]
    depth = mix_norm_w.shape[0]
    heads = a_log.shape[1]
    hd = gdn_norm_w.shape[1]
    gd = heads * hd
    sbd = (w_in.shape[2] - 4 * gd - 2 * heads) // 3
    sb_heads = sbd // hd
    n_exp = router_w.shape[2]
    assert seq % ROW_ALIGN == 0 and d % (2 * 8 * LANES) == 0 and hd == LANES
    assert 2 * heads <= LANES and n_exp <= LANES and sb_heads == heads
    assert depth == 1, "a second layer would need the meta rows carried through the combine stage"
    front = (-n_meta) % ROW_ALIGN
    lp = front + n_meta + seq
    rows = bsz * lp
    pack_rows = d // 2 // LANES

    h = jnp.concatenate([
        jnp.zeros((bsz, front, d), x.dtype),
        jnp.broadcast_to(meta_tokens.astype(x.dtype)[None], (bsz, n_meta, d)),
        x], axis=1).reshape(rows, d)

    wl = w_in[0]
    n_ba = 4 * gd
    w_main = jnp.concatenate([wl[:, :n_ba], wl[:, n_ba + 2 * heads:]], axis=1).astype(BF16)
    w_ba = jnp.pad(wl[:, n_ba:n_ba + 2 * heads], ((0, 0), (0, LANES - 2 * heads))).astype(BF16)
    tm = _pick(rows, (1024, 512, 256))
    tn = _pick(w_main.shape[1], (1792, 1024, 512, 256, 128))
    proj, ba = _in_proj(h, mix_norm_w[0][None], w_main, w_ba, tm, tn)
    proj3 = proj.reshape(bsz, lp, -1)
    ba3 = ba.reshape(bsz, lp, LANES)

    gparams = jnp.zeros((2, LANES), F32)
    gparams = gparams.at[0, heads:2 * heads].set(a_log[0].astype(F32))
    gparams = gparams.at[1, heads:2 * heads].set(dt_bias[0].astype(F32))
    qkv3, bg3 = _gdn_prep(proj3, ba3, conv_w[0].astype(F32), gparams, heads, hd)
    o_gdn = _gdn(qkv3, proj3, bg3, gdn_norm_w[0][None], heads, hd)
    o_sb = _sbk(proj3, sb_q_norm_w[0][None], sb_k_norm_w[0][None], sb_out_norm_w[0][None],
               sb_heads, hd, 4 * gd, front)

    rw = jnp.pad(router_w[0].astype(F32), ((0, 0), (0, LANES - n_exp)))
    rb = jnp.pad(router_b[0].astype(F32), (0, LANES - n_exp), constant_values=-1e30)[None]
    h1, hnp, info, gates, cnt = _mix(
        o_gdn.reshape(rows, gd), o_sb.reshape(rows, sbd), h, w_out[0].astype(BF16),
        ffn_norm_w[0][None], rw, rb, front, lp)

    slots, (pstart, padded), units, n_slots = _plan(
        info, cnt, n_exp, rows, lp, front, bsz * (n_meta + seq) * TOP_K)
    xs = _dispatch(slots, hnp, pstart, padded, units[3], n_slots, pack_rows, front, lp)
    ys = _experts(*units, xs, w_gate_up[0], b_gate_up[0], w_down[0], b_down[0], n_slots, pack_rows)
    return _combine(slots, gates, h1, ys, bsz, seq, lp, d)
```

```python
import functools

import jax
import jax.numpy as jnp
from jax import lax
from jax.experimental import pallas as pl
from jax.experimental.pallas import tpu as pltpu

F32 = jnp.float32
BF16 = jnp.bfloat16
U32 = jnp.uint32
I32 = jnp.int32

NORM_EPS = 1e-6
TOP_K = 4
SWIGLU_LIMIT = 7.0
SWIGLU_ALPHA = 1.702
LANES = 128
ROW_ALIGN = 256
GDN_PREP_ROWS = 256
GDN_CHUNK = 64
GDN_STEP_CHUNKS = 4
SB_BLOCK = 256
SB_GROUP = 8
MIX_ROWS = 512
DISPATCH_ROWS = 256
EXP_SUB = 256
EXP_GROUP = 3
EXP_UNIT = 9 * EXP_SUB
EXP_FTILE = 256
EXP_OUT_STAGES = 4
COMB_ROWS = 256
VMEM_LIMIT = 56 * 1024 * 1024
EXPERT_VMEM_LIMIT = 60 * 1024 * 1024
HIGH16 = 0xFFFF0000
SIGN_BIT = 0x80000000
LOG2E = 1.4426950408889634


def _cparams(n_grid, vmem=VMEM_LIMIT):
    return pltpu.CompilerParams(dimension_semantics=("arbitrary",) * n_grid, vmem_limit_bytes=vmem)


def _dot(a, b):
    return jnp.dot(a, b, preferred_element_type=F32)


def _dot_nt(a, b):
    return lax.dot_general(a, b, (((1,), (1,)), ((), ())), preferred_element_type=F32)


def _dot_tn(a, b):
    return lax.dot_general(a, b, (((0,), (0,)), ((), ())), preferred_element_type=F32)


def _split3(x):
    hi = x.astype(BF16)
    r1 = x - hi.astype(F32)
    mid = r1.astype(BF16)
    lo = (r1 - mid.astype(F32)).astype(BF16)
    return hi, mid, lo


def _dot3(a, b):
    ah = a.astype(BF16)
    al = (a - ah.astype(F32)).astype(BF16)
    bh = b.astype(BF16)
    bl = (b - bh.astype(F32)).astype(BF16)
    return _dot(ah, bh) + (_dot(ah, bl) + _dot(al, bh))


def _sigmoid(x):
    return 1.0 / (1.0 + jnp.exp(-x))


def _softplus(x):
    return jnp.maximum(x, 0.0) + jnp.log1p(jnp.exp(-jnp.abs(x)))


def _inproj_kernel(h_ref, nw_ref, w_ref, wba_ref, o_ref, ba_ref, xn_ref):
    @pl.when(pl.program_id(1) == 0)
    def _():
        x = h_ref[...]
        xn = x * lax.rsqrt(jnp.mean(x * x, axis=-1, keepdims=True) + NORM_EPS) * nw_ref[...]
        xn = xn.astype(BF16)
        xn_ref[...] = xn
        ba_ref[...] = _dot(xn, wba_ref[...])

    o_ref[...] = _dot(xn_ref[...], w_ref[...]).astype(o_ref.dtype)


def _in_proj(h0, norm_w, w_main, w_ba, tm, tn):
    rows, d = h0.shape
    n_main = w_main.shape[1]
    return pl.pallas_call(
        _inproj_kernel,
        grid=(rows // tm, n_main // tn),
        in_specs=[
            pl.BlockSpec((tm, d), lambda i, n: (i, 0)),
            pl.BlockSpec((1, d), lambda i, n: (0, 0)),
            pl.BlockSpec((d, tn), lambda i, n: (0, n)),
            pl.BlockSpec((d, LANES), lambda i, n: (0, 0)),
        ],
        out_specs=[
            pl.BlockSpec((tm, tn), lambda i, n: (i, n)),
            pl.BlockSpec((tm, LANES), lambda i, n: (i, 0)),
        ],
        out_shape=[
            jax.ShapeDtypeStruct((rows, n_main), BF16),
            jax.ShapeDtypeStruct((rows, LANES), F32),
        ],
        scratch_shapes=[pltpu.VMEM((tm, d), BF16)],
        compiler_params=_cparams(2),
        name="in_proj",
    )(h0, norm_w, w_main, w_ba)


def _gdn_prep_kernel(x_ref, prev_ref, ba_ref, cw_ref, gp_ref, o_ref, bg_ref, *, heads, hd):
    tp = x_ref.shape[0]
    gd = heads * hd
    taps = cw_ref.shape[0]
    x = x_ref[...].astype(F32)
    prev = jnp.where(pl.program_id(1) > 0, prev_ref[8:16, :].astype(F32), 0.0)
    xs = jnp.concatenate([prev, x], axis=0)
    y = cw_ref[taps - 1:taps, :] * x
    for s in range(1, taps):
        y = y + cw_ref[taps - 1 - s:taps - s, :] * pltpu.roll(xs, s, axis=0)[8:8 + tp, :]
    y = y * _sigmoid(y)
    for h in range(heads):
        q = y[:, h * hd:(h + 1) * hd]
        k = y[:, gd + h * hd:gd + (h + 1) * hd]
        q = q * (lax.rsqrt(jnp.sum(q * q, axis=-1, keepdims=True) + NORM_EPS) * (hd ** -0.5))
        k = k * lax.rsqrt(jnp.sum(k * k, axis=-1, keepdims=True) + NORM_EPS)
        o_ref[:, h * hd:(h + 1) * hd] = q.astype(o_ref.dtype)
        o_ref[:, gd + h * hd:gd + (h + 1) * hd] = k.astype(o_ref.dtype)
    o_ref[:, 2 * gd:] = y[:, 2 * gd:].astype(o_ref.dtype)
    ba = ba_ref[...]
    lane = lax.broadcasted_iota(I32, ba.shape, 1)
    decay = -jnp.exp(gp_ref[0:1, :]) * _softplus(ba + gp_ref[1:2, :])
    bg_ref[...] = jnp.where(lane < heads, _sigmoid(ba), decay)


def _gdn_prep(proj3, ba3, conv_w, gparams, heads, hd):
    bsz, lp, _ = proj3.shape
    gd = heads * hd
    tp = GDN_PREP_ROWS
    return pl.pallas_call(
        functools.partial(_gdn_prep_kernel, heads=heads, hd=hd),
        grid=(bsz, lp // tp),
        in_specs=[
            pl.BlockSpec((None, tp, 3 * gd), lambda b, i: (b, i, 0)),
            pl.BlockSpec((None, 16, 3 * gd), lambda b, i: (b, jnp.maximum(i * (tp // 16) - 1, 0), 0)),
            pl.BlockSpec((None, tp, LANES), lambda b, i: (b, i, 0)),
            pl.BlockSpec(conv_w.shape, lambda b, i: (0, 0)),
            pl.BlockSpec((2, LANES), lambda b, i: (0, 0)),
        ],
        out_specs=[
            pl.BlockSpec((None, tp, 3 * gd), lambda b, i: (b, i, 0)),
            pl.BlockSpec((None, tp, LANES), lambda b, i: (b, i, 0)),
        ],
        out_shape=[
            jax.ShapeDtypeStruct((bsz, lp, 3 * gd), BF16),
            jax.ShapeDtypeStruct((bsz, lp, LANES), F32),
        ],
        compiler_params=_cparams(2),
        name="gdn_prep",
    )(proj3, proj3, ba3, conv_w, gparams)


def _gdn_kernel(qkv_ref, z_ref, bg_ref, nw_ref, o_ref, state, *, heads, hd):
    c_rows = GDN_CHUNK
    gd = heads * hd

    @pl.when(pl.program_id(1) == 0)
    def _():
        state[...] = jnp.zeros_like(state)

    n_chunks = qkv_ref.shape[0] // c_rows
    r_i = lax.broadcasted_iota(I32, (c_rows, c_rows), 0)
    c_i = lax.broadcasted_iota(I32, (c_rows, c_rows), 1)
    incl = r_i >= c_i
    strict = r_i > c_i
    tri = incl.astype(BF16)
    n_sel = -(-heads // 8) * 8
    pick = (lax.broadcasted_iota(I32, (n_sel, LANES), 1)
            == lax.broadcasted_iota(I32, (n_sel, LANES), 0) + heads).astype(BF16)

    rows = [slice(j * c_rows, (j + 1) * c_rows) for j in range(n_chunks)]
    bg, gcum, gcum_rows = [], [], []
    for j in range(n_chunks):
        bg.append(bg_ref[rows[j], :])
        g_parts = _split3(bg[j])
        gcum.append(_dot(tri, g_parts[0]) + _dot(tri, g_parts[1]) + _dot(tri, g_parts[2]))
        c_parts = _split3(gcum[j])
        gcum_rows.append(_dot_nt(pick, c_parts[0]) + _dot_nt(pick, c_parts[1])
                         + _dot_nt(pick, c_parts[2]))

    units = [(j, h) for j in range(n_chunks) for h in range(heads)]
    us = range(len(units))
    q16 = [qkv_ref[rows[j], h * hd:(h + 1) * hd] for j, h in units]
    k16 = [qkv_ref[rows[j], gd + h * hd:gd + (h + 1) * hd] for j, h in units]
    k = [k16[u].astype(F32) for u in us]
    gc = [gcum[j][:, heads + h:heads + h + 1] for j, h in units]
    decay = [jnp.exp(jnp.minimum(gc[u] - gcum_rows[j][h:h + 1, :], 0.0))
             for u, (j, h) in enumerate(units)]
    kb = [k[u] * bg[j][:, h:h + 1] for u, (j, h) in enumerate(units)]
    p = [jnp.where(strict, _dot_nt(kb[u].astype(BF16), k16[u]) * decay[u], 0.0) for u in us]
    p = [(-p[u]).astype(BF16) for u in us]
    sol = [jnp.concatenate(
        [qkv_ref[rows[j], 2 * gd + h * hd:2 * gd + (h + 1) * hd].astype(F32) * bg[j][:, h:h + 1],
         kb[u] * jnp.exp(gc[u])], axis=1) for u, (j, h) in enumerate(units)]
    n_fac = c_rows.bit_length() - 1
    for i in range(n_fac):
        sol = [sol[u] + _dot(p[u], sol[u].astype(BF16)) for u in us]
        if i + 1 < n_fac:
            p = [_dot(p[u], p[u]).astype(BF16) for u in us]
    attn = [jnp.where(incl, _dot_nt(q16[u], k16[u]) * decay[u], 0.0).astype(BF16) for u in us]
    q_dec = [(q16[u].astype(F32) * jnp.exp(gc[u])).astype(BF16) for u in us]
    g_last = [gc[u][c_rows - 1:c_rows, :] for u in us]
    k_dec = [(k[u] * jnp.exp(g_last[u] - gc[u])).astype(BF16) for u in us]

    s_cur = [state[h] for h in range(heads)]
    for j in range(n_chunks):
        ids = [j * heads + h for h in range(heads)]
        s_b = [s.astype(BF16) for s in s_cur]
        v_new = [(sol[u][:, :hd] - _dot(sol[u][:, hd:].astype(BF16), s_b[h])).astype(BF16)
                 for h, u in enumerate(ids)]
        o = [_dot(q_dec[u], s_b[h]) + _dot(attn[u], v_new[h]) for h, u in enumerate(ids)]
        s_cur = [s_cur[h] * jnp.exp(g_last[u]) + _dot_tn(k_dec[u], v_new[h])
                 for h, u in enumerate(ids)]
        for h in range(heads):
            zf = z_ref[rows[j], h * hd:(h + 1) * hd].astype(F32)
            o_n = (o[h] * lax.rsqrt(jnp.mean(o[h] * o[h], axis=-1, keepdims=True) + NORM_EPS)
                   * nw_ref[...])
            o_ref[rows[j], h * hd:(h + 1) * hd] = (o_n * (zf * _sigmoid(zf))).astype(o_ref.dtype)
    for h in range(heads):
        state[h] = s_cur[h]


def _gdn(qkv3, proj3, bg3, norm_w, heads, hd):
    bsz, lp, _ = qkv3.shape
    gd = heads * hd
    c = GDN_CHUNK * GDN_STEP_CHUNKS
    return pl.pallas_call(
        functools.partial(_gdn_kernel, heads=heads, hd=hd),
        grid=(bsz, lp // c),
        in_specs=[
            pl.BlockSpec((None, c, 3 * gd), lambda b, i: (b, i, 0)),
            pl.BlockSpec((None, c, gd), lambda b, i: (b, i, 3)),
            pl.BlockSpec((None, c, LANES), lambda b, i: (b, i, 0)),
            pl.BlockSpec((1, hd), lambda b, i: (0, 0)),
        ],
        out_specs=pl.BlockSpec((None, c, gd), lambda b, i: (b, i, 0)),
        out_shape=jax.ShapeDtypeStruct((bsz, lp, gd), BF16),
        scratch_shapes=[pltpu.VMEM((heads, hd, hd), F32)],
        compiler_params=_cparams(2),
        name="gdn",
    )(qkv3, proj3, bg3, norm_w)


def _sbk_kernel(q_ref, k_ref, v_ref, qw_ref, kw_ref, ow_ref, tri_ref, o_ref, qn_ref, kn_ref,
                acc_ref, drop_ref, *, front, hd):
    t = SB_BLOCK
    nb = q_ref.shape[0] // t

    def prep(i, _):
        rows = pl.ds(pl.multiple_of(i * t, t), t)
        qb = q_ref[rows, :].astype(F32)
        kb = k_ref[rows, :].astype(F32)
        qn = qb * lax.rsqrt(jnp.mean(qb * qb, axis=-1, keepdims=True) + NORM_EPS) * qw_ref[...]
        qn_ref[rows, :] = (qn * (hd ** -0.5 * LOG2E)).astype(BF16)
        kn = kb * lax.rsqrt(jnp.mean(kb * kb, axis=-1, keepdims=True) + NORM_EPS) * kw_ref[...]
        kn_ref[rows, :] = kn.astype(BF16)
        acc_ref[rows, :] = jnp.zeros((t, hd), F32)
        drop_ref[rows, :] = jnp.zeros((t, LANES), F32)
        return 0
    lax.fori_loop(0, nb, prep, 0)

    skip = front // LANES * LANES

    def first_col(kind):
        return skip if kind in ("front", "causal_front") else 0

    def key_rows(kj, kind):
        c0 = first_col(kind)
        return c0, pl.ds(pl.multiple_of(kj * t, t) + c0, t - c0)

    def mask_of(kind):
        c0 = first_col(kind)
        r_i = lax.broadcasted_iota(I32, (t, t - c0), 0)
        c_i = lax.broadcasted_iota(I32, (t, t - c0), 1) + c0
        if kind == "causal":
            return c_i < r_i
        if kind == "front":
            return c_i >= front
        return (c_i < r_i) & (c_i >= front)

    def scores(qi, kj, kind):
        qrows = pl.ds(pl.multiple_of(qi * t, t), t)
        c0, krows = key_rows(kj, kind)
        s = _dot_nt(qn_ref[qrows, :], kn_ref[krows, :])
        neg_abs = pltpu.bitcast(pltpu.bitcast(s, U32) | jnp.uint32(SIGN_BIT), F32)
        drop = jnp.maximum(s, 0.0) + jnp.log2(1.0 + jnp.exp2(neg_abs))
        visible = None
        if kind is not None:
            visible = mask_of(kind)
            drop = jnp.where(visible, drop, 0.0)
        tail = _dot(drop.astype(BF16), tri_ref[0:t - c0, 0:t - c0])
        return s, tail, visible

    def finish(tile, part):
        (qi, kj, kind), (s, tail, visible) = tile, part
        qrows = pl.ds(pl.multiple_of(qi * t, t), t)
        c0, krows = key_rows(kj, kind)
        dropped = drop_ref[qrows, :]
        weight = jnp.exp2(s - tail - jnp.concatenate([dropped] * ((t - c0) // LANES), axis=1))
        if kind is not None:
            weight = jnp.where(visible, weight, 0.0)
        acc_ref[qrows, :] += _dot(weight.astype(BF16), v_ref[krows, :])
        drop_ref[qrows, :] = dropped + jnp.broadcast_to(tail[:, 0:1], dropped.shape)

    def run(tiles):
        pending = None
        for tile in tiles:
            part = scores(*tile)
            if pending is not None:
                finish(*pending)
            pending = (tile, part)
        finish(*pending)

    group = SB_GROUP

    def key_block(kj, diag_kind, plain_kind):
        n_tiles = nb - kj
        n_head = (n_tiles - 1) % group + 1
        for h in range(1, group + 1):
            @pl.when(n_head == h)
            def _():
                run([(kj, kj, diag_kind)] + [(kj + m, kj, plain_kind) for m in range(1, h)])

        def many(p, _):
            base = kj + n_head + group * p
            run([(base + m, kj, plain_kind) for m in range(group)])
            return 0
        lax.fori_loop(0, (n_tiles - n_head) // group, many, 0)

    def later_blocks(i, _):
        key_block(nb - 1 - i, "causal", None)
        return 0
    lax.fori_loop(0, nb - 1, later_blocks, 0)
    key_block(jnp.int32(0), "causal_front", "front")

    def write(i, _):
        rows = pl.ds(pl.multiple_of(i * t, t), t)
        o = acc_ref[rows, :]
        o = o * lax.rsqrt(jnp.mean(o * o, axis=-1, keepdims=True) + NORM_EPS) * ow_ref[...]
        o_ref[rows, :] = o.astype(o_ref.dtype)
        return 0
    lax.fori_loop(0, nb, write, 0)


def _sbk(proj3, qw, kw, ow, heads, hd, col0, front):
    bsz, lp, _ = proj3.shape
    t = SB_BLOCK
    cb = col0 // hd
    tri = (jnp.arange(t)[:, None] >= jnp.arange(t)[None, :]).astype(BF16)
    return pl.pallas_call(
        functools.partial(_sbk_kernel, front=front, hd=hd),
        grid=(bsz, heads),
        in_specs=[
            pl.BlockSpec((None, lp, hd), lambda b, h: (b, 0, cb + h)),
            pl.BlockSpec((None, lp, hd), lambda b, h: (b, 0, cb + heads + h)),
            pl.BlockSpec((None, lp, hd), lambda b, h: (b, 0, cb + 2 * heads + h)),
            pl.BlockSpec((1, hd), lambda b, h: (0, 0)),
            pl.BlockSpec((1, hd), lambda b, h: (0, 0)),
            pl.BlockSpec((1, hd), lambda b, h: (0, 0)),
            pl.BlockSpec(tri.shape, lambda b, h: (0, 0)),
        ],
        out_specs=pl.BlockSpec((None, lp, hd), lambda b, h: (b, 0, h)),
        out_shape=jax.ShapeDtypeStruct((bsz, lp, heads * hd), BF16),
        scratch_shapes=[pltpu.VMEM((lp, hd), BF16), pltpu.VMEM((lp, hd), BF16),
                        pltpu.VMEM((lp, hd), F32), pltpu.VMEM((lp, LANES), F32)],
        compiler_params=_cparams(2),
        name="sb",
    )(proj3, proj3, proj3, qw, kw, ow, tri)


def _mix_kernel(og_ref, os_ref, h_ref, wo_ref, nw_ref, rw_ref, rb_ref,
                h1_ref, hnp_ref, info_ref, gate_ref, cnt_ref, cnt_acc,
                *, front, lp, bsz, pack_rows):
    tm = MIX_ROWS
    i = pl.program_id(0)

    @pl.when(i == 0)
    def _():
        cnt_acc[...] = jnp.zeros_like(cnt_acc)

    gd = og_ref.shape[1]
    h1 = h_ref[...] + _dot(og_ref[...], wo_ref[0:gd, :]) + _dot(os_ref[...], wo_ref[gd:, :])
    h1_ref[...] = h1
    hn = h1 * lax.rsqrt(jnp.mean(h1 * h1, axis=-1, keepdims=True) + NORM_EPS) * nw_ref[...]

    half = hn.shape[1] // 2
    lo = pltpu.bitcast(hn[:, :half].astype(BF16).astype(F32), U32) >> 16
    hi = pltpu.bitcast(hn[:, half:].astype(BF16).astype(F32), U32) & jnp.uint32(HIGH16)
    word = lo | hi
    for s in range(pack_rows):
        hnp_ref[pl.ds(s, tm, stride=pack_rows), :] = word[:, s * LANES:(s + 1) * LANES]

    logits = _dot3(hn, rw_ref[...]) + rb_ref[...]
    lane = lax.broadcasted_iota(I32, (tm, LANES), 1)
    row = lax.broadcasted_iota(I32, (tm, 1), 0)
    pos = i * tm + row
    valid = pos < 0
    for b in range(bsz):
        valid = valid | ((pos >= b * lp + front) & (pos < (b + 1) * lp))
    work = logits
    tops, idxs, hots = [], [], []
    for _ in range(TOP_K):
        m = jnp.max(work, axis=1, keepdims=True)
        idx = jnp.min(jnp.where(work == m, lane, LANES), axis=1, keepdims=True)
        hot = lane == idx
        tops.append(m)
        idxs.append(idx)
        hots.append(hot)
        work = jnp.where(hot, -jnp.inf, work)
    exps = [jnp.exp(m - tops[0]) for m in tops]
    denom = exps[0] + exps[1] + exps[2] + exps[3]
    sel = jnp.zeros((tm, LANES), F32)
    for hot in hots:
        sel = sel + hot.astype(F32)
    sel = jnp.where(valid, sel, 0.0)

    r_i = lax.broadcasted_iota(I32, (tm, tm), 0)
    c_i = lax.broadcasted_iota(I32, (tm, tm), 1)
    before = (r_i > c_i).astype(BF16)
    rank = cnt_acc[0:1, :] + _dot(before, sel.astype(BF16))
    info = jnp.zeros((tm, LANES), I32)
    gates = jnp.zeros((tm, LANES), F32)
    for j in range(TOP_K):
        rank_j = jnp.sum(jnp.where(hots[j], rank, 0.0), axis=1, keepdims=True).astype(I32)
        info = jnp.where(lane == j, idxs[j], info)
        info = jnp.where(lane == TOP_K + j, rank_j, info)
        gates = jnp.where(lane == j, exps[j] / denom, gates)
    info_ref[...] = info
    gate_ref[...] = gates
    cnt_acc[...] = cnt_acc[...] + jnp.sum(sel, axis=0, keepdims=True)
    cnt_ref[...] = cnt_acc[...]


def _mix(o_gdn, o_sb, h0, w_out, norm_w, router_w, router_b, front, lp):
    rows, d = h0.shape
    gd = o_gdn.shape[1]
    tm = MIX_ROWS
    pack_rows = d // 2 // LANES
    kern = functools.partial(_mix_kernel, front=front, lp=lp, bsz=rows // lp, pack_rows=pack_rows)
    return pl.pallas_call(
        kern,
        grid=(rows // tm,),
        in_specs=[
            pl.BlockSpec((tm, gd), lambda i: (i, 0)),
            pl.BlockSpec((tm, gd), lambda i: (i, 0)),
            pl.BlockSpec((tm, d), lambda i: (i, 0)),
            pl.BlockSpec(w_out.shape, lambda i: (0, 0)),
            pl.BlockSpec((1, d), lambda i: (0, 0)),
            pl.BlockSpec((d, LANES), lambda i: (0, 0)),
            pl.BlockSpec((1, LANES), lambda i: (0, 0)),
        ],
        out_specs=[
            pl.BlockSpec((tm, d), lambda i: (i, 0)),
            pl.BlockSpec((tm * pack_rows, LANES), lambda i: (i, 0)),
            pl.BlockSpec((tm, LANES), lambda i: (i, 0)),
            pl.BlockSpec((tm, LANES), lambda i: (i, 0)),
            pl.BlockSpec((8, LANES), lambda i: (0, 0)),
        ],
        out_shape=[
            jax.ShapeDtypeStruct((rows, d), F32),
            jax.ShapeDtypeStruct((rows * pack_rows, LANES), U32),
            jax.ShapeDtypeStruct((rows, LANES), I32),
            jax.ShapeDtypeStruct((rows, LANES), F32),
            jax.ShapeDtypeStruct((8, LANES), F32),
        ],
        scratch_shapes=[pltpu.VMEM((8, LANES), F32)],
        compiler_params=_cparams(1),
        name="mix_router",
    )(o_gdn, o_sb, h0, w_out, norm_w, router_w, router_b)


def _dispatch_kernel(pstart_ref, padded_ref, used_ref, slot_ref, hnp_ref, xs_ref, zbuf, sem, zsem,
                     *, pack_rows, front, blocks_per_batch, n_slots):
    tm = DISPATCH_ROWS
    ts = EXP_SUB
    i = pl.program_id(0)
    n_exp = pstart_ref.shape[0]

    @pl.when(i == 0)
    def _():
        zbuf[...] = jnp.zeros_like(zbuf)

        def zero_copy(row):
            return pltpu.make_async_copy(
                zbuf, xs_ref.at[pl.ds(row * pack_rows, ts * pack_rows), :], zsem)

        def pad_block(e, _, *, start):
            @pl.when(padded_ref[e] > 0)
            def _():
                cp = zero_copy(pstart_ref[e] + padded_ref[e] - ts)
                cp.start() if start else cp.wait()
            return 0

        used = used_ref[0]
        n_tail = (n_slots - used) // ts
        lax.fori_loop(0, n_exp, functools.partial(pad_block, start=True), 0)
        lax.fori_loop(0, n_tail, lambda j, c: (zero_copy(used + j * ts).start(), c)[1], 0)
        lax.fori_loop(0, n_exp, functools.partial(pad_block, start=False), 0)
        lax.fori_loop(0, n_tail, lambda j, c: (zero_copy(used + j * ts).wait(), c)[1], 0)

    def copy(tok, j):
        src = pl.multiple_of(tok * pack_rows, pack_rows)
        return pltpu.make_async_copy(
            hnp_ref.at[pl.ds(src, pack_rows), :],
            xs_ref.at[pl.ds(slot_ref[0, tok * TOP_K + j] * pack_rows, pack_rows), :], sem)

    def start(tok, _):
        for j in range(TOP_K):
            copy(tok, j).start(priority=j % 2)
        return 0

    def wait_all(n_tok):
        for _ in range(TOP_K):
            pltpu.make_async_copy(
                hnp_ref.at[pl.ds(0, n_tok * pack_rows), :],
                xs_ref.at[pl.ds(0, n_tok * pack_rows), :], sem).wait()

    @pl.when(i % blocks_per_batch == 0)
    def _():
        lax.fori_loop(front, tm, start, 0)
        wait_all(tm - front)

    @pl.when(i % blocks_per_batch != 0)
    def _():
        lax.fori_loop(0, tm, start, 0)
        wait_all(tm)


def _dispatch(slots, hnp, pstart, padded, used_rows, n_slots, pack_rows, front, lp):
    rows = slots.shape[0]
    tm = DISPATCH_ROWS
    slots3 = slots.reshape(rows // tm, 1, tm * TOP_K)
    grid_spec = pltpu.PrefetchScalarGridSpec(
        num_scalar_prefetch=3,
        grid=(rows // tm,),
        in_specs=[
            pl.BlockSpec((None, 1, tm * TOP_K), lambda i, *_: (i, 0, 0), memory_space=pltpu.SMEM),
            pl.BlockSpec((tm * pack_rows, LANES), lambda i, *_: (i, 0)),
        ],
        out_specs=pl.BlockSpec(memory_space=pl.ANY),
        scratch_shapes=[pltpu.VMEM((EXP_SUB * pack_rows, LANES), U32),
                        pltpu.SemaphoreType.DMA(()), pltpu.SemaphoreType.DMA(())],
    )
    return pl.pallas_call(
        functools.partial(_dispatch_kernel, pack_rows=pack_rows, front=front,
                          blocks_per_batch=lp // tm, n_slots=n_slots),
        grid_spec=grid_spec,
        out_shape=jax.ShapeDtypeStruct((n_slots * pack_rows, LANES), U32),
        compiler_params=_cparams(1),
        name="dispatch",
    )(pstart, padded, used_rows, slots3, hnp)


def _expert_kernel(ue_ref, ur_ref, un_ref, used_ref, xs_ref, wg_ref, wu_ref, bg_ref, bu_ref, wd_ref,
                   bd_ref, ys_ref, xbuf, xb, acc, ystage, sem_in, sem_out,
                   *, pack_rows, n_slots):
    del ue_ref
    ts = EXP_SUB
    u = pl.program_id(0)
    f = pl.program_id(1)
    n_f = pl.num_programs(1)
    nsub = un_ref[u]
    row0 = ur_ref[u]
    d = xb.shape[1]
    half = d // 2

    @pl.when((u == 0) & (f == 0))
    def _():
        ystage[...] = jnp.zeros_like(ystage)
        used = used_ref[0]

        def tail_copy(i):
            return pltpu.make_async_copy(
                ystage.at[0], ys_ref.at[pl.ds((used + i * ts) * pack_rows, ts * pack_rows), :],
                sem_out.at[0])

        n_tail = (n_slots - used) // ts
        lax.fori_loop(0, n_tail, lambda i, c: (tail_copy(i).start(), c)[1], 0)
        lax.fori_loop(0, n_tail, lambda i, c: (tail_copy(i).wait(), c)[1], 0)

    def in_copy(unit_row0, s):
        return pltpu.make_async_copy(
            xs_ref.at[pl.ds((unit_row0 + s * ts) * pack_rows, ts * pack_rows), :],
            xbuf.at[s], sem_in.at[s])

    def start_loads(unit):
        unit_row0 = ur_ref[unit]
        lax.fori_loop(0, un_ref[unit], lambda s, c: (in_copy(unit_row0, s).start(), c)[1], 0)

    @pl.when((u == 0) & (f == 0))
    def _():
        start_loads(0)

    @pl.when((f == 0) & (nsub > 0))
    def _():
        def load(s, _):
            in_copy(row0, s).wait()
            words = jnp.concatenate(
                [xbuf[s, pl.ds(c, ts, stride=pack_rows), :] for c in range(pack_rows)], axis=1)
            r = pl.multiple_of(s * ts, ts)
            xb[pl.ds(r, ts), :half] = pltpu.bitcast(words << 16, F32).astype(BF16)
            xb[pl.ds(r, ts), half:] = pltpu.bitcast(words & jnp.uint32(HIGH16), F32).astype(BF16)
            acc[pl.ds(r, ts), :] = jnp.broadcast_to(bd_ref[...], (ts, d))
            return 0
        lax.fori_loop(0, nsub, load, 0)

        @pl.when(u + 1 < pl.num_programs(0))
        def _():
            start_loads(jnp.minimum(u + 1, pl.num_programs(0) - 1))

    @pl.when(nsub > 0)
    def _():
        def ffn_rows(r, m):
            x = xb[pl.ds(r, m), :]
            g = jnp.minimum(_dot(x, wg_ref[...].astype(BF16)) + bg_ref[...], SWIGLU_LIMIT)
            up = jnp.clip(_dot(x, wu_ref[...].astype(BF16)) + bu_ref[...],
                          -SWIGLU_LIMIT, SWIGLU_LIMIT)
            act = ((up + 1.0) * g * _sigmoid(SWIGLU_ALPHA * g)).astype(BF16)
            acc[pl.ds(r, m), :] += _dot(act, wd_ref[...].astype(BF16))

        group = EXP_GROUP

        def trip(p, _):
            ffn_rows(pl.multiple_of(p * (group * ts), ts), group * ts)
            return 0
        lax.fori_loop(0, nsub // group, trip, 0)
        for k in range(1, group):
            @pl.when(nsub % group == k)
            def _():
                ffn_rows(pl.multiple_of(nsub // group * (group * ts), ts), k * ts)

    n_stage = ystage.shape[0]

    def out_copy(s):
        return pltpu.make_async_copy(
            ystage.at[s % n_stage],
            ys_ref.at[pl.ds((row0 + s * ts) * pack_rows, ts * pack_rows), :],
            sem_out.at[s % n_stage])

    @pl.when((f == n_f - 1) & (nsub > 0))
    def _():
        def store(s, _):
            @pl.when(s >= n_stage)
            def _():
                out_copy(s - n_stage).wait()
            y = acc[pl.ds(pl.multiple_of(s * ts, ts), ts), :]
            lo = pltpu.bitcast(y[:, :half].astype(BF16).astype(F32), U32) >> 16
            hi = pltpu.bitcast(y[:, half:].astype(BF16).astype(F32), U32) & jnp.uint32(HIGH16)
            word = lo | hi
            for c in range(pack_rows):
                ystage[s % n_stage, pl.ds(c, ts, stride=pack_rows), :] = (
                    word[:, c * LANES:(c + 1) * LANES])
            out_copy(s).start()
            return 0
        lax.fori_loop(0, nsub, store, 0)
        for k in range(n_stage, 0, -1):
            @pl.when(nsub >= k)
            def _():
                out_copy(nsub - k).wait()


def _experts(unit_e, unit_row0, unit_nsub, used_rows, xs, w_gate_up, b_gate_up, w_down, b_down,
             n_slots, pack_rows):
    n_exp, d, two_de = w_gate_up.shape
    de = two_de // 2
    tf = EXP_FTILE
    n_f = de // tf
    n_units = unit_e.shape[0]
    last_f = n_f - 1

    def fidx(u, f, un):
        return jnp.where(un[u] > 0, f, last_f)

    grid_spec = pltpu.PrefetchScalarGridSpec(
        num_scalar_prefetch=4,
        grid=(n_units, n_f),
        in_specs=[
            pl.BlockSpec(memory_space=pl.ANY),
            pl.BlockSpec((None, d, tf), lambda u, f, ue, ur, un, us: (ue[u], 0, fidx(u, f, un))),
            pl.BlockSpec((None, d, tf),
                         lambda u, f, ue, ur, un, us: (ue[u], 0, n_f + fidx(u, f, un))),
            pl.BlockSpec((None, 1, tf), lambda u, f, ue, ur, un, us: (ue[u], 0, fidx(u, f, un))),
            pl.BlockSpec((None, 1, tf),
                         lambda u, f, ue, ur, un, us: (ue[u], 0, n_f + fidx(u, f, un))),
            pl.BlockSpec((None, tf, d), lambda u, f, ue, ur, un, us: (ue[u], fidx(u, f, un), 0)),
            pl.BlockSpec((None, 1, d), lambda u, f, ue, ur, un, us: (ue[u], 0, 0)),
        ],
        out_specs=pl.BlockSpec(memory_space=pl.ANY),
        scratch_shapes=[
            pltpu.VMEM((EXP_UNIT // EXP_SUB, EXP_SUB * pack_rows, LANES), U32),
            pltpu.VMEM((EXP_UNIT, d), BF16),
            pltpu.VMEM((EXP_UNIT, d), F32),
            pltpu.VMEM((EXP_OUT_STAGES, EXP_SUB * pack_rows, LANES), U32),
            pltpu.SemaphoreType.DMA((EXP_UNIT // EXP_SUB,)),
            pltpu.SemaphoreType.DMA((EXP_OUT_STAGES,)),
        ],
    )
    return pl.pallas_call(
        functools.partial(_expert_kernel, pack_rows=pack_rows, n_slots=n_slots),
        grid_spec=grid_spec,
        out_shape=jax.ShapeDtypeStruct((n_slots * pack_rows, LANES), U32),
        compiler_params=_cparams(2, vmem=EXPERT_VMEM_LIMIT),
        name="experts",
    )(unit_e, unit_row0, unit_nsub, used_rows, xs, w_gate_up, w_gate_up,
      b_gate_up.reshape(n_exp, 1, two_de), b_gate_up.reshape(n_exp, 1, two_de),
      w_down, b_down.reshape(n_exp, 1, d))


def _combine_kernel(slot_ref, gate_ref, h1_ref, ys_ref, o_ref, ybuf, sem, *, pack_rows):
    tc = COMB_ROWS
    half = h1_ref.shape[1] // 2

    def copy(tok, j):
        s = slot_ref[0, tok * TOP_K + j]
        dst = pl.multiple_of(tok * pack_rows, pack_rows)
        return pltpu.make_async_copy(
            ys_ref.at[pl.ds(s * pack_rows, pack_rows), :],
            ybuf.at[j, pl.ds(dst, pack_rows), :], sem)

    def start(tok, _):
        for j in range(TOP_K):
            copy(tok, j).start(priority=j % 2)
        return 0

    lax.fori_loop(0, tc, start, 0)
    for j in range(TOP_K):
        pltpu.make_async_copy(ys_ref.at[pl.ds(0, tc * pack_rows), :], ybuf.at[j], sem).wait()

    gates = gate_ref[...]
    for c in range(pack_rows):
        lo = h1_ref[:, c * LANES:(c + 1) * LANES]
        hi = h1_ref[:, half + c * LANES:half + (c + 1) * LANES]
        for j in range(TOP_K):
            word = ybuf[j, pl.ds(c, tc, stride=pack_rows), :]
            gate = gates[:, j:j + 1]
            lo = lo + gate * pltpu.bitcast(word << 16, F32)
            hi = hi + gate * pltpu.bitcast(word & jnp.uint32(HIGH16), F32)
        o_ref[:, c * LANES:(c + 1) * LANES] = lo
        o_ref[:, half + c * LANES:half + (c + 1) * LANES] = hi


def _combine(slots, gates, h1, ys, bsz, seq, lp, d):
    tc = COMB_ROWS
    rows = slots.shape[0]
    pack_rows = d // 2 // LANES
    slots3 = slots.reshape(rows // tc, 1, tc * TOP_K)
    nb = lp // tc
    first = (lp - seq) // tc
    return pl.pallas_call(
        functools.partial(_combine_kernel, pack_rows=pack_rows),
        grid=(bsz, seq // tc),
        in_specs=[
            pl.BlockSpec((None, 1, tc * TOP_K), lambda b, i: (b * nb + first + i, 0, 0),
                         memory_space=pltpu.SMEM),
            pl.BlockSpec((tc, LANES), lambda b, i: (b * nb + first + i, 0)),
            pl.BlockSpec((tc, d), lambda b, i: (b * nb + first + i, 0)),
            pl.BlockSpec(memory_space=pl.ANY),
        ],
        out_specs=pl.BlockSpec((None, tc, d), lambda b, i: (b, i, 0)),
        out_shape=jax.ShapeDtypeStruct((bsz, seq, d), F32),
        scratch_shapes=[pltpu.VMEM((TOP_K, tc * pack_rows, LANES), U32), pltpu.SemaphoreType.DMA(())],
        compiler_params=_cparams(2),
        name="combine",
    )(slots3, gates, h1, ys)


def _pick(n, candidates):
    for c in candidates:
        if n % c == 0:
            return c
    raise ValueError(f"no block size in {candidates} divides {n}")


def _plan(info, cnt, n_exp, rows, lp, front, n_assign):
    counts = cnt[0, :n_exp].astype(I32)
    padded = (counts + EXP_SUB - 1) // EXP_SUB * EXP_SUB
    pstart = jnp.cumsum(padded) - padded
    eid = info[:, :TOP_K]
    rank = info[:, TOP_K:2 * TOP_K]
    onehot = eid[:, :, None] == jnp.arange(n_exp, dtype=I32)[None, None, :]
    slot = rank + jnp.sum(jnp.where(onehot, pstart[None, None, :], 0), axis=-1)
    row_valid = (jnp.arange(rows, dtype=I32) % lp) >= front
    slots = jnp.where(row_valid[:, None], slot, -1).astype(I32)

    n_slots = (n_assign + n_exp * (EXP_SUB - 1)) // EXP_SUB * EXP_SUB
    units_per_e = (padded + EXP_UNIT - 1) // EXP_UNIT
    cum_units = jnp.cumsum(units_per_e)
    n_units = n_slots // EXP_UNIT + n_exp
    uidx = jnp.arange(n_units, dtype=I32)
    ue = jnp.sum(cum_units[None, :] <= uidx[:, None], axis=1).astype(I32)
    live = ue < n_exp
    last_e = jnp.max(jnp.where(counts > 0, jnp.arange(n_exp, dtype=I32), 0))
    ue = jnp.where(live, ue, last_e)
    k_in_e = uidx - (cum_units - units_per_e)[ue]
    nsub_e = (padded // EXP_SUB)[ue]
    units_e = jnp.maximum(units_per_e[ue], 1)
    base = nsub_e // units_e
    extra = nsub_e - base * units_e
    first_sub = k_in_e * base + jnp.minimum(k_in_e, extra)
    unit_row0 = jnp.where(live, pstart[ue] + first_sub * EXP_SUB, 0).astype(I32)
    unit_nsub = jnp.where(live, base + (k_in_e < extra), 0).astype(I32)
    used_rows = jnp.sum(padded).astype(I32)[None]
    return slots, (pstart.astype(I32), padded.astype(I32)), (ue, unit_row0, unit_nsub, used_rows), n_slots


def kernel(x, meta_tokens, mix_norm_w, w_in, conv_w, a_log, dt_bias, gdn_norm_w, sb_q_norm_w,
           sb_k_norm_w, sb_out_norm_w, w_out, ffn_norm_w, router_w, router_b, w_gate_up, b_gate_up,
           w_down, b_down):
    bsz, seq, d = x.shape
    n_meta = meta_tokens.shape[0]
    depth = mix_norm_w.shape[0]
    heads = a_log.shape[1]
    hd = gdn_norm_w.shape[1]
    gd = heads * hd
    sbd = (w_in.shape[2] - 4 * gd - 2 * heads) // 3
    sb_heads = sbd // hd
    n_exp = router_w.shape[2]
    assert seq % ROW_ALIGN == 0 and d % (2 * 8 * LANES) == 0 and hd == LANES
    assert 2 * heads <= LANES and n_exp <= LANES and sb_heads == heads
    assert depth == 1, "a second layer would need the meta rows carried through the combine stage"
    front = (-n_meta) % ROW_ALIGN
    lp = front + n_meta + seq
    rows = bsz * lp
    pack_rows = d // 2 // LANES

    h = jnp.concatenate([
        jnp.zeros((bsz, front, d), x.dtype),
        jnp.broadcast_to(meta_tokens.astype(x.dtype)[None], (bsz, n_meta, d)),
        x], axis=1).reshape(rows, d)

    wl = w_in[0]
    n_ba = 4 * gd
    w_main = jnp.concatenate([wl[:, :n_ba], wl[:, n_ba + 2 * heads:]], axis=1).astype(BF16)
    w_ba = jnp.pad(wl[:, n_ba:n_ba + 2 * heads], ((0, 0), (0, LANES - 2 * heads))).astype(BF16)
    tm = _pick(rows, (1024, 512, 256))
    tn = _pick(w_main.shape[1], (1792, 1024, 512, 256, 128))
    proj, ba = _in_proj(h, mix_norm_w[0][None], w_main, w_ba, tm, tn)
    proj3 = proj.reshape(bsz, lp, -1)
    ba3 = ba.reshape(bsz, lp, LANES)

    gparams = jnp.zeros((2, LANES), F32)
    gparams = gparams.at[0, heads:2 * heads].set(a_log[0].astype(F32))
    gparams = gparams.at[1, heads:2 * heads].set(dt_bias[0].astype(F32))
    qkv3, bg3 = _gdn_prep(proj3, ba3, conv_w[0].astype(F32), gparams, heads, hd)
    o_gdn = _gdn(qkv3, proj3, bg3, gdn_norm_w[0][None], heads, hd)
    o_sb = _sbk(proj3, sb_q_norm_w[0][None], sb_k_norm_w[0][None], sb_out_norm_w[0][None],
               sb_heads, hd, 4 * gd, front)

    rw = jnp.pad(router_w[0].astype(F32), ((0, 0), (0, LANES - n_exp)))
    rb = jnp.pad(router_b[0].astype(F32), (0, LANES - n_exp), constant_values=-1e30)[None]
    h1, hnp, info, gates, cnt = _mix(
        o_gdn.reshape(rows, gd), o_sb.reshape(rows, sbd), h, w_out[0].astype(BF16),
        ffn_norm_w[0][None], rw, rb, front, lp)

    slots, (pstart, padded), units, n_slots = _plan(
        info, cnt, n_exp, rows, lp, front, bsz * (n_meta + seq) * TOP_K)
    xs = _dispatch(slots, hnp, pstart, padded, units[3], n_slots, pack_rows, front, lp)
    ys = _experts(*units, xs, w_gate_up[0], b_gate_up[0], w_down[0], b_down[0], n_slots, pack_rows)
    return _combine(slots, gates, h1, ys, bsz, seq, lp, d)
```
